```python
import jax, jax.numpy as jnp
from jax import lax
import numpy as np

D_MODEL = 2048
BATCH = 16
SEQ = 256
DEPTH = 2
DEC_BATCH = 2
DEC_SEQ = 2048
PAST_LEN = 512

GRID_W = 64
N_HEADS = 16
HEAD_DIM = D_MODEL // N_HEADS
WIN_ROWS = 8
WIN_COLS = 16
Q_BLOCK = 128
CONV_WIDTH = 3
N_EXPERTS = 64
TOP_K = 8
N_GROUPS = 8
TOPK_GROUPS = 4
D_EXPERT = D_MODEL // 4
D_SHARED = D_EXPERT
ROUTED_SCALE = 2.5
N_MIXERS = 2
N_ATTN_LAYERS = (DEPTH + 1) // 2
N_CONV_LAYERS = DEPTH // 2
N_MOD = 6
RMS_EPS = 1e-6
NEG_INF = -1e30

kernel_name = "hybrid_natten_shortconv_moe_dit_step"


def rms_norm(x, g):
    xf = x.astype(jnp.float32)
    y = xf * lax.rsqrt(jnp.mean(xf * xf, axis=-1, keepdims=True) + RMS_EPS)
    return (y * g.astype(jnp.float32)).astype(x.dtype)


def adaln(cond, w_ada, b_ada):
    mod = jax.nn.silu(cond) @ w_ada + b_ada
    return jnp.split(mod[:, None, :], N_MOD, axis=-1)


def modulate(x, g, shift, scale):
    return rms_norm(x, g) * (1 + scale) + shift


def split_heads(qkv):
    b, n, _ = qkv.shape
    q, k, v = jnp.split(qkv, 3, axis=-1)
    shp = (b, n, N_HEADS, HEAD_DIM)
    return q.reshape(shp), k.reshape(shp), v.reshape(shp)


def context_attention(q, k, v):
    b, s, h, dh = q.shape
    qb = q.reshape(b, s // Q_BLOCK, Q_BLOCK, h, dh).swapaxes(0, 1)

    def block(qi):
        logits = jnp.einsum('bqhd,bkhd->bhqk', qi, k).astype(jnp.float32) * (dh ** -0.5)
        p = jax.nn.softmax(logits, axis=-1).astype(v.dtype)
        return jnp.einsum('bhqk,bkhd->bqhd', p, v)

    out = lax.map(block, qb)
    return out.swapaxes(0, 1).reshape(b, s, h * dh)


def neighbourhood_attention(q, k, v, k_ctx, v_ctx, rpb):
    b, n, h, dh = q.shape
    rows = n // GRID_W
    kh = min(WIN_ROWS, rows)
    n_loc = kh * GRID_W
    q = q.reshape(b, rows, GRID_W, h, dh)
    k = k.reshape(b, rows, GRID_W, h, dh)
    v = v.reshape(b, rows, GRID_W, h, dh)
    col = np.arange(GRID_W)
    col_start = np.clip(col - WIN_COLS // 2, 0, GRID_W - WIN_COLS)
    col_mask = jnp.asarray((col[None, :] >= col_start[:, None]) & (col[None, :] < col_start[:, None] + WIN_COLS))
    dc_idx = jnp.asarray(np.clip(col[None, :] - col[:, None] + WIN_COLS - 1, 0, 2 * WIN_COLS - 2))
    scale = dh ** -0.5

    def one_row(r):
        r0 = jnp.clip(r - kh // 2, 0, rows - kh)
        qr = lax.dynamic_index_in_dim(q, r, axis=1, keepdims=False)
        kr = lax.dynamic_slice_in_dim(k, r0, kh, axis=1)
        vr = lax.dynamic_slice_in_dim(v, r0, kh, axis=1).reshape(b, n_loc, h, dh)
        dr_idx = r0 + jnp.arange(kh) - r + (WIN_ROWS - 1)
        bias = rpb[:, dr_idx][:, :, dc_idx].transpose(0, 2, 1, 3)
        s_loc = jnp.einsum('bqhd,bikhd->bhqik', qr, kr).astype(jnp.float32) * scale + bias[None].astype(jnp.float32)
        s_loc = jnp.where(col_mask[None, None, :, None, :], s_loc, NEG_INF)
        s_ctx = jnp.einsum('bqhd,bphd->bhqp', qr, k_ctx).astype(jnp.float32) * scale
        logits = jnp.concatenate([s_loc.reshape(b, h, GRID_W, n_loc), s_ctx], axis=-1)
        p = jax.nn.softmax(logits, axis=-1).astype(v.dtype)
        o_loc = jnp.einsum('bhqk,bkhd->bqhd', p[..., :n_loc], vr)
        o_ctx = jnp.einsum('bhqp,bphd->bqhd', p[..., n_loc:], v_ctx)
        return o_loc + o_ctx

    out = lax.map(one_row, jnp.arange(rows, dtype=jnp.int32))
    return out.swapaxes(0, 1).reshape(b, n, h * dh)


def short_conv(x, w_in, conv_w, w_out):
    b_gate, c_gate, u = jnp.split(x @ w_in, 3, axis=-1)
    u = c_gate * u
    n = u.shape[1]
    pad = CONV_WIDTH // 2
    up = jnp.pad(u, ((0, 0), (pad, pad), (0, 0)))
    conv = sum(conv_w[j] * up[:, j:j + n] for j in range(CONV_WIDTH))
    return (b_gate * conv) @ w_out


def moe_ffn(x, w_router, b_router, w_gate, w_up, w_down, w_sh_gate, w_sh_up, w_sh_down):
    shp = x.shape
    xt = x.reshape(-1, shp[-1])
    scores = jax.nn.sigmoid((xt @ w_router).astype(jnp.float32))
    sel = scores + b_router.astype(jnp.float32)
    grp = sel.reshape(-1, N_GROUPS, N_EXPERTS // N_GROUPS)
    grp_score = jnp.sum(lax.top_k(grp, 2)[0], axis=-1)
    _, grp_idx = lax.top_k(grp_score, TOPK_GROUPS)
    grp_mask = jnp.sum(jax.nn.one_hot(grp_idx, N_GROUPS, dtype=jnp.float32), axis=-2)
    exp_mask = jnp.repeat(grp_mask, N_EXPERTS // N_GROUPS, axis=-1) > 0
    _, top_idx = lax.top_k(jnp.where(exp_mask, sel, NEG_INF), TOP_K)
    w = jnp.take_along_axis(scores, top_idx, axis=-1)
    w = w / jnp.sum(w, axis=-1, keepdims=True) * ROUTED_SCALE
    gates = jnp.sum(jax.nn.one_hot(top_idx, N_EXPERTS, dtype=jnp.float32) * w[..., None], axis=-2).astype(x.dtype)
    h = jax.nn.silu(jnp.einsum('td,edf->tef', xt, w_gate)) * jnp.einsum('td,edf->tef', xt, w_up)
    routed = jnp.einsum('tef,efd->td', h * gates[..., None], w_down)
    shared = (jax.nn.silu(xt @ w_sh_gate) * (xt @ w_sh_up)) @ w_sh_down
    return (routed + shared).reshape(shp)


def setup_inputs(seed: int = 0) -> dict:
    key = jax.random.key(seed)
    ks = jax.random.split(key, 26)
    D = D_MODEL

    def nrm(k, shape, s):
        return jax.random.normal(k, shape, jnp.float32) * s

    return {
        "x_prompt": nrm(ks[0], (BATCH, SEQ, D), 1.0),
        "x_sample": nrm(ks[1], (DEC_BATCH, DEC_SEQ, D), 1.0),
        "cache_k": nrm(ks[2], (DEC_BATCH, N_ATTN_LAYERS, PAST_LEN, N_HEADS, HEAD_DIM), 1.0),
        "cache_v": nrm(ks[3], (DEC_BATCH, N_ATTN_LAYERS, PAST_LEN, N_HEADS, HEAD_DIM), 1.0),
        "c": nrm(ks[4], (DEC_BATCH, D), 1.0),
        "c_ctx": nrm(ks[5], (D,), 1.0),
        "w_ada": nrm(ks[6], (DEPTH, D, N_MOD * D), 0.5 * D ** -0.5),
        "b_ada": nrm(ks[7], (DEPTH, N_MOD * D), 0.01),
        "norm_mix_pre": 1.0 + nrm(ks[8], (DEPTH, D), 0.05),
        "norm_mix_post": 1.0 + nrm(ks[9], (DEPTH, D), 0.05),
        "norm_ffn_pre": 1.0 + nrm(ks[10], (DEPTH, D), 0.05),
        "norm_ffn_post": 1.0 + nrm(ks[11], (DEPTH, D), 0.05),
        "w_qkv": nrm(ks[12], (N_ATTN_LAYERS, D, 3 * D), D ** -0.5),
        "w_o_attn": nrm(ks[13], (N_ATTN_LAYERS, D, D), D ** -0.5),
        "rpb": nrm(ks[14], (N_ATTN_LAYERS, N_HEADS, 2 * WIN_ROWS - 1, 2 * WIN_COLS - 1), 0.1),
        "w_conv_in": nrm(ks[15], (N_CONV_LAYERS, D, 3 * D), D ** -0.5),
        "conv_w": nrm(ks[16], (N_CONV_LAYERS, CONV_WIDTH, D), 0.5),
        "w_conv_out": nrm(ks[17], (N_CONV_LAYERS, D, D), D ** -0.5),
        "w_router": nrm(ks[18], (DEPTH, D, N_EXPERTS), D ** -0.5),
        "b_router": nrm(ks[19], (DEPTH, N_EXPERTS), 0.01),
        "w_exp_gate": nrm(ks[20], (DEPTH, N_EXPERTS, D, D_EXPERT), D ** -0.5),
        "w_exp_up": nrm(ks[21], (DEPTH, N_EXPERTS, D, D_EXPERT), D ** -0.5),
        "w_exp_down": nrm(ks[22], (DEPTH, N_EXPERTS, D_EXPERT, D), D_EXPERT ** -0.5),
        "w_sh_gate": nrm(ks[23], (DEPTH, D, D_SHARED), D ** -0.5),
        "w_sh_up": nrm(ks[24], (DEPTH, D, D_SHARED), D ** -0.5),
        "w_sh_down": nrm(ks[25], (DEPTH, D_SHARED, D), D_SHARED ** -0.5),
    }


def reference(x_prompt, x_sample, cache_k, cache_v, c, c_ctx, w_ada, b_ada,
              norm_mix_pre, norm_mix_post, norm_ffn_pre, norm_ffn_post,
              w_qkv, w_o_attn, rpb, w_conv_in, conv_w, w_conv_out,
              w_router, b_router, w_exp_gate, w_exp_up, w_exp_down,
              w_sh_gate, w_sh_up, w_sh_down):
    xp, xs = x_prompt, x_sample
    new_k, new_v = [], []
    for l in range(DEPTH):
        sh_mp, sc_mp, g_mp, sh_fp, sc_fp, g_fp = adaln(c_ctx[None, :], w_ada[l], b_ada[l])
        sh_ms, sc_ms, g_ms, sh_fs, sc_fs, g_fs = adaln(c, w_ada[l], b_ada[l])
        hp = modulate(xp, norm_mix_pre[l], sh_mp, sc_mp)
        hs = modulate(xs, norm_mix_pre[l], sh_ms, sc_ms)
        if l % N_MIXERS == 0:
            a = l // N_MIXERS
            qp, kp, vp = split_heads(hp @ w_qkv[a])
            qs, ks_, vs = split_heads(hs @ w_qkv[a])
            new_k.append(kp)
            new_v.append(vp)
            op = context_attention(qp, kp, vp) @ w_o_attn[a]
            os_ = neighbourhood_attention(qs, ks_, vs, cache_k[:, a], cache_v[:, a], rpb[a]) @ w_o_attn[a]
        else:
            m = l // N_MIXERS
            op = short_conv(hp, w_conv_in[m], conv_w[m], w_conv_out[m])
            os_ = short_conv(hs, w_conv_in[m], conv_w[m], w_conv_out[m])
        xp = xp + g_mp * rms_norm(op, norm_mix_post[l])
        xs = xs + g_ms * rms_norm(os_, norm_mix_post[l])
        hp = modulate(xp, norm_ffn_pre[l], sh_fp, sc_fp)
        hs = modulate(xs, norm_ffn_pre[l], sh_fs, sc_fs)
        fp = moe_ffn(hp, w_router[l], b_router[l], w_exp_gate[l], w_exp_up[l], w_exp_down[l],
                     w_sh_gate[l], w_sh_up[l], w_sh_down[l])
        fs = moe_ffn(hs, w_router[l], b_router[l], w_exp_gate[l], w_exp_up[l], w_exp_down[l],
                     w_sh_gate[l], w_sh_up[l], w_sh_down[l])
        xp = xp + g_fp * rms_norm(fp, norm_ffn_post[l])
        xs = xs + g_fs * rms_norm(fs, norm_ffn_post[l])
    y_prompt = xp
    y_sample = xs
    new_cache_k = jnp.stack(new_k, axis=1)
    new_cache_v = jnp.stack(new_v, axis=1)
    return (y_prompt, y_sample, new_cache_k, new_cache_v)
```

```python
import functools

import jax
import jax.numpy as jnp
from jax import lax
from jax.experimental import pallas as pl
from jax.experimental.pallas import tpu as pltpu

N_HEADS = 16
GRID_W = 64
WIN_ROWS = 8
WIN_COLS = 16
CONV_WIDTH = 3
N_EXPERTS = 64
TOP_K = 8
N_GROUPS = 8
TOPK_GROUPS = 4
ROUTED_SCALE = 2.5
N_MOD = 6
RMS_EPS = 1e-6
NEG_INF = -1e30

LANES = 128
SUBLANES = 8
VMEM_LIMIT = 56 * 1024 * 1024

MOD_SHIFT_MIX, MOD_SCALE_MIX, MOD_GATE_MIX, MOD_SHIFT_FFN, MOD_SCALE_FFN, MOD_GATE_FFN = range(6)
MOD_ROWS = 8

ROW_TILE = 256
EXPERT_TILE = 256
MM_TILE_M = 2048
MM_TILE_N = 512
GATHER_TILE = 128
SCATTER_TILE = 512

_f32 = jnp.float32
_bf16 = jnp.bfloat16


def _params(sem, vmem=VMEM_LIMIT):
    return pltpu.CompilerParams(dimension_semantics=sem, vmem_limit_bytes=vmem)


def _rms(x, g):
    return x * lax.rsqrt(jnp.mean(x * x, axis=-1, keepdims=True) + RMS_EPS) * g


def _dot(a, b):
    return jnp.dot(a, b, preferred_element_type=_f32)


def _dot_nt(a, b):
    return lax.dot_general(a, b, (((1,), (1,)), ((), ())), preferred_element_type=_f32)


def _cond_row(row0, n_ctx, dec_seq):
    return jnp.where(row0 < n_ctx, 0, 1 + (row0 - n_ctx) // dec_seq)


def _adaln_kernel(cb_ref, w_ref, b_ref, o_ref, acc_ref, *, n_cond):
    k = pl.program_id(2)

    @pl.when(k == 0)
    def _():
        acc_ref[...] = jnp.zeros_like(acc_ref)

    tk, tn = w_ref.shape
    for r in range(n_cond):
        s = cb_ref[r]
        s = s * jax.nn.sigmoid(s)
        for c in range(tn // LANES):
            p = w_ref[:, c * LANES:(c + 1) * LANES] * s
            acc_ref[r, :, c * LANES:(c + 1) * LANES] += p.reshape(tk // SUBLANES, SUBLANES, LANES).sum(axis=0)

    @pl.when(k == pl.num_programs(2) - 1)
    def _():
        o_ref[...] = jnp.zeros_like(o_ref)
        for r in range(n_cond):
            o_ref[pl.ds(r, 1), :] = acc_ref[r].sum(axis=0, keepdims=True) + b_ref[...]


def _adaln(cond, w_ada, b_ada):
    n_cond, d = cond.shape
    n_layers, _, n6 = w_ada.shape
    tk, tn = min(512, d), min(2048, n6)
    cb = jnp.broadcast_to(cond[:, :, None], (n_cond, d, LANES))
    mod = pl.pallas_call(
        functools.partial(_adaln_kernel, n_cond=n_cond),
        grid=(n_layers, n6 // tn, d // tk),
        in_specs=[
            pl.BlockSpec((n_cond, tk, LANES), lambda l, n, k: (0, k, 0)),
            pl.BlockSpec((None, tk, tn), lambda l, n, k: (l, k, n)),
            pl.BlockSpec((None, 1, tn), lambda l, n, k: (l, 0, n)),
        ],
        out_specs=pl.BlockSpec((None, MOD_ROWS, tn), lambda l, n, k: (l, 0, n)),
        out_shape=jax.ShapeDtypeStruct((n_layers, MOD_ROWS, n6), _f32),
        scratch_shapes=[pltpu.VMEM((n_cond, SUBLANES, tn), _f32)],
        compiler_params=_params(("arbitrary", "arbitrary", "arbitrary")),
        name="adaln",
    )(cb, w_ada, b_ada.reshape(n_layers, 1, n6))
    mod = mod[:, :n_cond].reshape(n_layers, n_cond, N_MOD, d)
    return jnp.pad(mod, ((0, 0), (0, 0), (0, MOD_ROWS - N_MOD), (0, 0)))


def _modulate_kernel(x_ref, g_ref, mod_ref, o_ref):
    y = _rms(x_ref[...], g_ref[...])
    h = y * (1.0 + mod_ref[pl.ds(MOD_SCALE_MIX, 1), :]) + mod_ref[pl.ds(MOD_SHIFT_MIX, 1), :]
    o_ref[...] = h.astype(o_ref.dtype)


def _modulate(x, g, mod, n_ctx, dec_seq):
    t, d = x.shape
    tm = ROW_TILE
    return pl.pallas_call(
        _modulate_kernel,
        grid=(t // tm,),
        in_specs=[
            pl.BlockSpec((tm, d), lambda i: (i, 0)),
            pl.BlockSpec((1, d), lambda i: (0, 0)),
            pl.BlockSpec((None, MOD_ROWS, d), lambda i: (_cond_row(i * tm, n_ctx, dec_seq), 0, 0)),
        ],
        out_specs=pl.BlockSpec((tm, d), lambda i: (i, 0)),
        out_shape=jax.ShapeDtypeStruct((t, d), _bf16),
        compiler_params=_params(("parallel",)),
        name="modulate",
    )(x, g.reshape(1, d), mod)


def _matmul_kernel(a_ref, w_ref, o_ref):
    o_ref[...] = _dot(a_ref[...], w_ref[...].astype(_bf16)).astype(o_ref.dtype)


def _matmul(a, w):
    n_rows = a.shape[0]
    k, n = w.shape
    tm, tn = min(MM_TILE_M, n_rows), min(MM_TILE_N, n)
    return pl.pallas_call(
        _matmul_kernel,
        grid=(n_rows // tm, n // tn),
        in_specs=[
            pl.BlockSpec((tm, k), lambda i, j: (i, 0)),
            pl.BlockSpec((k, tn), lambda i, j: (0, j)),
        ],
        out_specs=pl.BlockSpec((tm, tn), lambda i, j: (i, j)),
        out_shape=jax.ShapeDtypeStruct((n_rows, n), _f32),
        compiler_params=_params(("parallel", "parallel")),
        name="matmul",
    )(a, w)


def _ctx_attn_kernel(q_ref, k_ref, v_ref, o_ref, *, n_heads, scale):
    dh = q_ref.shape[1] // n_heads
    for h in range(n_heads):
        sl = slice(h * dh, (h + 1) * dh)
        q = q_ref[:, sl].astype(_bf16)
        k = k_ref[:, sl].astype(_bf16)
        v = v_ref[:, sl].astype(_bf16)
        s = _dot_nt(q, k) * scale
        p = jnp.exp(s - s.max(axis=-1, keepdims=True))
        o = _dot(p.astype(_bf16), v) / p.sum(axis=-1, keepdims=True)
        o_ref[:, sl] = o.astype(o_ref.dtype)


def _ctx_attention(qkv, batch, seq, d):
    dh = d // N_HEADS
    return pl.pallas_call(
        functools.partial(_ctx_attn_kernel, n_heads=N_HEADS, scale=dh ** -0.5),
        grid=(batch,),
        in_specs=[
            pl.BlockSpec((seq, d), lambda b: (b, 0)),
            pl.BlockSpec((seq, d), lambda b: (b, 1)),
            pl.BlockSpec((seq, d), lambda b: (b, 2)),
        ],
        out_specs=pl.BlockSpec((seq, d), lambda b: (b, 0)),
        out_shape=jax.ShapeDtypeStruct((batch * seq, d), _bf16),
        compiler_params=_params(("parallel",)),
        name="ctx_attention",
    )(qkv, qkv, qkv)


def _natten_kernel(q_ref, k_ref, v_ref, kc_ref, vc_ref, bias_ref, o_ref, *, rows, width, kh, scale):
    n_loc = kh * width
    kc = kc_ref[...].astype(_bf16)
    vc = vc_ref[...].astype(_bf16)
    q_col = lax.broadcasted_iota(jnp.int32, (width, n_loc), 0)
    k_col = lax.broadcasted_iota(jnp.int32, (width, n_loc), 1) % width
    col_start = jnp.clip(q_col - WIN_COLS // 2, 0, width - WIN_COLS)
    col_mask = (k_col >= col_start) & (k_col < col_start + WIN_COLS)

    def one_row(r, carry):
        r0 = jnp.clip(r - kh // 2, 0, rows - kh)
        q = q_ref[pl.ds(pl.multiple_of(r * width, width), width), :].astype(_bf16)
        win = pl.ds(pl.multiple_of(r0 * width, width), n_loc)
        kw = k_ref[win, :].astype(_bf16)
        vw = v_ref[win, :].astype(_bf16)
        s_loc = _dot_nt(q, kw) * scale + bias_ref[r - r0]
        s_loc = jnp.where(col_mask, s_loc, NEG_INF)
        s_ctx = _dot_nt(q, kc) * scale
        m = jnp.maximum(s_loc.max(axis=-1, keepdims=True), s_ctx.max(axis=-1, keepdims=True))
        p_loc = jnp.exp(s_loc - m)
        p_ctx = jnp.exp(s_ctx - m)
        den = p_loc.sum(axis=-1, keepdims=True) + p_ctx.sum(axis=-1, keepdims=True)
        o = _dot(p_loc.astype(_bf16), vw) + _dot(p_ctx.astype(_bf16), vc)
        o_ref[pl.ds(pl.multiple_of(r * width, width), width), :] = (o / den).astype(o_ref.dtype)
        return carry

    lax.fori_loop(0, rows, one_row, 0)


def _natten_bias(rpb, kh):
    col = jnp.arange(GRID_W)
    dc = jnp.clip(col[None, :] - col[:, None] + WIN_COLS - 1, 0, 2 * WIN_COLS - 2)
    dr = jnp.arange(kh)[None, :] - jnp.arange(kh)[:, None] + (WIN_ROWS - 1)
    t = rpb[:, dr][..., dc]
    return t.transpose(0, 1, 3, 2, 4).reshape(rpb.shape[0], kh, GRID_W, kh * GRID_W).astype(_f32)


def _natten(qkv, n_ctx, k_ctx, v_ctx, rpb, dec_batch, dec_seq, d):
    assert n_ctx % dec_seq == 0
    dh = d // N_HEADS
    rows = dec_seq // GRID_W
    kh = min(WIN_ROWS, rows)
    past = k_ctx.shape[1]
    bias = _natten_bias(rpb, kh)
    b0 = n_ctx // dec_seq
    return pl.pallas_call(
        functools.partial(_natten_kernel, rows=rows, width=GRID_W, kh=kh, scale=dh ** -0.5),
        grid=(dec_batch, N_HEADS),
        in_specs=[
            pl.BlockSpec((dec_seq, dh), lambda b, h: (b0 + b, h)),
            pl.BlockSpec((dec_seq, dh), lambda b, h: (b0 + b, N_HEADS + h)),
            pl.BlockSpec((dec_seq, dh), lambda b, h: (b0 + b, 2 * N_HEADS + h)),
            pl.BlockSpec((None, past, dh), lambda b, h: (b, 0, h)),
            pl.BlockSpec((None, past, dh), lambda b, h: (b, 0, h)),
            pl.BlockSpec((None, kh, GRID_W, kh * GRID_W), lambda b, h: (h, 0, 0, 0)),
        ],
        out_specs=pl.BlockSpec((dec_seq, dh), lambda b, h: (b, h)),
        out_shape=jax.ShapeDtypeStruct((dec_batch * dec_seq, d), _bf16),
        compiler_params=_params(("parallel", "parallel")),
        name="natten",
    )(qkv, qkv, qkv, k_ctx, v_ctx, bias)


def _conv_gate_kernel(b_ref, c_ref, u_ref, cp_ref, up_ref, cn_ref, un_ref, w_ref, o_ref,
                      *, n_ctx, seq, dec_seq):
    tm = o_ref.shape[0]
    row0 = pl.program_id(0) * tm
    in_ctx = row0 < n_ctx
    pos0 = jnp.where(in_ctx, row0 % seq, (row0 - n_ctx) % dec_seq)
    seq_len = jnp.where(in_ctx, seq, dec_seq)
    has_prev = pos0 > 0
    has_next = pos0 + tm < seq_len
    cu = c_ref[...] * u_ref[...]
    prev_row = jnp.where(has_prev, cp_ref[pl.ds(SUBLANES - 1, 1), :] * up_ref[pl.ds(SUBLANES - 1, 1), :], 0.0)
    next_row = jnp.where(has_next, cn_ref[pl.ds(0, 1), :] * un_ref[pl.ds(0, 1), :], 0.0)
    ridx = lax.broadcasted_iota(jnp.int32, cu.shape, 0)
    before = jnp.where(ridx == 0, prev_row, pltpu.roll(cu, 1, axis=0))
    after = jnp.where(ridx == tm - 1, next_row, pltpu.roll(cu, tm - 1, axis=0))
    conv = w_ref[pl.ds(0, 1), :] * before + w_ref[pl.ds(1, 1), :] * cu + w_ref[pl.ds(2, 1), :] * after
    o_ref[...] = (b_ref[...] * conv).astype(o_ref.dtype)


def _conv_gate(bcu, conv_w, n_ctx, seq, dec_seq, d):
    t = bcu.shape[0]
    tm = ROW_TILE
    halo = tm // SUBLANES
    last = t // SUBLANES - 1
    cw = jnp.pad(conv_w, ((0, SUBLANES - CONV_WIDTH), (0, 0)))
    prev_map = lambda col: (lambda i: (jnp.maximum(i * halo - 1, 0), col))
    next_map = lambda col: (lambda i: (jnp.minimum((i + 1) * halo, last), col))
    return pl.pallas_call(
        functools.partial(_conv_gate_kernel, n_ctx=n_ctx, seq=seq, dec_seq=dec_seq),
        grid=(t // tm,),
        in_specs=[
            pl.BlockSpec((tm, d), lambda i: (i, 0)),
            pl.BlockSpec((tm, d), lambda i: (i, 1)),
            pl.BlockSpec((tm, d), lambda i: (i, 2)),
            pl.BlockSpec((SUBLANES, d), prev_map(1)),
            pl.BlockSpec((SUBLANES, d), prev_map(2)),
            pl.BlockSpec((SUBLANES, d), next_map(1)),
            pl.BlockSpec((SUBLANES, d), next_map(2)),
            pl.BlockSpec((SUBLANES, d), lambda i: (0, 0)),
        ],
        out_specs=pl.BlockSpec((tm, d), lambda i: (i, 0)),
        out_shape=jax.ShapeDtypeStruct((t, d), _bf16),
        compiler_params=_params(("parallel",)),
        name="conv_gate",
    )(bcu, bcu, bcu, bcu, bcu, bcu, bcu, cw)


def _route(sel, scores):
    n_grp, eg, tm = sel.shape
    n_exp = n_grp * eg
    j_iota = lax.broadcasted_iota(jnp.int32, sel.shape, 1)
    m1 = sel.max(axis=1, keepdims=True)
    j1 = jnp.min(jnp.where(sel == m1, j_iota, eg), axis=1, keepdims=True)
    m2 = jnp.max(jnp.where(j_iota == j1, -jnp.inf, sel), axis=1, keepdims=True)
    grp = m1 + m2
    g_iota = lax.broadcasted_iota(jnp.int32, grp.shape, 0)
    g_sel = g_iota < 0
    for _ in range(TOPK_GROUPS):
        gm = grp.max(axis=0, keepdims=True)
        gi = jnp.min(jnp.where(grp == gm, g_iota, n_grp), axis=0, keepdims=True)
        hit = g_iota == gi
        g_sel = g_sel | hit
        grp = jnp.where(hit, -jnp.inf, grp)
    cur = jnp.where(jnp.broadcast_to(g_sel, sel.shape), sel, NEG_INF)
    e_iota = lax.broadcasted_iota(jnp.int32, sel.shape, 0) * eg + j_iota
    ids, ws, hits = [], [], []
    for _ in range(TOP_K):
        m = cur.max(axis=1, keepdims=True).max(axis=0, keepdims=True)
        ei = jnp.min(jnp.where(cur == m, e_iota, n_exp), axis=1, keepdims=True).min(axis=0, keepdims=True)
        hit = e_iota == ei
        ids.append(ei)
        ws.append(jnp.sum(jnp.where(hit, scores, 0.0), axis=1, keepdims=True).sum(axis=0, keepdims=True))
        hits.append(hit)
        cur = jnp.where(hit, -jnp.inf, cur)
    total = functools.reduce(lambda a, b: a + b, ws)
    ws = [w / total * ROUTED_SCALE for w in ws]
    return ids, ws, hits


def _post_mixer_kernel(a_ref, wo_ref, x_ref, gpost_ref, mod_ref, gpre_ref, wr_ref, br_ref,
                       x1_ref, hf_ref, idx_ref, wt_ref, rank_ref, cnt_ref, carry_ref, *, n_slabs):
    i = pl.program_id(0)
    tm = x_ref.shape[0]

    @pl.when(i == 0)
    def _():
        carry_ref[...] = jnp.zeros_like(carry_ref)

    o = _dot(a_ref[...], wo_ref[...])
    x1 = x_ref[...] + mod_ref[pl.ds(MOD_GATE_MIX, 1), :] * _rms(o, gpost_ref[...])
    x1_ref[...] = x1
    hf = _rms(x1, gpre_ref[...]) * (1.0 + mod_ref[pl.ds(MOD_SCALE_FFN, 1), :]) + mod_ref[pl.ds(MOD_SHIFT_FFN, 1), :]
    for j in range(n_slabs):
        hf_ref[pl.ds(j, tm, stride=n_slabs), :] = hf[:, j * LANES:(j + 1) * LANES]

    logits = lax.dot_general(wr_ref[...], hf, (((1,), (1,)), ((), ())),
                             precision=lax.Precision.HIGHEST, preferred_element_type=_f32)
    n_exp = logits.shape[0]
    grouped = (N_GROUPS, n_exp // N_GROUPS, tm)
    scores = jax.nn.sigmoid(logits)
    ids, ws, hits = _route((scores + br_ref[...]).reshape(grouped), scores.reshape(grouped))
    for k in range(TOP_K):
        idx_ref[pl.ds(k, 1), :] = ids[k][0]
        wt_ref[pl.ds(k, 1), :] = ws[k][0]

    any_hit = functools.reduce(lambda a, b: a | b, hits)
    mask = jnp.where(any_hit, 1.0, 0.0).reshape(n_exp, tm).astype(_bf16)
    t_src = lax.broadcasted_iota(jnp.int32, (tm, tm), 0)
    t_dst = lax.broadcasted_iota(jnp.int32, (tm, tm), 1)
    before = jnp.where(t_src < t_dst, 1.0, 0.0).astype(_bf16)
    rank = _dot(mask, before) + jnp.concatenate([carry_ref[...]] * (tm // LANES), axis=1)
    rank = rank.reshape(grouped)
    for k in range(TOP_K):
        rk = jnp.sum(jnp.where(hits[k], rank, 0.0), axis=1, keepdims=True).sum(axis=0, keepdims=True)
        rank_ref[pl.ds(k, 1), :] = rk[0].astype(jnp.int32)
    carry_ref[...] += _dot(mask, jnp.ones((tm, LANES), _bf16))
    cnt_ref[...] = carry_ref[...].astype(jnp.int32)


def _post_mixer(a, w_out, x, g_post, mod, g_pre, w_router, b_router, n_ctx, dec_seq):
    t, d = x.shape
    tm = ROW_TILE
    n_slabs = d // LANES
    e = w_router.shape[1]
    row = lambda i: (i, 0)
    fixed = lambda i: (0, 0)
    col = lambda i: (0, i)
    return pl.pallas_call(
        functools.partial(_post_mixer_kernel, n_slabs=n_slabs),
        grid=(t // tm,),
        in_specs=[
            pl.BlockSpec((tm, d), row),
            pl.BlockSpec((d, d), fixed),
            pl.BlockSpec((tm, d), row),
            pl.BlockSpec((1, d), fixed),
            pl.BlockSpec((None, MOD_ROWS, d), lambda i: (_cond_row(i * tm, n_ctx, dec_seq), 0, 0)),
            pl.BlockSpec((1, d), fixed),
            pl.BlockSpec((e, d), fixed),
            pl.BlockSpec((e, 1), fixed),
        ],
        out_specs=[
            pl.BlockSpec((tm, d), row),
            pl.BlockSpec((tm * n_slabs, LANES), row),
            pl.BlockSpec((TOP_K, tm), col),
            pl.BlockSpec((TOP_K, tm), col),
            pl.BlockSpec((TOP_K, tm), col),
            pl.BlockSpec((e, LANES), fixed),
        ],
        out_shape=[
            jax.ShapeDtypeStruct((t, d), _f32),
            jax.ShapeDtypeStruct((t * n_slabs, LANES), _f32),
            jax.ShapeDtypeStruct((TOP_K, t), jnp.int32),
            jax.ShapeDtypeStruct((TOP_K, t), _f32),
            jax.ShapeDtypeStruct((TOP_K, t), jnp.int32),
            jax.ShapeDtypeStruct((e, LANES), jnp.int32),
        ],
        scratch_shapes=[pltpu.VMEM((e, LANES), _f32)],
        compiler_params=_params(("arbitrary",)),
        name="post_mixer",
    )(a, w_out.astype(_bf16), x, g_post.reshape(1, d), mod, g_pre.reshape(1, d),
      w_router.T, b_router.reshape(e, 1))


def _slab(ref, slot, n_slabs):
    return ref.at[pl.ds(pl.multiple_of(slot, n_slabs), n_slabs)]


def _dispatch_kernel(pos_ref, src_ref, dst_ref, sem, *, tile, n_slabs):
    base = pl.program_id(0) * tile

    def issue(t, carry):
        src = _slab(src_ref, (base + t) * n_slabs, n_slabs)
        for k in range(TOP_K):
            pltpu.make_async_copy(src, _slab(dst_ref, pos_ref[t * TOP_K + k], n_slabs), sem).start()
        return carry

    lax.fori_loop(0, tile, issue, 0)
    n = tile * TOP_K * n_slabs
    pltpu.make_async_copy(dst_ref.at[pl.ds(0, n)], dst_ref.at[pl.ds(0, n)], sem).wait()


def _dispatch(hf, pos, n_sorted, n_slabs):
    t = hf.shape[0] // n_slabs
    tile = min(SCATTER_TILE, t)
    return pl.pallas_call(
        functools.partial(_dispatch_kernel, tile=tile, n_slabs=n_slabs),
        grid=(t // tile,),
        in_specs=[
            pl.BlockSpec((tile * TOP_K,), lambda i: (i,), memory_space=pltpu.SMEM),
            pl.BlockSpec(memory_space=pl.ANY),
        ],
        out_specs=pl.BlockSpec(memory_space=pl.ANY),
        out_shape=jax.ShapeDtypeStruct((n_sorted * n_slabs, LANES), _f32),
        scratch_shapes=[pltpu.SemaphoreType.DMA(())],
        compiler_params=_params(("arbitrary",)),
        name="dispatch",
    )(pos, hf)


def _expert_kernel(te_ref, tb_ref, tv_ref, x_ref, wg_ref, wu_ref, wd_ref, o_ref, xb_ref, *, n_slabs):
    i = pl.program_id(0)
    tm = xb_ref.shape[0]

    @pl.when(tv_ref[i] > 0)
    def _():
        valid = lax.broadcasted_iota(jnp.int32, (tm, LANES), 0) < tv_ref[i]
        for j in range(n_slabs):
            piece = x_ref[pl.ds(j, tm, stride=n_slabs), :]
            xb_ref[:, j * LANES:(j + 1) * LANES] = jnp.where(valid, piece, 0.0).astype(_bf16)
        x = xb_ref[...]
        g = _dot(x, wg_ref[...].astype(_bf16))
        u = _dot(x, wu_ref[...].astype(_bf16))
        h = (g * jax.nn.sigmoid(g) * u).astype(_bf16)
        y = _dot(h, wd_ref[...].astype(_bf16))
        for j in range(n_slabs):
            o_ref[pl.ds(j, tm, stride=n_slabs), :] = y[:, j * LANES:(j + 1) * LANES]


def _experts(xs, w_gate, w_up, w_down, tile_expert, tile_block, tile_valid):
    n_exp, d, f = w_gate.shape
    n_slabs = d // LANES
    tm = EXPERT_TILE
    n_tiles = tile_expert.shape[0]
    rows = lambda i, te, tb, tv: (tb[i], 0)
    wsel = lambda i, te, tb, tv: (te[i], 0, 0)
    return pl.pallas_call(
        functools.partial(_expert_kernel, n_slabs=n_slabs),
        grid_spec=pltpu.PrefetchScalarGridSpec(
            num_scalar_prefetch=3,
            grid=(n_tiles,),
            in_specs=[
                pl.BlockSpec((tm * n_slabs, LANES), rows),
                pl.BlockSpec((None, d, f), wsel),
                pl.BlockSpec((None, d, f), wsel),
                pl.BlockSpec((None, f, d), wsel),
            ],
            out_specs=pl.BlockSpec((tm * n_slabs, LANES), rows),
            scratch_shapes=[pltpu.VMEM((tm, d), _bf16)],
        ),
        out_shape=jax.ShapeDtypeStruct(xs.shape, _f32),
        compiler_params=_params(("arbitrary",)),
        name="experts",
    )(tile_expert, tile_block, tile_valid, xs, w_gate, w_up, w_down)


def _combine_kernel(pos_ref, ys_ref, wt_ref, sh_ref, x1_ref, gpost_ref, mod_ref, gnext_ref, modn_ref,
                    x2_ref, hn_ref, buf_ref, sem, *, n_slabs):
    tm = x1_ref.shape[0]

    def issue(t, carry):
        for k in range(TOP_K):
            pltpu.make_async_copy(_slab(ys_ref, pos_ref[t * TOP_K + k], n_slabs),
                                  _slab(buf_ref, (k * tm + t) * n_slabs, n_slabs), sem).start()
        return carry

    lax.fori_loop(0, tm, issue, 0)
    pltpu.make_async_copy(ys_ref.at[pl.ds(0, TOP_K * tm * n_slabs)], buf_ref, sem).wait()

    w = wt_ref[...]
    cols = []
    for j in range(n_slabs):
        acc = sh_ref[pl.ds(j, tm, stride=n_slabs), :]
        for k in range(TOP_K):
            piece = buf_ref[pl.ds(k * tm * n_slabs + j, tm, stride=n_slabs), :]
            acc = acc + w[:, k:k + 1] * piece
        cols.append(acc)
    f = jnp.concatenate(cols, axis=1)
    x2 = x1_ref[...] + mod_ref[pl.ds(MOD_GATE_FFN, 1), :] * _rms(f, gpost_ref[...])
    x2_ref[...] = x2
    hn = _rms(x2, gnext_ref[...]) * (1.0 + modn_ref[pl.ds(MOD_SCALE_MIX, 1), :]) + modn_ref[pl.ds(MOD_SHIFT_MIX, 1), :]
    hn_ref[...] = hn.astype(hn_ref.dtype)


def _combine(ys, pos, wt, shared, x1, g_post, mod, g_next, mod_next, n_ctx, dec_seq):
    t, d = x1.shape
    n_slabs = d // LANES
    tm = min(GATHER_TILE, t)
    row = lambda i: (i, 0)
    fixed = lambda i: (0, 0)
    cond = lambda i: (_cond_row(i * tm, n_ctx, dec_seq), 0, 0)
    return pl.pallas_call(
        functools.partial(_combine_kernel, n_slabs=n_slabs),
        grid=(t // tm,),
        in_specs=[
            pl.BlockSpec((tm * TOP_K,), lambda i: (i,), memory_space=pltpu.SMEM),
            pl.BlockSpec(memory_space=pl.ANY),
            pl.BlockSpec((tm, TOP_K), row),
            pl.BlockSpec((tm * n_slabs, LANES), row),
            pl.BlockSpec((tm, d), row),
            pl.BlockSpec((1, d), fixed),
            pl.BlockSpec((None, MOD_ROWS, d), cond),
            pl.BlockSpec((1, d), fixed),
            pl.BlockSpec((None, MOD_ROWS, d), cond),
        ],
        out_specs=[pl.BlockSpec((tm, d), row), pl.BlockSpec((tm, d), row)],
        out_shape=[jax.ShapeDtypeStruct((t, d), _f32), jax.ShapeDtypeStruct((t, d), _bf16)],
        scratch_shapes=[pltpu.VMEM((TOP_K * tm * n_slabs, LANES), _f32), pltpu.SemaphoreType.DMA(())],
        compiler_params=_params(("arbitrary",)),
        name="combine",
    )(pos, ys, wt, shared, x1, g_post.reshape(1, d), mod, g_next.reshape(1, d), mod_next)


def _moe_plan(idx_t, rank_t, counts, n_tiles, n_slabs):
    tm = EXPERT_TILE
    padded = (counts + tm - 1) // tm * tm
    ends = jnp.cumsum(padded)
    offs = ends - padded
    pos = ((offs[idx_t] + rank_t) * n_slabs).T.reshape(-1)
    start = jnp.arange(n_tiles, dtype=jnp.int32) * tm
    live = start < ends[-1]
    last_tile = jnp.maximum(ends[-1] // tm - 1, 0)
    tile = jnp.where(live, jnp.arange(n_tiles, dtype=jnp.int32), last_tile)
    expert = jnp.minimum(jnp.searchsorted(ends, tile * tm, side="right"), counts.shape[0] - 1).astype(jnp.int32)
    valid = jnp.where(live, jnp.clip(offs[expert] + counts[expert] - start, 0, tm), 0).astype(jnp.int32)
    return pos.astype(jnp.int32), expert, tile.astype(jnp.int32), valid


def _moe(hf, idx_t, wt_t, rank_t, cnt, w_gate, w_up, w_down, w_sh_gate, w_sh_up, w_sh_down):
    n_exp, d, _ = w_gate.shape
    n_slabs = d // LANES
    t = hf.shape[0] // n_slabs
    tm = EXPERT_TILE
    n_tiles = t * TOP_K // tm + n_exp
    pos, tile_expert, tile_block, tile_valid = _moe_plan(idx_t, rank_t, cnt[:, 0], n_tiles, n_slabs)
    xs = _dispatch(hf, pos, n_tiles * tm, n_slabs)
    ys = _experts(xs, w_gate, w_up, w_down, tile_expert, tile_block, tile_valid)
    n_sh = t // tm
    shared = _experts(hf, w_sh_gate[None], w_sh_up[None], w_sh_down[None],
                      jnp.zeros((n_sh,), jnp.int32), jnp.arange(n_sh, dtype=jnp.int32),
                      jnp.full((n_sh,), tm, jnp.int32))
    return ys, pos, wt_t.T, shared


def kernel(x_prompt, x_sample, cache_k, cache_v, c, c_ctx, w_ada, b_ada, norm_mix_pre, norm_mix_post, norm_ffn_pre, norm_ffn_post, w_qkv, w_o_attn, rpb, w_conv_in, conv_w, w_conv_out, w_router, b_router, w_exp_gate, w_exp_up, w_exp_down, w_sh_gate, w_sh_up, w_sh_down):
    batch, seq, d = x_prompt.shape
    dec_batch, dec_seq, _ = x_sample.shape
    depth = w_ada.shape[0]
    n_ctx, n_lat = batch * seq, dec_batch * dec_seq
    dh = d // N_HEADS
    past = cache_k.shape[2]

    x = jnp.concatenate([x_prompt.reshape(n_ctx, d), x_sample.reshape(n_lat, d)], axis=0)
    cond = jnp.concatenate([c_ctx[None, :], c], axis=0)
    mod = _adaln(cond, w_ada, b_ada)

    new_k, new_v = [], []
    h = _modulate(x, norm_mix_pre[0], mod[0], n_ctx, dec_seq)
    for l in range(depth):
        if l % 2 == 0:
            a = l // 2
            qkv = _matmul(h, w_qkv[a])
            new_k.append(qkv[:n_ctx, d:2 * d].reshape(batch, seq, N_HEADS, dh))
            new_v.append(qkv[:n_ctx, 2 * d:].reshape(batch, seq, N_HEADS, dh))
            o_p = _ctx_attention(qkv, batch, seq, d)
            o_s = _natten(qkv, n_ctx, cache_k[:, a].reshape(dec_batch, past, d),
                          cache_v[:, a].reshape(dec_batch, past, d), rpb[a], dec_batch, dec_seq, d)
            mixed = jnp.concatenate([o_p, o_s], axis=0)
            w_out = w_o_attn[a]
        else:
            m = l // 2
            mixed = _conv_gate(_matmul(h, w_conv_in[m]), conv_w[m], n_ctx, seq, dec_seq, d)
            w_out = w_conv_out[m]
        x1, hf, idx_t, wt_t, rank_t, cnt = _post_mixer(
            mixed, w_out, x, norm_mix_post[l], mod[l], norm_ffn_pre[l], w_router[l], b_router[l], n_ctx, dec_seq)
        ys, pos, wt, shared = _moe(hf, idx_t, wt_t, rank_t, cnt, w_exp_gate[l], w_exp_up[l], w_exp_down[l],
                                   w_sh_gate[l], w_sh_up[l], w_sh_down[l])
        nxt = min(l + 1, depth - 1)
        x, h = _combine(ys, pos, wt, shared, x1, norm_ffn_post[l], mod[l], norm_mix_pre[nxt], mod[nxt],
                        n_ctx, dec_seq)

    y_prompt = x[:n_ctx].reshape(batch, seq, d)
    y_sample = x[n_ctx:].reshape(dec_batch, dec_seq, d)
    return (y_prompt, y_sample, jnp.stack(new_k, axis=1), jnp.stack(new_v, axis=1))
```

```python
import functools

import jax
import jax.numpy as jnp
from jax import lax
from jax.experimental import pallas as pl
from jax.experimental.pallas import tpu as pltpu

N_HEADS = 16
GRID_W = 64
WIN_ROWS = 8
WIN_COLS = 16
CONV_WIDTH = 3
N_EXPERTS = 64
TOP_K = 8
N_GROUPS = 8
TOPK_GROUPS = 4
ROUTED_SCALE = 2.5
N_MOD = 6
RMS_EPS = 1e-6
NEG_INF = -1e30

LANES = 128
SUBLANES = 8
VMEM_LIMIT = 56 * 1024 * 1024

MOD_SHIFT_MIX, MOD_SCALE_MIX, MOD_GATE_MIX, MOD_SHIFT_FFN, MOD_SCALE_FFN, MOD_GATE_FFN = range(6)
MOD_ROWS = 8

ROW_TILE = 256
EXPERT_TILE = 256
MM_TILE_M = 2048
MM_TILE_N = 512
GATHER_TILE = 128
SCATTER_TILE = 512

_f32 = jnp.float32
_bf16 = jnp.bfloat16


def _params(sem, vmem=VMEM_LIMIT):
    return pltpu.CompilerParams(dimension_semantics=sem, vmem_limit_bytes=vmem)


def _rms(x, g):
    return x * lax.rsqrt(jnp.mean(x * x, axis=-1, keepdims=True) + RMS_EPS) * g


def _dot(a, b):
    return jnp.dot(a, b, preferred_element_type=_f32)


def _dot_nt(a, b):
    return lax.dot_general(a, b, (((1,), (1,)), ((), ())), preferred_element_type=_f32)


def _cond_row(row0, n_ctx, dec_seq):
    return jnp.where(row0 < n_ctx, 0, 1 + (row0 - n_ctx) // dec_seq)


def _adaln_kernel(cb_ref, w_ref, b_ref, o_ref, acc_ref, *, n_cond):
    k = pl.program_id(2)

    @pl.when(k == 0)
    def _():
        acc_ref[...] = jnp.zeros_like(acc_ref)

    tk, tn = w_ref.shape
    for r in range(n_cond):
        s = cb_ref[r]
        s = s * jax.nn.sigmoid(s)
        for c in range(tn // LANES):
            p = w_ref[:, c * LANES:(c + 1) * LANES] * s
            acc_ref[r, :, c * LANES:(c + 1) * LANES] += p.reshape(tk // SUBLANES, SUBLANES, LANES).sum(axis=0)

    @pl.when(k == pl.num_programs(2) - 1)
    def _():
        o_ref[...] = jnp.zeros_like(o_ref)
        for r in range(n_cond):
            o_ref[pl.ds(r, 1), :] = acc_ref[r].sum(axis=0, keepdims=True) + b_ref[...]


def _adaln(cond, w_ada, b_ada):
    n_cond, d = cond.shape
    n_layers, _, n6 = w_ada.shape
    tk, tn = min(512, d), min(2048, n6)
    cb = jnp.broadcast_to(cond[:, :, None], (n_cond, d, LANES))
    mod = pl.pallas_call(
        functools.partial(_adaln_kernel, n_cond=n_cond),
        grid=(n_layers, n6 // tn, d // tk),
        in_specs=[
            pl.BlockSpec((n_cond, tk, LANES), lambda l, n, k: (0, k, 0)),
            pl.BlockSpec((None, tk, tn), lambda l, n, k: (l, k, n)),
            pl.BlockSpec((None, 1, tn), lambda l, n, k: (l, 0, n)),
        ],
        out_specs=pl.BlockSpec((None, MOD_ROWS, tn), lambda l, n, k: (l, 0, n)),
        out_shape=jax.ShapeDtypeStruct((n_layers, MOD_ROWS, n6), _f32),
        scratch_shapes=[pltpu.VMEM((n_cond, SUBLANES, tn), _f32)],
        compiler_params=_params(("arbitrary", "arbitrary", "arbitrary")),
        name="adaln",
    )(cb, w_ada, b_ada.reshape(n_layers, 1, n6))
    mod = mod[:, :n_cond].reshape(n_layers, n_cond, N_MOD, d)
    return jnp.pad(mod, ((0, 0), (0, 0), (0, MOD_ROWS - N_MOD), (0, 0)))


def _split_specs(parts, tm, width):
    if len(parts) == 1:
        return [pl.BlockSpec((tm, width), lambda i: (i, 0))]
    n0 = parts[0].shape[0] // tm
    n1 = parts[1].shape[0] // tm
    return [pl.BlockSpec((tm, width), lambda i: (jnp.minimum(i, n0 - 1), 0)),
            pl.BlockSpec((tm, width), lambda i: (jnp.clip(i - n0, 0, n1 - 1), 0))]


def _split_read(refs, n_first):
    if len(refs) == 1:
        return refs[0][...]
    return jnp.where(pl.program_id(0) < n_first, refs[0][...], refs[1][...])


def _modulate_kernel(*refs, n_x, n_first):
    x_refs, (g_ref, mod_ref, o_ref) = refs[:n_x], refs[n_x:]
    y = _rms(_split_read(x_refs, n_first), g_ref[...])
    h = y * (1.0 + mod_ref[pl.ds(MOD_SCALE_MIX, 1), :]) + mod_ref[pl.ds(MOD_SHIFT_MIX, 1), :]
    o_ref[...] = h.astype(o_ref.dtype)


def _modulate(xs, g, mod, n_ctx, dec_seq):
    t = sum(x.shape[0] for x in xs)
    d = xs[0].shape[1]
    tm = ROW_TILE
    return pl.pallas_call(
        functools.partial(_modulate_kernel, n_x=len(xs), n_first=xs[0].shape[0] // tm),
        grid=(t // tm,),
        in_specs=_split_specs(xs, tm, d) + [
            pl.BlockSpec((1, d), lambda i: (0, 0)),
            pl.BlockSpec((None, MOD_ROWS, d), lambda i: (_cond_row(i * tm, n_ctx, dec_seq), 0, 0)),
        ],
        out_specs=pl.BlockSpec((tm, d), lambda i: (i, 0)),
        out_shape=jax.ShapeDtypeStruct((t, d), _bf16),
        compiler_params=_params(("parallel",)),
        name="modulate",
    )(*xs, g.reshape(1, d), mod)


def _matmul_kernel(a_ref, w_ref, o_ref):
    o_ref[...] = _dot(a_ref[...], w_ref[...].astype(_bf16)).astype(o_ref.dtype)


def _matmul(a, w):
    n_rows = a.shape[0]
    k, n = w.shape
    tm, tn = min(MM_TILE_M, n_rows), min(MM_TILE_N, n)
    return pl.pallas_call(
        _matmul_kernel,
        grid=(n_rows // tm, n // tn),
        in_specs=[
            pl.BlockSpec((tm, k), lambda i, j: (i, 0)),
            pl.BlockSpec((k, tn), lambda i, j: (0, j)),
        ],
        out_specs=pl.BlockSpec((tm, tn), lambda i, j: (i, j)),
        out_shape=jax.ShapeDtypeStruct((n_rows, n), _f32),
        compiler_params=_params(("parallel", "parallel")),
        name="matmul",
    )(a, w)


def _ctx_attn_kernel(q_ref, k_ref, v_ref, o_ref, kout_ref, vout_ref, *, n_heads, scale):
    seq = q_ref.shape[0]
    dh = q_ref.shape[1] // n_heads
    for h in range(n_heads):
        sl = slice(h * dh, (h + 1) * dh)
        k32 = k_ref[:, sl]
        v32 = v_ref[:, sl]
        kout_ref[pl.ds(h, seq, stride=n_heads), :] = k32
        vout_ref[pl.ds(h, seq, stride=n_heads), :] = v32
        q = q_ref[:, sl].astype(_bf16)
        s = _dot_nt(q, k32.astype(_bf16)) * scale
        p = jnp.exp(s - s.max(axis=-1, keepdims=True))
        o = _dot(p.astype(_bf16), v32.astype(_bf16)) / p.sum(axis=-1, keepdims=True)
        o_ref[:, sl] = o.astype(o_ref.dtype)


def _ctx_attention(qkv, batch, seq, d):
    dh = d // N_HEADS
    assert dh == LANES
    cache = jax.ShapeDtypeStruct((batch * seq * N_HEADS, dh), _f32)
    return pl.pallas_call(
        functools.partial(_ctx_attn_kernel, n_heads=N_HEADS, scale=dh ** -0.5),
        grid=(batch,),
        in_specs=[
            pl.BlockSpec((seq, d), lambda b: (b, 0)),
            pl.BlockSpec((seq, d), lambda b: (b, 1)),
            pl.BlockSpec((seq, d), lambda b: (b, 2)),
        ],
        out_specs=[
            pl.BlockSpec((seq, d), lambda b: (b, 0)),
            pl.BlockSpec((seq * N_HEADS, dh), lambda b: (b, 0)),
            pl.BlockSpec((seq * N_HEADS, dh), lambda b: (b, 0)),
        ],
        out_shape=[jax.ShapeDtypeStruct((batch * seq, d), _bf16), cache, cache],
        compiler_params=_params(("parallel",)),
        name="ctx_attention",
    )(qkv, qkv, qkv)


def _natten_kernel(q_ref, k_ref, v_ref, kc_ref, vc_ref, bias_ref, o_ref, *, rows, width, kh, scale):
    n_loc = kh * width
    kc = kc_ref[...].astype(_bf16)
    vc = vc_ref[...].astype(_bf16)
    q_col = lax.broadcasted_iota(jnp.int32, (width, n_loc), 0)
    k_col = lax.broadcasted_iota(jnp.int32, (width, n_loc), 1) % width
    col_start = jnp.clip(q_col - WIN_COLS // 2, 0, width - WIN_COLS)
    col_mask = (k_col >= col_start) & (k_col < col_start + WIN_COLS)

    def one_row(r, carry):
        r0 = jnp.clip(r - kh // 2, 0, rows - kh)
        q = q_ref[pl.ds(pl.multiple_of(r * width, width), width), :].astype(_bf16)
        win = pl.ds(pl.multiple_of(r0 * width, width), n_loc)
        kw = k_ref[win, :].astype(_bf16)
        vw = v_ref[win, :].astype(_bf16)
        s_loc = _dot_nt(q, kw) * scale + bias_ref[r - r0]
        s_loc = jnp.where(col_mask, s_loc, NEG_INF)
        s_ctx = _dot_nt(q, kc) * scale
        m = jnp.maximum(s_loc.max(axis=-1, keepdims=True), s_ctx.max(axis=-1, keepdims=True))
        p_loc = jnp.exp(s_loc - m)
        p_ctx = jnp.exp(s_ctx - m)
        den = p_loc.sum(axis=-1, keepdims=True) + p_ctx.sum(axis=-1, keepdims=True)
        o = _dot(p_loc.astype(_bf16), vw) + _dot(p_ctx.astype(_bf16), vc)
        o_ref[pl.ds(pl.multiple_of(r * width, width), width), :] = (o / den).astype(o_ref.dtype)
        return carry

    lax.fori_loop(0, rows, one_row, 0)


def _natten_bias(rpb, kh):
    col = jnp.arange(GRID_W)
    dc = jnp.clip(col[None, :] - col[:, None] + WIN_COLS - 1, 0, 2 * WIN_COLS - 2)
    dr = jnp.arange(kh)[None, :] - jnp.arange(kh)[:, None] + (WIN_ROWS - 1)
    t = rpb[:, dr][..., dc]
    return t.transpose(0, 1, 3, 2, 4).reshape(rpb.shape[0], kh, GRID_W, kh * GRID_W).astype(_f32)


def _natten(qkv, n_ctx, k_ctx, v_ctx, rpb, dec_batch, dec_seq, d):
    assert n_ctx % dec_seq == 0
    dh = d // N_HEADS
    rows = dec_seq // GRID_W
    kh = min(WIN_ROWS, rows)
    past = k_ctx.shape[1]
    bias = _natten_bias(rpb, kh)
    b0 = n_ctx // dec_seq
    return pl.pallas_call(
        functools.partial(_natten_kernel, rows=rows, width=GRID_W, kh=kh, scale=dh ** -0.5),
        grid=(dec_batch, N_HEADS),
        in_specs=[
            pl.BlockSpec((dec_seq, dh), lambda b, h: (b0 + b, h)),
            pl.BlockSpec((dec_seq, dh), lambda b, h: (b0 + b, N_HEADS + h)),
            pl.BlockSpec((dec_seq, dh), lambda b, h: (b0 + b, 2 * N_HEADS + h)),
            pl.BlockSpec((None, past, dh), lambda b, h: (b, 0, h)),
            pl.BlockSpec((None, past, dh), lambda b, h: (b, 0, h)),
            pl.BlockSpec((None, kh, GRID_W, kh * GRID_W), lambda b, h: (h, 0, 0, 0)),
        ],
        out_specs=pl.BlockSpec((dec_seq, dh), lambda b, h: (b, h)),
        out_shape=jax.ShapeDtypeStruct((dec_batch * dec_seq, d), _bf16),
        compiler_params=_params(("parallel", "parallel")),
        name="natten",
    )(qkv, qkv, qkv, k_ctx, v_ctx, bias)


def _conv_gate_kernel(b_ref, c_ref, u_ref, cp_ref, up_ref, cn_ref, un_ref, w_ref, o_ref,
                      *, n_ctx, seq, dec_seq):
    tm = o_ref.shape[0]
    row0 = pl.program_id(0) * tm
    in_ctx = row0 < n_ctx
    pos0 = jnp.where(in_ctx, row0 % seq, (row0 - n_ctx) % dec_seq)
    seq_len = jnp.where(in_ctx, seq, dec_seq)
    has_prev = pos0 > 0
    has_next = pos0 + tm < seq_len
    cu = c_ref[...] * u_ref[...]
    prev_row = jnp.where(has_prev, cp_ref[pl.ds(SUBLANES - 1, 1), :] * up_ref[pl.ds(SUBLANES - 1, 1), :], 0.0)
    next_row = jnp.where(has_next, cn_ref[pl.ds(0, 1), :] * un_ref[pl.ds(0, 1), :], 0.0)
    ridx = lax.broadcasted_iota(jnp.int32, cu.shape, 0)
    before = jnp.where(ridx == 0, prev_row, pltpu.roll(cu, 1, axis=0))
    after = jnp.where(ridx == tm - 1, next_row, pltpu.roll(cu, tm - 1, axis=0))
    conv = w_ref[pl.ds(0, 1), :] * before + w_ref[pl.ds(1, 1), :] * cu + w_ref[pl.ds(2, 1), :] * after
    o_ref[...] = (b_ref[...] * conv).astype(o_ref.dtype)


def _conv_gate(bcu, conv_w, n_ctx, seq, dec_seq, d):
    t = bcu.shape[0]
    tm = ROW_TILE
    halo = tm // SUBLANES
    last = t // SUBLANES - 1
    cw = jnp.pad(conv_w, ((0, SUBLANES - CONV_WIDTH), (0, 0)))
    prev_map = lambda col: (lambda i: (jnp.maximum(i * halo - 1, 0), col))
    next_map = lambda col: (lambda i: (jnp.minimum((i + 1) * halo, last), col))
    return pl.pallas_call(
        functools.partial(_conv_gate_kernel, n_ctx=n_ctx, seq=seq, dec_seq=dec_seq),
        grid=(t // tm,),
        in_specs=[
            pl.BlockSpec((tm, d), lambda i: (i, 0)),
            pl.BlockSpec((tm, d), lambda i: (i, 1)),
            pl.BlockSpec((tm, d), lambda i: (i, 2)),
            pl.BlockSpec((SUBLANES, d), prev_map(1)),
            pl.BlockSpec((SUBLANES, d), prev_map(2)),
            pl.BlockSpec((SUBLANES, d), next_map(1)),
            pl.BlockSpec((SUBLANES, d), next_map(2)),
            pl.BlockSpec((SUBLANES, d), lambda i: (0, 0)),
        ],
        out_specs=pl.BlockSpec((tm, d), lambda i: (i, 0)),
        out_shape=jax.ShapeDtypeStruct((t, d), _bf16),
        compiler_params=_params(("parallel",)),
        name="conv_gate",
    )(bcu, bcu, bcu, bcu, bcu, bcu, bcu, cw)


def _route(sel, scores):
    n_grp, eg, tm = sel.shape
    n_exp = n_grp * eg
    j_iota = lax.broadcasted_iota(jnp.int32, sel.shape, 1)
    m1 = sel.max(axis=1, keepdims=True)
    j1 = jnp.min(jnp.where(sel == m1, j_iota, eg), axis=1, keepdims=True)
    m2 = jnp.max(jnp.where(j_iota == j1, -jnp.inf, sel), axis=1, keepdims=True)
    grp = m1 + m2
    g_iota = lax.broadcasted_iota(jnp.int32, grp.shape, 0)
    g_sel = g_iota < 0
    for _ in range(TOPK_GROUPS):
        gm = grp.max(axis=0, keepdims=True)
        gi = jnp.min(jnp.where(grp == gm, g_iota, n_grp), axis=0, keepdims=True)
        hit = g_iota == gi
        g_sel = g_sel | hit
        grp = jnp.where(hit, -jnp.inf, grp)
    cur = jnp.where(jnp.broadcast_to(g_sel, sel.shape), sel, NEG_INF)
    e_iota = lax.broadcasted_iota(jnp.int32, sel.shape, 0) * eg + j_iota
    ids, ws, hits = [], [], []
    for _ in range(TOP_K):
        m = cur.max(axis=1, keepdims=True).max(axis=0, keepdims=True)
        ei = jnp.min(jnp.where(cur == m, e_iota, n_exp), axis=1, keepdims=True).min(axis=0, keepdims=True)
        hit = e_iota == ei
        ids.append(ei)
        ws.append(jnp.sum(jnp.where(hit, scores, 0.0), axis=1, keepdims=True).sum(axis=0, keepdims=True))
        hits.append(hit)
        cur = jnp.where(hit, -jnp.inf, cur)
    total = functools.reduce(lambda a, b: a + b, ws)
    ws = [w / total * ROUTED_SCALE for w in ws]
    return ids, ws, hits


def _post_mixer_kernel(*refs, n_slabs, n_a, n_x, n_first):
    a_refs, x_refs = refs[:n_a], refs[n_a:n_a + n_x]
    (wo_ref, gpost_ref, mod_ref, gpre_ref, wr_ref, br_ref,
     x1_ref, hf_ref, idx_ref, wt_ref, rank_ref, cnt_ref, carry_ref) = refs[n_a + n_x:]
    i = pl.program_id(0)
    tm = x1_ref.shape[0]

    @pl.when(i == 0)
    def _():
        carry_ref[...] = jnp.zeros_like(carry_ref)

    o = _dot(_split_read(a_refs, n_first), wo_ref[...])
    x1 = _split_read(x_refs, n_first) + mod_ref[pl.ds(MOD_GATE_MIX, 1), :] * _rms(o, gpost_ref[...])
    x1_ref[...] = x1
    hf = _rms(x1, gpre_ref[...]) * (1.0 + mod_ref[pl.ds(MOD_SCALE_FFN, 1), :]) + mod_ref[pl.ds(MOD_SHIFT_FFN, 1), :]
    for j in range(n_slabs):
        hf_ref[pl.ds(j, tm, stride=n_slabs), :] = hf[:, j * LANES:(j + 1) * LANES]

    logits = lax.dot_general(wr_ref[...], hf, (((1,), (1,)), ((), ())),
                             precision=lax.Precision.HIGHEST, preferred_element_type=_f32)
    n_exp = logits.shape[0]
    grouped = (N_GROUPS, n_exp // N_GROUPS, tm)
    scores = jax.nn.sigmoid(logits)
    ids, ws, hits = _route((scores + br_ref[...]).reshape(grouped), scores.reshape(grouped))
    for k in range(TOP_K):
        idx_ref[pl.ds(k, 1), :] = ids[k][0]
        wt_ref[pl.ds(k, 1), :] = ws[k][0]

    any_hit = functools.reduce(lambda a, b: a | b, hits)
    mask = jnp.where(any_hit, 1.0, 0.0).reshape(n_exp, tm).astype(_bf16)
    t_src = lax.broadcasted_iota(jnp.int32, (tm, tm), 0)
    t_dst = lax.broadcasted_iota(jnp.int32, (tm, tm), 1)
    before = jnp.where(t_src < t_dst, 1.0, 0.0).astype(_bf16)
    rank = _dot(mask, before) + jnp.concatenate([carry_ref[...]] * (tm // LANES), axis=1)
    rank = rank.reshape(grouped)
    for k in range(TOP_K):
        rk = jnp.sum(jnp.where(hits[k], rank, 0.0), axis=1, keepdims=True).sum(axis=0, keepdims=True)
        rank_ref[pl.ds(k, 1), :] = rk[0].astype(jnp.int32)
    carry_ref[...] += _dot(mask, jnp.ones((tm, LANES), _bf16))
    cnt_ref[...] = carry_ref[...].astype(jnp.int32)


def _post_mixer(a_parts, w_out, x_parts, g_post, mod, g_pre, w_router, b_router, n_ctx, dec_seq):
    t = sum(x.shape[0] for x in x_parts)
    d = x_parts[0].shape[1]
    tm = ROW_TILE
    n_slabs = d // LANES
    e = w_router.shape[1]
    row = lambda i: (i, 0)
    fixed = lambda i: (0, 0)
    col = lambda i: (0, i)
    return pl.pallas_call(
        functools.partial(_post_mixer_kernel, n_slabs=n_slabs, n_a=len(a_parts), n_x=len(x_parts),
                          n_first=n_ctx // tm),
        grid=(t // tm,),
        in_specs=_split_specs(a_parts, tm, d) + _split_specs(x_parts, tm, d) + [
            pl.BlockSpec((d, d), fixed),
            pl.BlockSpec((1, d), fixed),
            pl.BlockSpec((None, MOD_ROWS, d), lambda i: (_cond_row(i * tm, n_ctx, dec_seq), 0, 0)),
            pl.BlockSpec((1, d), fixed),
            pl.BlockSpec((e, d), fixed),
            pl.BlockSpec((e, 1), fixed),
        ],
        out_specs=[
            pl.BlockSpec((tm, d), row),
            pl.BlockSpec((tm * n_slabs, LANES), row),
            pl.BlockSpec((TOP_K, tm), col),
            pl.BlockSpec((TOP_K, tm), col),
            pl.BlockSpec((TOP_K, tm), col),
            pl.BlockSpec((e, LANES), fixed),
        ],
        out_shape=[
            jax.ShapeDtypeStruct((t, d), _f32),
            jax.ShapeDtypeStruct((t * n_slabs, LANES), _f32),
            jax.ShapeDtypeStruct((TOP_K, t), jnp.int32),
            jax.ShapeDtypeStruct((TOP_K, t), _f32),
            jax.ShapeDtypeStruct((TOP_K, t), jnp.int32),
            jax.ShapeDtypeStruct((e, LANES), jnp.int32),
        ],
        scratch_shapes=[pltpu.VMEM((e, LANES), _f32)],
        compiler_params=_params(("arbitrary",)),
        name="post_mixer",
    )(*a_parts, *x_parts, w_out.astype(_bf16), g_post.reshape(1, d), mod, g_pre.reshape(1, d),
      w_router.T, b_router.reshape(e, 1))


def _slab(ref, slot, n_slabs):
    return ref.at[pl.ds(pl.multiple_of(slot, n_slabs), n_slabs)]


def _plan_kernel(cnt_ref, idx_ref, rank_ref, pos_ref, te_ref, tbi_ref, tbo_ref, tv_ref, padrow_ref, padn_ref,
                 offs_ref, *, tm, n_slabs):
    n_exp = cnt_ref.shape[0]
    n_tiles = te_ref.shape[0]

    def per_expert(e, carry):
        off, tile = carry
        cnt = cnt_ref[e]
        n_t = (cnt + tm - 1) // tm
        offs_ref[e] = off
        padrow_ref[e] = (off + cnt) * n_slabs
        padn_ref[e] = n_t * tm - cnt

        def per_tile(j, c):
            te_ref[tile + j] = e
            tbi_ref[tile + j] = tile + j
            tbo_ref[tile + j] = tile + j
            tv_ref[tile + j] = jnp.minimum(cnt - j * tm, tm)
            return c

        lax.fori_loop(0, n_t, per_tile, 0)
        return off + n_t * tm, tile + n_t

    _, live = lax.fori_loop(0, n_exp, per_expert, (jnp.int32(0), jnp.int32(0)))
    last = jnp.maximum(live - 1, 0)
    last_expert = te_ref[last]

    def dead_tile(i, c):
        te_ref[i] = last_expert
        tbi_ref[i] = last
        tbo_ref[i] = i
        tv_ref[i] = 0
        return c

    lax.fori_loop(live, n_tiles, dead_tile, 0)

    idx = idx_ref[...]
    pos = rank_ref[...]
    for e in range(n_exp):
        pos = pos + jnp.where(idx == e, offs_ref[e], 0)
    pos_ref[...] = pos * n_slabs


def _plan(counts, idx_t, rank_t, n_tiles, n_slabs):
    n_exp = counts.shape[0]
    smem = pl.BlockSpec(memory_space=pltpu.SMEM)
    vmem = pl.BlockSpec(memory_space=pltpu.VMEM)
    tiles = jax.ShapeDtypeStruct((n_tiles,), jnp.int32)
    experts = jax.ShapeDtypeStruct((n_exp,), jnp.int32)
    return pl.pallas_call(
        functools.partial(_plan_kernel, tm=EXPERT_TILE, n_slabs=n_slabs),
        in_specs=[smem, vmem, vmem],
        out_specs=[vmem, smem, smem, smem, smem, smem, smem],
        out_shape=[jax.ShapeDtypeStruct(idx_t.shape, jnp.int32), tiles, tiles, tiles, tiles, experts, experts],
        scratch_shapes=[pltpu.SMEM((n_exp,), jnp.int32)],
        name="plan",
    )(counts, idx_t, rank_t)


def _pad_copy(e, b, padrow_ref, padn_ref, zero_ref, dst_ref, sem, n_slabs):
    n = padn_ref[e]
    rows = (1 << b) * n_slabs
    start = padrow_ref[e] + ((n >> (b + 1)) << (b + 1)) * n_slabs
    copy = pltpu.make_async_copy(zero_ref.at[pl.ds(0, rows)],
                                 dst_ref.at[pl.ds(pl.multiple_of(start, n_slabs), rows)], sem)
    return ((n >> b) & 1) == 1, copy


def _dispatch_kernel(pos_ref, padrow_ref, padn_ref, src_ref, dst_ref, zero_ref, sem, zero_sem,
                     *, n_slabs, pad_bits):
    i = pl.program_id(0)
    tile = pos_ref.shape[1]
    n_exp = padn_ref.shape[0]

    def for_each_pad_copy(fn):
        def body(e, c):
            for b in range(pad_bits):
                needed, copy = _pad_copy(e, b, padrow_ref, padn_ref, zero_ref, dst_ref, zero_sem, n_slabs)
                pl.when(needed)(functools.partial(fn, copy))
            return c
        lax.fori_loop(0, n_exp, body, 0)

    @pl.when(i == 0)
    def _():
        zero_ref[...] = jnp.zeros_like(zero_ref)
        for_each_pad_copy(lambda copy: copy.start())

    def issue(t, carry):
        src = _slab(src_ref, t * n_slabs, n_slabs)
        for k in range(TOP_K):
            pltpu.make_async_copy(src, _slab(dst_ref, pos_ref[k, t], n_slabs), sem).start()
        return carry

    lax.fori_loop(0, tile, issue, 0)
    for k in range(TOP_K):
        pltpu.make_async_copy(src_ref, dst_ref.at[pl.ds(0, tile * n_slabs)], sem).wait()

    @pl.when(i == 0)
    def _():
        for_each_pad_copy(lambda copy: copy.wait())


def _dispatch(hf, pos, padrow, padn, n_sorted, n_slabs):
    t = hf.shape[0] // n_slabs
    tile = min(SCATTER_TILE, t)
    pad_bits = (EXPERT_TILE - 1).bit_length()
    smem = pl.BlockSpec(memory_space=pltpu.SMEM)
    return pl.pallas_call(
        functools.partial(_dispatch_kernel, n_slabs=n_slabs, pad_bits=pad_bits),
        grid=(t // tile,),
        in_specs=[
            pl.BlockSpec((TOP_K, tile), lambda i: (0, i), memory_space=pltpu.SMEM),
            smem, smem,
            pl.BlockSpec((tile * n_slabs, LANES), lambda i: (i, 0)),
        ],
        out_specs=pl.BlockSpec(memory_space=pl.ANY),
        out_shape=jax.ShapeDtypeStruct((n_sorted * n_slabs, LANES), _f32),
        scratch_shapes=[pltpu.VMEM(((1 << (pad_bits - 1)) * n_slabs, LANES), _f32),
                        pltpu.SemaphoreType.DMA(()), pltpu.SemaphoreType.DMA(())],
        compiler_params=_params(("arbitrary",)),
        name="dispatch",
    )(pos, padrow, padn, hf)


def _expert_kernel(te_ref, tbi_ref, tbo_ref, tv_ref, x_ref, wg_ref, wu_ref, wd_ref, o_ref,
                   xb_ref, wgb_ref, wub_ref, wdb_ref, *, n_slabs):
    i = pl.program_id(0)
    tm = xb_ref.shape[0]
    live = tv_ref[i] > 0
    new_expert = (i == 0) | (te_ref[i] != te_ref[jnp.maximum(i - 1, 0)])

    @pl.when(live & new_expert)
    def _():
        wgb_ref[...] = wg_ref[...].astype(_bf16)
        wub_ref[...] = wu_ref[...].astype(_bf16)
        wdb_ref[...] = wd_ref[...].astype(_bf16)

    @pl.when(live)
    def _():
        for j in range(n_slabs):
            xb_ref[:, j * LANES:(j + 1) * LANES] = x_ref[pl.ds(j, tm, stride=n_slabs), :].astype(_bf16)
        x = xb_ref[...]
        g = _dot(x, wgb_ref[...])
        u = _dot(x, wub_ref[...])
        h = (g * jax.nn.sigmoid(g) * u).astype(_bf16)
        y = _dot(h, wdb_ref[...])
        for j in range(n_slabs):
            o_ref[pl.ds(j, tm, stride=n_slabs), :] = y[:, j * LANES:(j + 1) * LANES]

    @pl.when(jnp.logical_not(live))
    def _():
        o_ref[...] = jnp.zeros_like(o_ref)


def _experts(xs, w_gate, w_up, w_down, tile_expert, tile_in, tile_out, tile_valid):
    n_exp, d, f = w_gate.shape
    n_slabs = d // LANES
    tm = EXPERT_TILE
    n_tiles = tile_expert.shape[0]
    wsel = lambda i, te, tbi, tbo, tv: (te[i], 0, 0)
    return pl.pallas_call(
        functools.partial(_expert_kernel, n_slabs=n_slabs),
        grid_spec=pltpu.PrefetchScalarGridSpec(
            num_scalar_prefetch=4,
            grid=(n_tiles,),
            in_specs=[
                pl.BlockSpec((tm * n_slabs, LANES), lambda i, te, tbi, tbo, tv: (tbi[i], 0)),
                pl.BlockSpec((None, d, f), wsel),
                pl.BlockSpec((None, d, f), wsel),
                pl.BlockSpec((None, f, d), wsel),
            ],
            out_specs=pl.BlockSpec((tm * n_slabs, LANES), lambda i, te, tbi, tbo, tv: (tbo[i], 0)),
            scratch_shapes=[pltpu.VMEM((tm, d), _bf16), pltpu.VMEM((d, f), _bf16),
                            pltpu.VMEM((d, f), _bf16), pltpu.VMEM((f, d), _bf16)],
        ),
        out_shape=jax.ShapeDtypeStruct(xs.shape, _f32),
        compiler_params=_params(("arbitrary",)),
        name="experts",
    )(tile_expert, tile_in, tile_out, tile_valid, xs, w_gate, w_up, w_down)


def _combine_kernel(pos_ref, ys_ref, wt_ref, sh_ref, x1_ref, gpost_ref, mod_ref, *rest, n_slabs, n_first):
    final = n_first is not None
    if final:
        yctx_ref, ylat_ref, buf_ref, sem = rest
    else:
        gnext_ref, modn_ref, x2_ref, hn_ref, buf_ref, sem = rest
    tm = x1_ref.shape[0]

    def issue(t, carry):
        for k in range(TOP_K):
            pltpu.make_async_copy(_slab(ys_ref, pos_ref[k, t], n_slabs),
                                  _slab(buf_ref, (k * tm + t) * n_slabs, n_slabs), sem).start()
        return carry

    lax.fori_loop(0, tm, issue, 0)
    pltpu.make_async_copy(ys_ref.at[pl.ds(0, TOP_K * tm * n_slabs)], buf_ref, sem).wait()

    w = wt_ref[...]
    cols = []
    for j in range(n_slabs):
        acc = sh_ref[pl.ds(j, tm, stride=n_slabs), :]
        for k in range(TOP_K):
            piece = buf_ref[pl.ds(k * tm * n_slabs + j, tm, stride=n_slabs), :]
            acc = acc + w[:, k:k + 1] * piece
        cols.append(acc)
    f = jnp.concatenate(cols, axis=1)
    x2 = x1_ref[...] + mod_ref[pl.ds(MOD_GATE_FFN, 1), :] * _rms(f, gpost_ref[...])
    if final:
        in_ctx = pl.program_id(0) < n_first

        @pl.when(in_ctx)
        def _():
            yctx_ref[...] = x2

        @pl.when(jnp.logical_not(in_ctx))
        def _():
            ylat_ref[...] = x2
    else:
        x2_ref[...] = x2
        hn = (_rms(x2, gnext_ref[...]) * (1.0 + modn_ref[pl.ds(MOD_SCALE_MIX, 1), :])
              + modn_ref[pl.ds(MOD_SHIFT_MIX, 1), :])
        hn_ref[...] = hn.astype(hn_ref.dtype)


def _combine(ys, pos, wt, shared, x1, g_post, mod, n_ctx, dec_seq, g_next=None, mod_next=None):
    t, d = x1.shape
    n_slabs = d // LANES
    tm = min(GATHER_TILE, t)
    final = g_next is None
    row = lambda i: (i, 0)
    fixed = lambda i: (0, 0)
    cond = lambda i: (_cond_row(i * tm, n_ctx, dec_seq), 0, 0)
    in_specs = [
        pl.BlockSpec((TOP_K, tm), lambda i: (0, i), memory_space=pltpu.SMEM),
        pl.BlockSpec(memory_space=pl.ANY),
        pl.BlockSpec((tm, TOP_K), row),
        pl.BlockSpec((tm * n_slabs, LANES), row),
        pl.BlockSpec((tm, d), row),
        pl.BlockSpec((1, d), fixed),
        pl.BlockSpec((None, MOD_ROWS, d), cond),
    ]
    args = [pos, ys, wt, shared, x1, g_post.reshape(1, d), mod]
    if final:
        n0, n1 = n_ctx // tm, (t - n_ctx) // tm
        out_specs = [pl.BlockSpec((tm, d), lambda i: (jnp.minimum(i, n0 - 1), 0)),
                     pl.BlockSpec((tm, d), lambda i: (jnp.clip(i - n0, 0, n1 - 1), 0))]
        out_shape = [jax.ShapeDtypeStruct((n_ctx, d), _f32), jax.ShapeDtypeStruct((t - n_ctx, d), _f32)]
    else:
        in_specs += [pl.BlockSpec((1, d), fixed), pl.BlockSpec((None, MOD_ROWS, d), cond)]
        args += [g_next.reshape(1, d), mod_next]
        out_specs = [pl.BlockSpec((tm, d), row), pl.BlockSpec((tm, d), row)]
        out_shape = [jax.ShapeDtypeStruct((t, d), _f32), jax.ShapeDtypeStruct((t, d), _bf16)]
    return pl.pallas_call(
        functools.partial(_combine_kernel, n_slabs=n_slabs, n_first=n_ctx // tm if final else None),
        grid=(t // tm,),
        in_specs=in_specs,
        out_specs=out_specs,
        out_shape=out_shape,
        scratch_shapes=[pltpu.VMEM((TOP_K * tm * n_slabs, LANES), _f32), pltpu.SemaphoreType.DMA(())],
        compiler_params=_params(("arbitrary",)),
        name="combine",
    )(*args)


def _moe(hf, idx_t, wt_t, rank_t, cnt, w_gate, w_up, w_down, w_sh_gate, w_sh_up, w_sh_down):
    n_exp, d, _ = w_gate.shape
    n_slabs = d // LANES
    t = hf.shape[0] // n_slabs
    tm = EXPERT_TILE
    n_tiles = t * TOP_K // tm + n_exp
    pos, tile_expert, tile_in, tile_out, tile_valid, padrow, padn = _plan(
        cnt[:, 0], idx_t, rank_t, n_tiles, n_slabs)
    xs = _dispatch(hf, pos, padrow, padn, n_tiles * tm, n_slabs)
    ys = _experts(xs, w_gate, w_up, w_down, tile_expert, tile_in, tile_out, tile_valid)
    n_sh = t // tm
    every = jnp.arange(n_sh, dtype=jnp.int32)
    shared = _experts(hf, w_sh_gate[None], w_sh_up[None], w_sh_down[None],
                      jnp.zeros((n_sh,), jnp.int32), every, every, jnp.full((n_sh,), tm, jnp.int32))
    return ys, pos, wt_t.T, shared


def kernel(x_prompt, x_sample, cache_k, cache_v, c, c_ctx, w_ada, b_ada, norm_mix_pre, norm_mix_post, norm_ffn_pre, norm_ffn_post, w_qkv, w_o_attn, rpb, w_conv_in, conv_w, w_conv_out, w_router, b_router, w_exp_gate, w_exp_up, w_exp_down, w_sh_gate, w_sh_up, w_sh_down):
    batch, seq, d = x_prompt.shape
    dec_batch, dec_seq, _ = x_sample.shape
    depth = w_ada.shape[0]
    n_ctx, n_lat = batch * seq, dec_batch * dec_seq
    dh = d // N_HEADS
    past = cache_k.shape[2]

    x = [x_prompt.reshape(n_ctx, d), x_sample.reshape(n_lat, d)]
    cond = jnp.concatenate([c_ctx[None, :], c], axis=0)
    mod = _adaln(cond, w_ada, b_ada)

    new_k, new_v = [], []
    h = _modulate(x, norm_mix_pre[0], mod[0], n_ctx, dec_seq)
    for l in range(depth):
        if l % 2 == 0:
            a = l // 2
            qkv = _matmul(h, w_qkv[a])
            o_ctx, k_new, v_new = _ctx_attention(qkv, batch, seq, d)
            new_k.append(k_new.reshape(batch, seq, N_HEADS, dh))
            new_v.append(v_new.reshape(batch, seq, N_HEADS, dh))
            o_lat = _natten(qkv, n_ctx, cache_k[:, a].reshape(dec_batch, past, d),
                            cache_v[:, a].reshape(dec_batch, past, d), rpb[a], dec_batch, dec_seq, d)
            mixed = [o_ctx, o_lat]
            w_out = w_o_attn[a]
        else:
            m = l // 2
            mixed = [_conv_gate(_matmul(h, w_conv_in[m]), conv_w[m], n_ctx, seq, dec_seq, d)]
            w_out = w_conv_out[m]
        x1, hf, idx_t, wt_t, rank_t, cnt = _post_mixer(
            mixed, w_out, x, norm_mix_post[l], mod[l], norm_ffn_pre[l], w_router[l], b_router[l], n_ctx, dec_seq)
        ys, pos, wt, shared = _moe(hf, idx_t, wt_t, rank_t, cnt, w_exp_gate[l], w_exp_up[l], w_exp_down[l],
                                   w_sh_gate[l], w_sh_up[l], w_sh_down[l])
        if l + 1 < depth:
            x2, h = _combine(ys, pos, wt, shared, x1, norm_ffn_post[l], mod[l], n_ctx, dec_seq,
                             norm_mix_pre[l + 1], mod[l + 1])
            x = [x2]
        else:
            y_ctx, y_lat = _combine(ys, pos, wt, shared, x1, norm_ffn_post[l], mod[l], n_ctx, dec_seq)

    return (y_ctx.reshape(batch, seq, d), y_lat.reshape(dec_batch, dec_seq, d),
            jnp.stack(new_k, axis=1), jnp.stack(new_v, axis=1))
```

```python
import functools

import jax
import jax.numpy as jnp
from jax import lax
from jax.experimental import pallas as pl
from jax.experimental.pallas import tpu as pltpu

N_HEADS = 16
GRID_W = 64
WIN_ROWS = 8
WIN_COLS = 16
CONV_WIDTH = 3
N_EXPERTS = 64
TOP_K = 8
N_GROUPS = 8
TOPK_GROUPS = 4
ROUTED_SCALE = 2.5
N_MOD = 6
RMS_EPS = 1e-6
NEG_INF = -1e30

LANES = 128
SUBLANES = 8
VMEM_LIMIT = 56 * 1024 * 1024

MOD_SHIFT_MIX, MOD_SCALE_MIX, MOD_GATE_MIX, MOD_SHIFT_FFN, MOD_SCALE_FFN, MOD_GATE_FFN = range(6)
MOD_ROWS = 8

ROW_TILE = 256
EXPERT_TILE = 256
MM_TILE_M = 2048
MM_TILE_N = 512
GATHER_TILE = 128
COMBINE_SUB = 32
SCATTER_TILE = 512

_f32 = jnp.float32
_bf16 = jnp.bfloat16


def _params(sem, vmem=VMEM_LIMIT):
    return pltpu.CompilerParams(dimension_semantics=sem, vmem_limit_bytes=vmem)


def _rms(x, g):
    return x * lax.rsqrt(jnp.mean(x * x, axis=-1, keepdims=True) + RMS_EPS) * g


def _dot(a, b):
    return jnp.dot(a, b, preferred_element_type=_f32)


def _dot_nt(a, b):
    return lax.dot_general(a, b, (((1,), (1,)), ((), ())), preferred_element_type=_f32)


def _cond_row(row0, n_ctx, dec_seq):
    return jnp.where(row0 < n_ctx, 0, 1 + (row0 - n_ctx) // dec_seq)


def _adaln_kernel(cb_ref, w_ref, b_ref, o_ref, acc_ref, *, n_cond):
    k = pl.program_id(2)

    @pl.when(k == 0)
    def _():
        acc_ref[...] = jnp.zeros_like(acc_ref)

    tk, tn = w_ref.shape
    for r in range(n_cond):
        s = cb_ref[r]
        s = s * jax.nn.sigmoid(s)
        for c in range(tn // LANES):
            p = w_ref[:, c * LANES:(c + 1) * LANES] * s
            acc_ref[r, :, c * LANES:(c + 1) * LANES] += p.reshape(tk // SUBLANES, SUBLANES, LANES).sum(axis=0)

    @pl.when(k == pl.num_programs(2) - 1)
    def _():
        o_ref[...] = jnp.zeros_like(o_ref)
        for r in range(n_cond):
            o_ref[pl.ds(r, 1), :] = acc_ref[r].sum(axis=0, keepdims=True) + b_ref[...]


def _adaln(cond, w_ada, b_ada):
    n_cond, d = cond.shape
    n_layers, _, n6 = w_ada.shape
    tk, tn = min(512, d), min(2048, n6)
    cb = jnp.broadcast_to(cond[:, :, None], (n_cond, d, LANES))
    mod = pl.pallas_call(
        functools.partial(_adaln_kernel, n_cond=n_cond),
        grid=(n_layers, n6 // tn, d // tk),
        in_specs=[
            pl.BlockSpec((n_cond, tk, LANES), lambda l, n, k: (0, k, 0)),
            pl.BlockSpec((None, tk, tn), lambda l, n, k: (l, k, n)),
            pl.BlockSpec((None, 1, tn), lambda l, n, k: (l, 0, n)),
        ],
        out_specs=pl.BlockSpec((None, MOD_ROWS, tn), lambda l, n, k: (l, 0, n)),
        out_shape=jax.ShapeDtypeStruct((n_layers, MOD_ROWS, n6), _f32),
        scratch_shapes=[pltpu.VMEM((n_cond, SUBLANES, tn), _f32)],
        compiler_params=_params(("arbitrary", "arbitrary", "arbitrary")),
        name="adaln",
    )(cb, w_ada, b_ada.reshape(n_layers, 1, n6))
    mod = mod[:, :n_cond].reshape(n_layers, n_cond, N_MOD, d)
    return jnp.pad(mod, ((0, 0), (0, 0), (0, MOD_ROWS - N_MOD), (0, 0)))


def _split_specs(parts, tm, width):
    if len(parts) == 1:
        return [pl.BlockSpec((tm, width), lambda i: (i, 0))]
    n0 = parts[0].shape[0] // tm
    n1 = parts[1].shape[0] // tm
    return [pl.BlockSpec((tm, width), lambda i: (jnp.minimum(i, n0 - 1), 0)),
            pl.BlockSpec((tm, width), lambda i: (jnp.clip(i - n0, 0, n1 - 1), 0))]


def _split_read(refs, n_first):
    if len(refs) == 1:
        return refs[0][...]
    return jnp.where(pl.program_id(0) < n_first, refs[0][...], refs[1][...])


def _modulate_kernel(*refs, n_x, n_first):
    x_refs, (g_ref, mod_ref, o_ref) = refs[:n_x], refs[n_x:]
    y = _rms(_split_read(x_refs, n_first), g_ref[...])
    h = y * (1.0 + mod_ref[pl.ds(MOD_SCALE_MIX, 1), :]) + mod_ref[pl.ds(MOD_SHIFT_MIX, 1), :]
    o_ref[...] = h.astype(o_ref.dtype)


def _modulate(xs, g, mod, n_ctx, dec_seq):
    t = sum(x.shape[0] for x in xs)
    d = xs[0].shape[1]
    tm = ROW_TILE
    return pl.pallas_call(
        functools.partial(_modulate_kernel, n_x=len(xs), n_first=xs[0].shape[0] // tm),
        grid=(t // tm,),
        in_specs=_split_specs(xs, tm, d) + [
            pl.BlockSpec((1, d), lambda i: (0, 0)),
            pl.BlockSpec((None, MOD_ROWS, d), lambda i: (_cond_row(i * tm, n_ctx, dec_seq), 0, 0)),
        ],
        out_specs=pl.BlockSpec((tm, d), lambda i: (i, 0)),
        out_shape=jax.ShapeDtypeStruct((t, d), _bf16),
        compiler_params=_params(("parallel",)),
        name="modulate",
    )(*xs, g.reshape(1, d), mod)


def _matmul_kernel(a_ref, w_ref, o_ref):
    o_ref[...] = _dot(a_ref[...], w_ref[...].astype(_bf16)).astype(o_ref.dtype)


def _matmul(a, w):
    n_rows = a.shape[0]
    k, n = w.shape
    tm, tn = min(MM_TILE_M, n_rows), min(MM_TILE_N, n)
    return pl.pallas_call(
        _matmul_kernel,
        grid=(n_rows // tm, n // tn),
        in_specs=[
            pl.BlockSpec((tm, k), lambda i, j: (i, 0)),
            pl.BlockSpec((k, tn), lambda i, j: (0, j)),
        ],
        out_specs=pl.BlockSpec((tm, tn), lambda i, j: (i, j)),
        out_shape=jax.ShapeDtypeStruct((n_rows, n), _f32),
        compiler_params=_params(("parallel", "parallel")),
        name="matmul",
    )(a, w)


def _ctx_attn_kernel(q_ref, k_ref, v_ref, o_ref, kout_ref, vout_ref, *, n_heads, scale):
    seq = q_ref.shape[0]
    dh = q_ref.shape[1] // n_heads
    for h in range(n_heads):
        sl = slice(h * dh, (h + 1) * dh)
        k32 = k_ref[:, sl]
        v32 = v_ref[:, sl]
        kout_ref[pl.ds(h, seq, stride=n_heads), :] = k32
        vout_ref[pl.ds(h, seq, stride=n_heads), :] = v32
        q = q_ref[:, sl].astype(_bf16)
        s = _dot_nt(q, k32.astype(_bf16)) * scale
        p = jnp.exp(s - s.max(axis=-1, keepdims=True))
        o = _dot(p.astype(_bf16), v32.astype(_bf16)) / p.sum(axis=-1, keepdims=True)
        o_ref[:, sl] = o.astype(o_ref.dtype)


def _ctx_attention(qkv, batch, seq, d):
    dh = d // N_HEADS
    assert dh == LANES
    cache = jax.ShapeDtypeStruct((batch * seq * N_HEADS, dh), _f32)
    return pl.pallas_call(
        functools.partial(_ctx_attn_kernel, n_heads=N_HEADS, scale=dh ** -0.5),
        grid=(batch,),
        in_specs=[
            pl.BlockSpec((seq, d), lambda b: (b, 0)),
            pl.BlockSpec((seq, d), lambda b: (b, 1)),
            pl.BlockSpec((seq, d), lambda b: (b, 2)),
        ],
        out_specs=[
            pl.BlockSpec((seq, d), lambda b: (b, 0)),
            pl.BlockSpec((seq * N_HEADS, dh), lambda b: (b, 0)),
            pl.BlockSpec((seq * N_HEADS, dh), lambda b: (b, 0)),
        ],
        out_shape=[jax.ShapeDtypeStruct((batch * seq, d), _bf16), cache, cache],
        compiler_params=_params(("parallel",)),
        name="ctx_attention",
    )(qkv, qkv, qkv)


def _natten_kernel(q_ref, k_ref, v_ref, kc_ref, vc_ref, bias_ref, o_ref, *, rows, width, kh, scale):
    n_loc = kh * width
    kc = kc_ref[...].astype(_bf16)
    vc = vc_ref[...].astype(_bf16)
    q_col = lax.broadcasted_iota(jnp.int32, (width, n_loc), 0)
    k_col = lax.broadcasted_iota(jnp.int32, (width, n_loc), 1) % width
    col_start = jnp.clip(q_col - WIN_COLS // 2, 0, width - WIN_COLS)
    col_mask = (k_col >= col_start) & (k_col < col_start + WIN_COLS)

    def one_row(r, carry):
        r0 = jnp.clip(r - kh // 2, 0, rows - kh)
        q = q_ref[pl.ds(pl.multiple_of(r * width, width), width), :].astype(_bf16)
        win = pl.ds(pl.multiple_of(r0 * width, width), n_loc)
        kw = k_ref[win, :].astype(_bf16)
        vw = v_ref[win, :].astype(_bf16)
        s_loc = _dot_nt(q, kw) * scale + bias_ref[r - r0]
        s_loc = jnp.where(col_mask, s_loc, NEG_INF)
        s_ctx = _dot_nt(q, kc) * scale
        m = jnp.maximum(s_loc.max(axis=-1, keepdims=True), s_ctx.max(axis=-1, keepdims=True))
        p_loc = jnp.exp(s_loc - m)
        p_ctx = jnp.exp(s_ctx - m)
        den = p_loc.sum(axis=-1, keepdims=True) + p_ctx.sum(axis=-1, keepdims=True)
        o = _dot(p_loc.astype(_bf16), vw) + _dot(p_ctx.astype(_bf16), vc)
        o_ref[pl.ds(pl.multiple_of(r * width, width), width), :] = (o / den).astype(o_ref.dtype)
        return carry

    lax.fori_loop(0, rows, one_row, 0)


def _natten_bias(rpb, kh):
    col = jnp.arange(GRID_W)
    dc = jnp.clip(col[None, :] - col[:, None] + WIN_COLS - 1, 0, 2 * WIN_COLS - 2)
    dr = jnp.arange(kh)[None, :] - jnp.arange(kh)[:, None] + (WIN_ROWS - 1)
    t = rpb[:, dr][..., dc]
    return t.transpose(0, 1, 3, 2, 4).reshape(rpb.shape[0], kh, GRID_W, kh * GRID_W).astype(_f32)


def _natten(qkv, n_ctx, k_ctx, v_ctx, rpb, dec_batch, dec_seq, d):
    assert n_ctx % dec_seq == 0
    dh = d // N_HEADS
    rows = dec_seq // GRID_W
    kh = min(WIN_ROWS, rows)
    past = k_ctx.shape[1]
    bias = _natten_bias(rpb, kh)
    b0 = n_ctx // dec_seq
    return pl.pallas_call(
        functools.partial(_natten_kernel, rows=rows, width=GRID_W, kh=kh, scale=dh ** -0.5),
        grid=(dec_batch, N_HEADS),
        in_specs=[
            pl.BlockSpec((dec_seq, dh), lambda b, h: (b0 + b, h)),
            pl.BlockSpec((dec_seq, dh), lambda b, h: (b0 + b, N_HEADS + h)),
            pl.BlockSpec((dec_seq, dh), lambda b, h: (b0 + b, 2 * N_HEADS + h)),
            pl.BlockSpec((None, past, dh), lambda b, h: (b, 0, h)),
            pl.BlockSpec((None, past, dh), lambda b, h: (b, 0, h)),
            pl.BlockSpec((None, kh, GRID_W, kh * GRID_W), lambda b, h: (h, 0, 0, 0)),
        ],
        out_specs=pl.BlockSpec((dec_seq, dh), lambda b, h: (b, h)),
        out_shape=jax.ShapeDtypeStruct((dec_batch * dec_seq, d), _bf16),
        compiler_params=_params(("parallel", "parallel")),
        name="natten",
    )(qkv, qkv, qkv, k_ctx, v_ctx, bias)


def _conv_gate_kernel(b_ref, c_ref, u_ref, cp_ref, up_ref, cn_ref, un_ref, w_ref, o_ref,
                      *, n_ctx, seq, dec_seq):
    tm = o_ref.shape[0]
    row0 = pl.program_id(0) * tm
    in_ctx = row0 < n_ctx
    pos0 = jnp.where(in_ctx, row0 % seq, (row0 - n_ctx) % dec_seq)
    seq_len = jnp.where(in_ctx, seq, dec_seq)
    has_prev = pos0 > 0
    has_next = pos0 + tm < seq_len
    cu = c_ref[...] * u_ref[...]
    prev_row = jnp.where(has_prev, cp_ref[pl.ds(SUBLANES - 1, 1), :] * up_ref[pl.ds(SUBLANES - 1, 1), :], 0.0)
    next_row = jnp.where(has_next, cn_ref[pl.ds(0, 1), :] * un_ref[pl.ds(0, 1), :], 0.0)
    ridx = lax.broadcasted_iota(jnp.int32, cu.shape, 0)
    before = jnp.where(ridx == 0, prev_row, pltpu.roll(cu, 1, axis=0))
    after = jnp.where(ridx == tm - 1, next_row, pltpu.roll(cu, tm - 1, axis=0))
    conv = w_ref[pl.ds(0, 1), :] * before + w_ref[pl.ds(1, 1), :] * cu + w_ref[pl.ds(2, 1), :] * after
    o_ref[...] = (b_ref[...] * conv).astype(o_ref.dtype)


def _conv_gate(bcu, conv_w, n_ctx, seq, dec_seq, d):
    t = bcu.shape[0]
    tm = ROW_TILE
    halo = tm // SUBLANES
    last = t // SUBLANES - 1
    cw = jnp.pad(conv_w, ((0, SUBLANES - CONV_WIDTH), (0, 0)))
    prev_map = lambda col: (lambda i: (jnp.maximum(i * halo - 1, 0), col))
    next_map = lambda col: (lambda i: (jnp.minimum((i + 1) * halo, last), col))
    return pl.pallas_call(
        functools.partial(_conv_gate_kernel, n_ctx=n_ctx, seq=seq, dec_seq=dec_seq),
        grid=(t // tm,),
        in_specs=[
            pl.BlockSpec((tm, d), lambda i: (i, 0)),
            pl.BlockSpec((tm, d), lambda i: (i, 1)),
            pl.BlockSpec((tm, d), lambda i: (i, 2)),
            pl.BlockSpec((SUBLANES, d), prev_map(1)),
            pl.BlockSpec((SUBLANES, d), prev_map(2)),
            pl.BlockSpec((SUBLANES, d), next_map(1)),
            pl.BlockSpec((SUBLANES, d), next_map(2)),
            pl.BlockSpec((SUBLANES, d), lambda i: (0, 0)),
        ],
        out_specs=pl.BlockSpec((tm, d), lambda i: (i, 0)),
        out_shape=jax.ShapeDtypeStruct((t, d), _bf16),
        compiler_params=_params(("parallel",)),
        name="conv_gate",
    )(bcu, bcu, bcu, bcu, bcu, bcu, bcu, cw)


def _route(sel, scores):
    n_grp, eg, tm = sel.shape
    n_exp = n_grp * eg
    j_iota = lax.broadcasted_iota(jnp.int32, sel.shape, 1)
    m1 = sel.max(axis=1, keepdims=True)
    j1 = jnp.min(jnp.where(sel == m1, j_iota, eg), axis=1, keepdims=True)
    m2 = jnp.max(jnp.where(j_iota == j1, -jnp.inf, sel), axis=1, keepdims=True)
    grp = m1 + m2
    g_iota = lax.broadcasted_iota(jnp.int32, grp.shape, 0)
    g_sel = g_iota < 0
    for _ in range(TOPK_GROUPS):
        gm = grp.max(axis=0, keepdims=True)
        gi = jnp.min(jnp.where(grp == gm, g_iota, n_grp), axis=0, keepdims=True)
        hit = g_iota == gi
        g_sel = g_sel | hit
        grp = jnp.where(hit, -jnp.inf, grp)
    cur = jnp.where(jnp.broadcast_to(g_sel, sel.shape), sel, NEG_INF)
    e_iota = lax.broadcasted_iota(jnp.int32, sel.shape, 0) * eg + j_iota
    ids, ws, hits = [], [], []
    for _ in range(TOP_K):
        m = cur.max(axis=1, keepdims=True).max(axis=0, keepdims=True)
        ei = jnp.min(jnp.where(cur == m, e_iota, n_exp), axis=1, keepdims=True).min(axis=0, keepdims=True)
        hit = e_iota == ei
        ids.append(ei)
        ws.append(jnp.sum(jnp.where(hit, scores, 0.0), axis=1, keepdims=True).sum(axis=0, keepdims=True))
        hits.append(hit)
        cur = jnp.where(hit, -jnp.inf, cur)
    total = functools.reduce(lambda a, b: a + b, ws)
    ws = [w / total * ROUTED_SCALE for w in ws]
    return ids, ws, hits


def _post_mixer_kernel(*refs, n_a, n_x, n_first):
    a_refs, x_refs = refs[:n_a], refs[n_a:n_a + n_x]
    (wo_ref, gpost_ref, mod_ref, gpre_ref, wr_ref, br_ref,
     x1_ref, hf_ref, idx_ref, wt_ref, rank_ref, cnt_ref, carry_ref) = refs[n_a + n_x:]
    i = pl.program_id(0)
    tm = x1_ref.shape[0]

    @pl.when(i == 0)
    def _():
        carry_ref[...] = jnp.zeros_like(carry_ref)

    o = _dot(_split_read(a_refs, n_first), wo_ref[...])
    x1 = _split_read(x_refs, n_first) + mod_ref[pl.ds(MOD_GATE_MIX, 1), :] * _rms(o, gpost_ref[...])
    x1_ref[...] = x1
    hf = _rms(x1, gpre_ref[...]) * (1.0 + mod_ref[pl.ds(MOD_SCALE_FFN, 1), :]) + mod_ref[pl.ds(MOD_SHIFT_FFN, 1), :]
    hf_ref[...] = hf

    logits = lax.dot_general(wr_ref[...], hf, (((1,), (1,)), ((), ())),
                             precision=lax.Precision.HIGHEST, preferred_element_type=_f32)
    n_exp = logits.shape[0]
    grouped = (N_GROUPS, n_exp // N_GROUPS, tm)
    scores = jax.nn.sigmoid(logits)
    ids, ws, hits = _route((scores + br_ref[...]).reshape(grouped), scores.reshape(grouped))
    for k in range(TOP_K):
        idx_ref[pl.ds(k, 1), :] = ids[k][0]
        wt_ref[pl.ds(k, 1), :] = ws[k][0]

    any_hit = functools.reduce(lambda a, b: a | b, hits)
    mask = jnp.where(any_hit, 1.0, 0.0).reshape(n_exp, tm).astype(_bf16)
    t_src = lax.broadcasted_iota(jnp.int32, (tm, tm), 0)
    t_dst = lax.broadcasted_iota(jnp.int32, (tm, tm), 1)
    before = jnp.where(t_src < t_dst, 1.0, 0.0).astype(_bf16)
    rank = _dot(mask, before) + jnp.concatenate([carry_ref[...]] * (tm // LANES), axis=1)
    rank = rank.reshape(grouped)
    for k in range(TOP_K):
        rk = jnp.sum(jnp.where(hits[k], rank, 0.0), axis=1, keepdims=True).sum(axis=0, keepdims=True)
        rank_ref[pl.ds(k, 1), :] = rk[0].astype(jnp.int32)
    carry_ref[...] += _dot(mask, jnp.ones((tm, LANES), _bf16))
    cnt_ref[...] = carry_ref[...].astype(jnp.int32)


def _post_mixer(a_parts, w_out, x_parts, g_post, mod, g_pre, w_router, b_router, n_ctx, dec_seq):
    t = sum(x.shape[0] for x in x_parts)
    d = x_parts[0].shape[1]
    tm = ROW_TILE
    e = w_router.shape[1]
    row = lambda i: (i, 0)
    fixed = lambda i: (0, 0)
    col = lambda i: (0, i)
    return pl.pallas_call(
        functools.partial(_post_mixer_kernel, n_a=len(a_parts), n_x=len(x_parts), n_first=n_ctx // tm),
        grid=(t // tm,),
        in_specs=_split_specs(a_parts, tm, d) + _split_specs(x_parts, tm, d) + [
            pl.BlockSpec((d, d), fixed),
            pl.BlockSpec((1, d), fixed),
            pl.BlockSpec((None, MOD_ROWS, d), lambda i: (_cond_row(i * tm, n_ctx, dec_seq), 0, 0)),
            pl.BlockSpec((1, d), fixed),
            pl.BlockSpec((e, d), fixed),
            pl.BlockSpec((e, 1), fixed),
        ],
        out_specs=[
            pl.BlockSpec((tm, d), row),
            pl.BlockSpec((tm, d), row),
            pl.BlockSpec((TOP_K, tm), col),
            pl.BlockSpec((TOP_K, tm), col),
            pl.BlockSpec((TOP_K, tm), col),
            pl.BlockSpec((e, LANES), fixed),
        ],
        out_shape=[
            jax.ShapeDtypeStruct((t, d), _f32),
            jax.ShapeDtypeStruct((t, d), _f32),
            jax.ShapeDtypeStruct((TOP_K, t), jnp.int32),
            jax.ShapeDtypeStruct((TOP_K, t), _f32),
            jax.ShapeDtypeStruct((TOP_K, t), jnp.int32),
            jax.ShapeDtypeStruct((e, LANES), jnp.int32),
        ],
        scratch_shapes=[pltpu.VMEM((e, LANES), _f32)],
        compiler_params=_params(("arbitrary",)),
        name="post_mixer",
    )(*a_parts, *x_parts, w_out.astype(_bf16), g_post.reshape(1, d), mod, g_pre.reshape(1, d),
      w_router.T, b_router.reshape(e, 1))


def _row(ref, r):
    return ref.at[pl.ds(r, 1), :]


def _plan_kernel(cnt_ref, idx_ref, rank_ref, pos_ref, te_ref, tbi_ref, tbo_ref, tv_ref, padrow_ref, padn_ref,
                 offs_ref, *, tm):
    n_exp = cnt_ref.shape[0]
    n_tiles = te_ref.shape[0]

    def per_expert(e, carry):
        off, tile = carry
        cnt = cnt_ref[e]
        n_t = (cnt + tm - 1) // tm
        offs_ref[e] = off
        padrow_ref[e] = off + cnt
        padn_ref[e] = n_t * tm - cnt

        def per_tile(j, c):
            te_ref[tile + j] = e
            tbi_ref[tile + j] = tile + j
            tbo_ref[tile + j] = tile + j
            tv_ref[tile + j] = jnp.minimum(cnt - j * tm, tm)
            return c

        lax.fori_loop(0, n_t, per_tile, 0)
        return off + n_t * tm, tile + n_t

    _, live = lax.fori_loop(0, n_exp, per_expert, (jnp.int32(0), jnp.int32(0)))
    last = jnp.maximum(live - 1, 0)
    last_expert = te_ref[last]

    def dead_tile(i, c):
        te_ref[i] = last_expert
        tbi_ref[i] = last
        tbo_ref[i] = i
        tv_ref[i] = 0
        return c

    lax.fori_loop(live, n_tiles, dead_tile, 0)

    idx = idx_ref[...]
    pos = rank_ref[...]
    for e in range(n_exp):
        pos = pos + jnp.where(idx == e, offs_ref[e], 0)
    pos_ref[...] = pos


def _plan(counts, idx_t, rank_t, n_tiles):
    n_exp = counts.shape[0]
    smem = pl.BlockSpec(memory_space=pltpu.SMEM)
    vmem = pl.BlockSpec(memory_space=pltpu.VMEM)
    tiles = jax.ShapeDtypeStruct((n_tiles,), jnp.int32)
    experts = jax.ShapeDtypeStruct((n_exp,), jnp.int32)
    return pl.pallas_call(
        functools.partial(_plan_kernel, tm=EXPERT_TILE),
        in_specs=[smem, vmem, vmem],
        out_specs=[vmem, smem, smem, smem, smem, smem, smem],
        out_shape=[jax.ShapeDtypeStruct(idx_t.shape, jnp.int32), tiles, tiles, tiles, tiles, experts, experts],
        scratch_shapes=[pltpu.SMEM((n_exp,), jnp.int32)],
        name="plan",
    )(counts, idx_t, rank_t)


def _pad_copies(start, n, zero_ref, dst_ref, sem, pad_bits):
    single = n & (SUBLANES - 1)
    for s in range(SUBLANES - 1):
        yield s < single, pltpu.make_async_copy(_row(zero_ref, 0), _row(dst_ref, start + s), sem)
    base = pl.multiple_of(start + single, SUBLANES)
    groups = n // SUBLANES
    for b in range(pad_bits - (SUBLANES.bit_length() - 1)):
        rows = SUBLANES << b
        first = pl.multiple_of(base + ((groups >> (b + 1)) << (b + 1)) * SUBLANES, SUBLANES)
        copy = pltpu.make_async_copy(zero_ref.at[pl.ds(0, rows), :], dst_ref.at[pl.ds(first, rows), :], sem)
        yield ((groups >> b) & 1) == 1, copy


def _dispatch_kernel(pos_ref, padrow_ref, padn_ref, src_ref, dst_ref, zero_ref, sem, zero_sem, *, pad_bits):
    i = pl.program_id(0)
    tile = pos_ref.shape[1]
    n_exp = padn_ref.shape[0]

    def for_each_pad_copy(fn):
        def body(e, c):
            for needed, copy in _pad_copies(padrow_ref[e], padn_ref[e], zero_ref, dst_ref, zero_sem, pad_bits):
                pl.when(needed)(functools.partial(fn, copy))
            return c
        lax.fori_loop(0, n_exp, body, 0)

    @pl.when(i == 0)
    def _():
        zero_ref[...] = jnp.zeros_like(zero_ref)
        for_each_pad_copy(lambda copy: copy.start())

    def issue(t, carry):
        for k in range(TOP_K):
            pltpu.make_async_copy(_row(src_ref, t), _row(dst_ref, pos_ref[k, t]), sem).start()
        return carry

    lax.fori_loop(0, tile, issue, 0)
    for k in range(TOP_K):
        pltpu.make_async_copy(src_ref, dst_ref.at[pl.ds(0, tile), :], sem).wait()

    @pl.when(i == 0)
    def _():
        for_each_pad_copy(lambda copy: copy.wait())


def _dispatch(hf, pos, padrow, padn, n_sorted):
    t, d = hf.shape
    tile = min(SCATTER_TILE, t)
    pad_bits = (EXPERT_TILE - 1).bit_length()
    smem = pl.BlockSpec(memory_space=pltpu.SMEM)
    return pl.pallas_call(
        functools.partial(_dispatch_kernel, pad_bits=pad_bits),
        grid=(t // tile,),
        in_specs=[
            pl.BlockSpec((TOP_K, tile), lambda i: (0, i), memory_space=pltpu.SMEM),
            smem, smem,
            pl.BlockSpec((tile, d), lambda i: (i, 0)),
        ],
        out_specs=pl.BlockSpec(memory_space=pl.ANY),
        out_shape=jax.ShapeDtypeStruct((n_sorted, d), _f32),
        scratch_shapes=[pltpu.VMEM((1 << (pad_bits - 1), d), _f32),
                        pltpu.SemaphoreType.DMA(()), pltpu.SemaphoreType.DMA(())],
        compiler_params=_params(("arbitrary",)),
        name="dispatch",
    )(pos, padrow, padn, hf)


def _expert_kernel(te_ref, tbi_ref, tbo_ref, tv_ref, x_ref, wg_ref, wu_ref, wd_ref, o_ref,
                   wgb_ref, wub_ref, wdb_ref):
    i = pl.program_id(0)
    live = tv_ref[i] > 0
    new_expert = (i == 0) | (te_ref[i] != te_ref[jnp.maximum(i - 1, 0)])

    @pl.when(live & new_expert)
    def _():
        wgb_ref[...] = wg_ref[...].astype(_bf16)
        wub_ref[...] = wu_ref[...].astype(_bf16)
        wdb_ref[...] = wd_ref[...].astype(_bf16)

    @pl.when(live)
    def _():
        x = x_ref[...].astype(_bf16)
        g = _dot(x, wgb_ref[...])
        u = _dot(x, wub_ref[...])
        h = (g * jax.nn.sigmoid(g) * u).astype(_bf16)
        o_ref[...] = _dot(h, wdb_ref[...])

    @pl.when(jnp.logical_not(live))
    def _():
        o_ref[...] = jnp.zeros_like(o_ref)


def _experts(xs, w_gate, w_up, w_down, layer, tile_expert, tile_in, tile_out, tile_valid):
    _, n_exp, d, f = w_gate.shape
    tm = EXPERT_TILE
    n_tiles = tile_expert.shape[0]
    wsel = lambda i, te, tbi, tbo, tv: (layer, te[i], 0, 0)
    return pl.pallas_call(
        _expert_kernel,
        grid_spec=pltpu.PrefetchScalarGridSpec(
            num_scalar_prefetch=4,
            grid=(n_tiles,),
            in_specs=[
                pl.BlockSpec((tm, d), lambda i, te, tbi, tbo, tv: (tbi[i], 0)),
                pl.BlockSpec((None, None, d, f), wsel),
                pl.BlockSpec((None, None, d, f), wsel),
                pl.BlockSpec((None, None, f, d), wsel),
            ],
            out_specs=pl.BlockSpec((tm, d), lambda i, te, tbi, tbo, tv: (tbo[i], 0)),
            scratch_shapes=[pltpu.VMEM((d, f), _bf16), pltpu.VMEM((d, f), _bf16), pltpu.VMEM((f, d), _bf16)],
        ),
        out_shape=jax.ShapeDtypeStruct(xs.shape, _f32),
        compiler_params=_params(("arbitrary",)),
        name="experts",
    )(tile_expert, tile_in, tile_out, tile_valid, xs, w_gate, w_up, w_down)


def _combine_kernel(pos_ref, posn_ref, ys_ref, wt_ref, sh_ref, x1_ref, gpost_ref, mod_ref, *rest, n_first):
    final = n_first is not None
    if final:
        yctx_ref, ylat_ref, buf_ref, f_ref, sems = rest
    else:
        gnext_ref, modn_ref, x2_ref, hn_ref, buf_ref, f_ref, sems = rest
    tm, d = x1_ref.shape
    i = pl.program_id(0)
    slot = i % 2

    def gather(p_ref, s):
        def issue(t, carry):
            for k in range(TOP_K):
                pltpu.make_async_copy(_row(ys_ref, p_ref[k, t]),
                                      buf_ref.at[s, pl.ds(k * tm + t, 1), :], sems.at[s]).start()
            return carry
        lax.fori_loop(0, tm, issue, 0)

    @pl.when(i == 0)
    def _():
        gather(pos_ref, 0)

    @pl.when(i + 1 < pl.num_programs(0))
    def _():
        gather(posn_ref, 1 - slot)

    pltpu.make_async_copy(ys_ref.at[pl.ds(0, TOP_K * tm), :], buf_ref.at[slot], sems.at[slot]).wait()

    def sum_rows(r, carry):
        rows = pl.ds(pl.multiple_of(r * COMBINE_SUB, COMBINE_SUB), COMBINE_SUB)
        w = wt_ref[rows, :]
        wb = [jnp.broadcast_to(w[:, k:k + 1], (COMBINE_SUB, LANES)) for k in range(TOP_K)]
        for j in range(d // LANES):
            cols = slice(j * LANES, (j + 1) * LANES)
            acc = sh_ref[rows, cols]
            for k in range(TOP_K):
                src = pl.ds(pl.multiple_of(k * tm + r * COMBINE_SUB, COMBINE_SUB), COMBINE_SUB)
                acc = acc + wb[k] * buf_ref[slot, src, cols]
            f_ref[rows, cols] = acc
        return carry

    lax.fori_loop(0, tm // COMBINE_SUB, sum_rows, 0)
    f = f_ref[...]
    x2 = x1_ref[...] + mod_ref[pl.ds(MOD_GATE_FFN, 1), :] * _rms(f, gpost_ref[...])
    if final:
        in_ctx = pl.program_id(0) < n_first

        @pl.when(in_ctx)
        def _():
            yctx_ref[...] = x2

        @pl.when(jnp.logical_not(in_ctx))
        def _():
            ylat_ref[...] = x2
    else:
        x2_ref[...] = x2
        hn = (_rms(x2, gnext_ref[...]) * (1.0 + modn_ref[pl.ds(MOD_SCALE_MIX, 1), :])
              + modn_ref[pl.ds(MOD_SHIFT_MIX, 1), :])
        hn_ref[...] = hn.astype(hn_ref.dtype)


def _combine(ys, pos, wt, shared, x1, g_post, mod, n_ctx, dec_seq, g_next=None, mod_next=None):
    t, d = x1.shape
    tm = min(GATHER_TILE, t)
    n_steps = t // tm
    final = g_next is None
    row = lambda i: (i, 0)
    fixed = lambda i: (0, 0)
    cond = lambda i: (_cond_row(i * tm, n_ctx, dec_seq), 0, 0)
    in_specs = [
        pl.BlockSpec((TOP_K, tm), lambda i: (0, i), memory_space=pltpu.SMEM),
        pl.BlockSpec((TOP_K, tm), lambda i: (0, jnp.minimum(i + 1, n_steps - 1)), memory_space=pltpu.SMEM),
        pl.BlockSpec(memory_space=pl.ANY),
        pl.BlockSpec((tm, TOP_K), row),
        pl.BlockSpec((tm, d), row),
        pl.BlockSpec((tm, d), row),
        pl.BlockSpec((1, d), fixed),
        pl.BlockSpec((None, MOD_ROWS, d), cond),
    ]
    args = [pos, pos, ys, wt, shared, x1, g_post.reshape(1, d), mod]
    if final:
        n0, n1 = n_ctx // tm, (t - n_ctx) // tm
        out_specs = [pl.BlockSpec((tm, d), lambda i: (jnp.minimum(i, n0 - 1), 0)),
                     pl.BlockSpec((tm, d), lambda i: (jnp.clip(i - n0, 0, n1 - 1), 0))]
        out_shape = [jax.ShapeDtypeStruct((n_ctx, d), _f32), jax.ShapeDtypeStruct((t - n_ctx, d), _f32)]
    else:
        in_specs += [pl.BlockSpec((1, d), fixed), pl.BlockSpec((None, MOD_ROWS, d), cond)]
        args += [g_next.reshape(1, d), mod_next]
        out_specs = [pl.BlockSpec((tm, d), row), pl.BlockSpec((tm, d), row)]
        out_shape = [jax.ShapeDtypeStruct((t, d), _f32), jax.ShapeDtypeStruct((t, d), _bf16)]
    return pl.pallas_call(
        functools.partial(_combine_kernel, n_first=n_ctx // tm if final else None),
        grid=(n_steps,),
        in_specs=in_specs,
        out_specs=out_specs,
        out_shape=out_shape,
        scratch_shapes=[pltpu.VMEM((2, TOP_K * tm, d), _f32), pltpu.VMEM((tm, d), _f32),
                        pltpu.SemaphoreType.DMA((2,))],
        compiler_params=_params(("arbitrary",)),
        name="combine",
    )(*args)


def _moe(hf, idx_t, wt_t, rank_t, cnt, layer, w_gate, w_up, w_down, w_sh_gate, w_sh_up, w_sh_down):
    n_exp, d = w_gate.shape[1:3]
    t = hf.shape[0]
    tm = EXPERT_TILE
    n_tiles = t * TOP_K // tm + n_exp
    pos, tile_expert, tile_in, tile_out, tile_valid, padrow, padn = _plan(cnt[:, 0], idx_t, rank_t, n_tiles)
    xs = _dispatch(hf, pos, padrow, padn, n_tiles * tm)
    ys = _experts(xs, w_gate, w_up, w_down, layer, tile_expert, tile_in, tile_out, tile_valid)
    n_sh = t // tm
    every = jnp.arange(n_sh, dtype=jnp.int32)
    shared = _experts(hf, w_sh_gate[:, None], w_sh_up[:, None], w_sh_down[:, None], layer,
                      jnp.zeros((n_sh,), jnp.int32), every, every, jnp.full((n_sh,), tm, jnp.int32))
    return ys, pos, wt_t.T, shared


def kernel(x_prompt, x_sample, cache_k, cache_v, c, c_ctx, w_ada, b_ada, norm_mix_pre, norm_mix_post, norm_ffn_pre, norm_ffn_post, w_qkv, w_o_attn, rpb, w_conv_in, conv_w, w_conv_out, w_router, b_router, w_exp_gate, w_exp_up, w_exp_down, w_sh_gate, w_sh_up, w_sh_down):
    batch, seq, d = x_prompt.shape
    dec_batch, dec_seq, _ = x_sample.shape
    depth = w_ada.shape[0]
    n_ctx, n_lat = batch * seq, dec_batch * dec_seq
    dh = d // N_HEADS
    past = cache_k.shape[2]

    x = [x_prompt.reshape(n_ctx, d), x_sample.reshape(n_lat, d)]
    cond = jnp.concatenate([c_ctx[None, :], c], axis=0)
    mod = _adaln(cond, w_ada, b_ada)

    new_k, new_v = [], []
    h = _modulate(x, norm_mix_pre[0], mod[0], n_ctx, dec_seq)
    for l in range(depth):
        if l % 2 == 0:
            a = l // 2
            qkv = _matmul(h, w_qkv[a])
            o_ctx, k_new, v_new = _ctx_attention(qkv, batch, seq, d)
            new_k.append(k_new.reshape(batch, seq, N_HEADS, dh))
            new_v.append(v_new.reshape(batch, seq, N_HEADS, dh))
            o_lat = _natten(qkv, n_ctx, cache_k[:, a].reshape(dec_batch, past, d),
                            cache_v[:, a].reshape(dec_batch, past, d), rpb[a], dec_batch, dec_seq, d)
            mixed = [o_ctx, o_lat]
            w_out = w_o_attn[a]
        else:
            m = l // 2
            mixed = [_conv_gate(_matmul(h, w_conv_in[m]), conv_w[m], n_ctx, seq, dec_seq, d)]
            w_out = w_conv_out[m]
        x1, hf, idx_t, wt_t, rank_t, cnt = _post_mixer(
            mixed, w_out, x, norm_mix_post[l], mod[l], norm_ffn_pre[l], w_router[l], b_router[l], n_ctx, dec_seq)
        ys, pos, wt, shared = _moe(hf, idx_t, wt_t, rank_t, cnt, l, w_exp_gate, w_exp_up, w_exp_down,
                                   w_sh_gate, w_sh_up, w_sh_down)
        if l + 1 < depth:
            x2, h = _combine(ys, pos, wt, shared, x1, norm_ffn_post[l], mod[l], n_ctx, dec_seq,
                             norm_mix_pre[l + 1], mod[l + 1])
            x = [x2]
        else:
            y_ctx, y_lat = _combine(ys, pos, wt, shared, x1, norm_ffn_post[l], mod[l], n_ctx, dec_seq)

    return (y_ctx.reshape(batch, seq, d), y_lat.reshape(dec_batch, dec_seq, d),
            jnp.stack(new_k, axis=1), jnp.stack(new_v, axis=1))
```

```python
import functools

import jax
import jax.numpy as jnp
from jax import lax
from jax.experimental import pallas as pl
from jax.experimental.pallas import tpu as pltpu

N_HEADS = 16
GRID_W = 64
WIN_ROWS = 8
WIN_COLS = 16
CONV_WIDTH = 3
N_EXPERTS = 64
TOP_K = 8
N_GROUPS = 8
TOPK_GROUPS = 4
ROUTED_SCALE = 2.5
N_MOD = 6
RMS_EPS = 1e-6
NEG_INF = -1e30

LANES = 128
SUBLANES = 8
VMEM_LIMIT = 56 * 1024 * 1024

MOD_SHIFT_MIX, MOD_SCALE_MIX, MOD_GATE_MIX, MOD_SHIFT_FFN, MOD_SCALE_FFN, MOD_GATE_FFN = range(6)
MOD_ROWS = 8

ROW_TILE = 256
ROUTER_TILE = 1024
EXPERT_TILE = 256
MM_TILE_M = 2048
MM_TILE_N = 512
GATHER_TILE = 128
COMBINE_SUB = 32
SCATTER_TILE = 512

_f32 = jnp.float32
_bf16 = jnp.bfloat16


def _params(sem, vmem=VMEM_LIMIT):
    return pltpu.CompilerParams(dimension_semantics=sem, vmem_limit_bytes=vmem)


def _rms(x, g):
    return x * lax.rsqrt(jnp.mean(x * x, axis=-1, keepdims=True) + RMS_EPS) * g


def _dot(a, b):
    return jnp.dot(a, b, preferred_element_type=_f32)


def _dot_nt(a, b):
    return lax.dot_general(a, b, (((1,), (1,)), ((), ())), preferred_element_type=_f32)


def _cond_row(row0, n_ctx, dec_seq):
    return jnp.where(row0 < n_ctx, 0, 1 + (row0 - n_ctx) // dec_seq)


def _adaln_kernel(cb_ref, w_ref, b_ref, o_ref, acc_ref, *, n_cond):
    k = pl.program_id(2)

    @pl.when(k == 0)
    def _():
        acc_ref[...] = jnp.zeros_like(acc_ref)

    tk, tn = w_ref.shape
    for r in range(n_cond):
        s = cb_ref[r]
        s = s * jax.nn.sigmoid(s)
        for c in range(tn // LANES):
            p = w_ref[:, c * LANES:(c + 1) * LANES] * s
            acc_ref[r, :, c * LANES:(c + 1) * LANES] += p.reshape(tk // SUBLANES, SUBLANES, LANES).sum(axis=0)

    @pl.when(k == pl.num_programs(2) - 1)
    def _():
        o_ref[...] = jnp.zeros_like(o_ref)
        for r in range(n_cond):
            o_ref[pl.ds(r, 1), :] = acc_ref[r].sum(axis=0, keepdims=True) + b_ref[...]


def _adaln(cond, w_ada, b_ada):
    n_cond, d = cond.shape
    n_layers, _, n6 = w_ada.shape
    tk, tn = min(512, d), min(2048, n6)
    cb = jnp.broadcast_to(cond[:, :, None], (n_cond, d, LANES))
    mod = pl.pallas_call(
        functools.partial(_adaln_kernel, n_cond=n_cond),
        grid=(n_layers, n6 // tn, d // tk),
        in_specs=[
            pl.BlockSpec((n_cond, tk, LANES), lambda l, n, k: (0, k, 0)),
            pl.BlockSpec((None, tk, tn), lambda l, n, k: (l, k, n)),
            pl.BlockSpec((None, 1, tn), lambda l, n, k: (l, 0, n)),
        ],
        out_specs=pl.BlockSpec((None, MOD_ROWS, tn), lambda l, n, k: (l, 0, n)),
        out_shape=jax.ShapeDtypeStruct((n_layers, MOD_ROWS, n6), _f32),
        scratch_shapes=[pltpu.VMEM((n_cond, SUBLANES, tn), _f32)],
        compiler_params=_params(("arbitrary", "arbitrary", "arbitrary")),
        name="adaln",
    )(cb, w_ada, b_ada.reshape(n_layers, 1, n6))
    mod = mod[:, :n_cond].reshape(n_layers, n_cond, N_MOD, d)
    return jnp.pad(mod, ((0, 0), (0, 0), (0, MOD_ROWS - N_MOD), (0, 0)))


def _split_specs(parts, tm, width):
    if len(parts) == 1:
        return [pl.BlockSpec((tm, width), lambda i: (i, 0))]
    n0 = parts[0].shape[0] // tm
    n1 = parts[1].shape[0] // tm
    return [pl.BlockSpec((tm, width), lambda i: (jnp.minimum(i, n0 - 1), 0)),
            pl.BlockSpec((tm, width), lambda i: (jnp.clip(i - n0, 0, n1 - 1), 0))]


def _split_read(refs, n_first):
    if len(refs) == 1:
        return refs[0][...]
    return jnp.where(pl.program_id(0) < n_first, refs[0][...], refs[1][...])


def _modulate_kernel(*refs, n_x, n_first):
    x_refs, (g_ref, mod_ref, o_ref) = refs[:n_x], refs[n_x:]
    y = _rms(_split_read(x_refs, n_first), g_ref[...])
    h = y * (1.0 + mod_ref[pl.ds(MOD_SCALE_MIX, 1), :]) + mod_ref[pl.ds(MOD_SHIFT_MIX, 1), :]
    o_ref[...] = h.astype(o_ref.dtype)


def _modulate(xs, g, mod, n_ctx, dec_seq):
    t = sum(x.shape[0] for x in xs)
    d = xs[0].shape[1]
    tm = ROW_TILE
    return pl.pallas_call(
        functools.partial(_modulate_kernel, n_x=len(xs), n_first=xs[0].shape[0] // tm),
        grid=(t // tm,),
        in_specs=_split_specs(xs, tm, d) + [
            pl.BlockSpec((1, d), lambda i: (0, 0)),
            pl.BlockSpec((None, MOD_ROWS, d), lambda i: (_cond_row(i * tm, n_ctx, dec_seq), 0, 0)),
        ],
        out_specs=pl.BlockSpec((tm, d), lambda i: (i, 0)),
        out_shape=jax.ShapeDtypeStruct((t, d), _bf16),
        compiler_params=_params(("parallel",)),
        name="modulate",
    )(*xs, g.reshape(1, d), mod)


def _matmul_kernel(a_ref, w_ref, o_ref):
    o_ref[...] = _dot(a_ref[...], w_ref[...].astype(_bf16)).astype(o_ref.dtype)


def _matmul(a, w):
    n_rows = a.shape[0]
    k, n = w.shape
    tm, tn = min(MM_TILE_M, n_rows), min(MM_TILE_N, n)
    return pl.pallas_call(
        _matmul_kernel,
        grid=(n_rows // tm, n // tn),
        in_specs=[
            pl.BlockSpec((tm, k), lambda i, j: (i, 0)),
            pl.BlockSpec((k, tn), lambda i, j: (0, j)),
        ],
        out_specs=pl.BlockSpec((tm, tn), lambda i, j: (i, j)),
        out_shape=jax.ShapeDtypeStruct((n_rows, n), _f32),
        compiler_params=_params(("parallel", "parallel")),
        name="matmul",
    )(a, w)


def _ctx_attn_kernel(q_ref, k_ref, v_ref, o_ref, kout_ref, vout_ref, *, n_heads, scale):
    seq = q_ref.shape[0]
    dh = q_ref.shape[1] // n_heads
    for h in range(n_heads):
        sl = slice(h * dh, (h + 1) * dh)
        k32 = k_ref[:, sl]
        v32 = v_ref[:, sl]
        kout_ref[pl.ds(h, seq, stride=n_heads), :] = k32
        vout_ref[pl.ds(h, seq, stride=n_heads), :] = v32
        q = q_ref[:, sl].astype(_bf16)
        s = _dot_nt(q, k32.astype(_bf16)) * scale
        p = jnp.exp(s - s.max(axis=-1, keepdims=True))
        o = _dot(p.astype(_bf16), v32.astype(_bf16)) / p.sum(axis=-1, keepdims=True)
        o_ref[:, sl] = o.astype(o_ref.dtype)


def _ctx_attention(qkv, batch, seq, d):
    dh = d // N_HEADS
    assert dh == LANES
    cache = jax.ShapeDtypeStruct((batch * seq * N_HEADS, dh), _f32)
    return pl.pallas_call(
        functools.partial(_ctx_attn_kernel, n_heads=N_HEADS, scale=dh ** -0.5),
        grid=(batch,),
        in_specs=[
            pl.BlockSpec((seq, d), lambda b: (b, 0)),
            pl.BlockSpec((seq, d), lambda b: (b, 1)),
            pl.BlockSpec((seq, d), lambda b: (b, 2)),
        ],
        out_specs=[
            pl.BlockSpec((seq, d), lambda b: (b, 0)),
            pl.BlockSpec((seq * N_HEADS, dh), lambda b: (b, 0)),
            pl.BlockSpec((seq * N_HEADS, dh), lambda b: (b, 0)),
        ],
        out_shape=[jax.ShapeDtypeStruct((batch * seq, d), _bf16), cache, cache],
        compiler_params=_params(("parallel",)),
        name="ctx_attention",
    )(qkv, qkv, qkv)


def _natten_kernel(q_ref, k_ref, v_ref, kc_ref, vc_ref, bias_ref, o_ref, sctx_ref, pctx_ref, oloc_ref, den_ref,
                   *, rows, width, kh, scale, unroll):
    n_loc = kh * width
    q_col = lax.broadcasted_iota(jnp.int32, (width, n_loc), 0)
    k_col = lax.broadcasted_iota(jnp.int32, (width, n_loc), 1) % width
    col_start = jnp.clip(q_col - WIN_COLS // 2, 0, width - WIN_COLS)
    col_mask = (k_col >= col_start) & (k_col < col_start + WIN_COLS)

    sctx_ref[...] = _dot_nt(q_ref[...].astype(_bf16), kc_ref[...].astype(_bf16)) * scale

    def one_row(r):
        r0 = jnp.clip(r - kh // 2, 0, rows - kh)
        qrows = pl.ds(pl.multiple_of(r * width, width), width)
        win = pl.ds(pl.multiple_of(r0 * width, width), n_loc)
        q = q_ref[qrows, :].astype(_bf16)
        s_loc = _dot_nt(q, k_ref[win, :].astype(_bf16)) * scale + bias_ref[r - r0]
        s_loc = jnp.where(col_mask, s_loc, NEG_INF)
        s_ctx = sctx_ref[qrows, :]
        m = jnp.maximum(s_loc.max(axis=-1, keepdims=True), s_ctx.max(axis=-1, keepdims=True))
        p_loc = jnp.exp(s_loc - m)
        p_ctx = jnp.exp(s_ctx - m)
        den = p_loc.sum(axis=-1, keepdims=True) + p_ctx.sum(axis=-1, keepdims=True)
        pctx_ref[qrows, :] = p_ctx.astype(_bf16)
        oloc_ref[qrows, :] = _dot(p_loc.astype(_bf16), v_ref[win, :].astype(_bf16))
        den_ref[qrows, :] = jnp.broadcast_to(den, (width, den_ref.shape[1]))

    def some_rows(g, carry):
        for u in range(unroll):
            one_row(g * unroll + u)
        return carry

    lax.fori_loop(0, rows // unroll, some_rows, 0)
    o = oloc_ref[...] + _dot(pctx_ref[...], vc_ref[...].astype(_bf16))
    o_ref[...] = (o / den_ref[...]).astype(o_ref.dtype)


def _natten_bias(rpb, kh):
    col = jnp.arange(GRID_W)
    dc = jnp.clip(col[None, :] - col[:, None] + WIN_COLS - 1, 0, 2 * WIN_COLS - 2)
    dr = jnp.arange(kh)[None, :] - jnp.arange(kh)[:, None] + (WIN_ROWS - 1)
    t = rpb[:, dr][..., dc]
    return t.transpose(0, 1, 3, 2, 4).reshape(rpb.shape[0], kh, GRID_W, kh * GRID_W).astype(_f32)


def _natten(qkv, n_ctx, k_ctx, v_ctx, rpb, dec_batch, dec_seq, d):
    assert n_ctx % dec_seq == 0
    dh = d // N_HEADS
    rows = dec_seq // GRID_W
    kh = min(WIN_ROWS, rows)
    past = k_ctx.shape[1]
    bias = _natten_bias(rpb, kh)
    b0 = n_ctx // dec_seq
    unroll = 4 if rows % 4 == 0 else 1
    return pl.pallas_call(
        functools.partial(_natten_kernel, rows=rows, width=GRID_W, kh=kh, scale=dh ** -0.5, unroll=unroll),
        grid=(dec_batch, N_HEADS),
        in_specs=[
            pl.BlockSpec((dec_seq, dh), lambda b, h: (b0 + b, h)),
            pl.BlockSpec((dec_seq, dh), lambda b, h: (b0 + b, N_HEADS + h)),
            pl.BlockSpec((dec_seq, dh), lambda b, h: (b0 + b, 2 * N_HEADS + h)),
            pl.BlockSpec((None, past, dh), lambda b, h: (b, 0, h)),
            pl.BlockSpec((None, past, dh), lambda b, h: (b, 0, h)),
            pl.BlockSpec((None, kh, GRID_W, kh * GRID_W), lambda b, h: (h, 0, 0, 0)),
        ],
        out_specs=pl.BlockSpec((dec_seq, dh), lambda b, h: (b, h)),
        out_shape=jax.ShapeDtypeStruct((dec_batch * dec_seq, d), _bf16),
        scratch_shapes=[pltpu.VMEM((dec_seq, past), _f32), pltpu.VMEM((dec_seq, past), _bf16),
                        pltpu.VMEM((dec_seq, dh), _f32), pltpu.VMEM((dec_seq, dh), _f32)],
        compiler_params=_params(("parallel", "parallel")),
        name="natten",
    )(qkv, qkv, qkv, k_ctx, v_ctx, bias)


def _conv_gate_kernel(b_ref, c_ref, u_ref, cp_ref, up_ref, cn_ref, un_ref, w_ref, o_ref,
                      *, n_ctx, seq, dec_seq):
    tm = o_ref.shape[0]
    row0 = pl.program_id(0) * tm
    in_ctx = row0 < n_ctx
    pos0 = jnp.where(in_ctx, row0 % seq, (row0 - n_ctx) % dec_seq)
    seq_len = jnp.where(in_ctx, seq, dec_seq)
    has_prev = pos0 > 0
    has_next = pos0 + tm < seq_len
    cu = c_ref[...] * u_ref[...]
    prev_row = jnp.where(has_prev, cp_ref[pl.ds(SUBLANES - 1, 1), :] * up_ref[pl.ds(SUBLANES - 1, 1), :], 0.0)
    next_row = jnp.where(has_next, cn_ref[pl.ds(0, 1), :] * un_ref[pl.ds(0, 1), :], 0.0)
    ridx = lax.broadcasted_iota(jnp.int32, cu.shape, 0)
    before = jnp.where(ridx == 0, prev_row, pltpu.roll(cu, 1, axis=0))
    after = jnp.where(ridx == tm - 1, next_row, pltpu.roll(cu, tm - 1, axis=0))
    conv = w_ref[pl.ds(0, 1), :] * before + w_ref[pl.ds(1, 1), :] * cu + w_ref[pl.ds(2, 1), :] * after
    o_ref[...] = (b_ref[...] * conv).astype(o_ref.dtype)


def _conv_gate(bcu, conv_w, n_ctx, seq, dec_seq, d):
    t = bcu.shape[0]
    tm = ROW_TILE
    halo = tm // SUBLANES
    last = t // SUBLANES - 1
    cw = jnp.pad(conv_w, ((0, SUBLANES - CONV_WIDTH), (0, 0)))
    prev_map = lambda col: (lambda i: (jnp.maximum(i * halo - 1, 0), col))
    next_map = lambda col: (lambda i: (jnp.minimum((i + 1) * halo, last), col))
    return pl.pallas_call(
        functools.partial(_conv_gate_kernel, n_ctx=n_ctx, seq=seq, dec_seq=dec_seq),
        grid=(t // tm,),
        in_specs=[
            pl.BlockSpec((tm, d), lambda i: (i, 0)),
            pl.BlockSpec((tm, d), lambda i: (i, 1)),
            pl.BlockSpec((tm, d), lambda i: (i, 2)),
            pl.BlockSpec((SUBLANES, d), prev_map(1)),
            pl.BlockSpec((SUBLANES, d), prev_map(2)),
            pl.BlockSpec((SUBLANES, d), next_map(1)),
            pl.BlockSpec((SUBLANES, d), next_map(2)),
            pl.BlockSpec((SUBLANES, d), lambda i: (0, 0)),
        ],
        out_specs=pl.BlockSpec((tm, d), lambda i: (i, 0)),
        out_shape=jax.ShapeDtypeStruct((t, d), _bf16),
        compiler_params=_params(("parallel",)),
        name="conv_gate",
    )(bcu, bcu, bcu, bcu, bcu, bcu, bcu, cw)


def _route(sel, scores):
    n_grp, eg, tm = sel.shape
    n_exp = n_grp * eg
    j_iota = lax.broadcasted_iota(jnp.int32, sel.shape, 1)
    m1 = sel.max(axis=1, keepdims=True)
    j1 = jnp.min(jnp.where(sel == m1, j_iota, eg), axis=1, keepdims=True)
    m2 = jnp.max(jnp.where(j_iota == j1, -jnp.inf, sel), axis=1, keepdims=True)
    grp = m1 + m2
    g_iota = lax.broadcasted_iota(jnp.int32, grp.shape, 0)
    g_sel = g_iota < 0
    for _ in range(TOPK_GROUPS):
        gm = grp.max(axis=0, keepdims=True)
        gi = jnp.min(jnp.where(grp == gm, g_iota, n_grp), axis=0, keepdims=True)
        hit = g_iota == gi
        g_sel = g_sel | hit
        grp = jnp.where(hit, -jnp.inf, grp)
    cur = jnp.where(jnp.broadcast_to(g_sel, sel.shape), sel, NEG_INF)
    e_iota = lax.broadcasted_iota(jnp.int32, sel.shape, 0) * eg + j_iota
    ids, ws, hits = [], [], []
    for _ in range(TOP_K):
        m = cur.max(axis=1, keepdims=True).max(axis=0, keepdims=True)
        ei = jnp.min(jnp.where(cur == m, e_iota, n_exp), axis=1, keepdims=True).min(axis=0, keepdims=True)
        hit = e_iota == ei
        ids.append(ei)
        ws.append(jnp.sum(jnp.where(hit, scores, 0.0), axis=1, keepdims=True).sum(axis=0, keepdims=True))
        hits.append(hit)
        cur = jnp.where(hit, -jnp.inf, cur)
    total = functools.reduce(lambda a, b: a + b, ws)
    ws = [w / total * ROUTED_SCALE for w in ws]
    return ids, ws, hits


def _split_bf16(x):
    hi = x.astype(_bf16)
    return hi, (x - hi.astype(_f32)).astype(_bf16)


def _post_mixer_kernel(*refs, n_a, n_x, n_first):
    a_refs, x_refs = refs[:n_a], refs[n_a:n_a + n_x]
    wo_ref, gpost_ref, mod_ref, gpre_ref, wr_ref, x1_ref, hf_ref, logit_ref = refs[n_a + n_x:]
    o = _dot(_split_read(a_refs, n_first), wo_ref[...])
    x1 = _split_read(x_refs, n_first) + mod_ref[pl.ds(MOD_GATE_MIX, 1), :] * _rms(o, gpost_ref[...])
    x1_ref[...] = x1
    hf = _rms(x1, gpre_ref[...]) * (1.0 + mod_ref[pl.ds(MOD_SCALE_FFN, 1), :]) + mod_ref[pl.ds(MOD_SHIFT_FFN, 1), :]
    hf_ref[...] = hf
    h_hi, h_lo = _split_bf16(hf)
    w_hi, w_lo = _split_bf16(wr_ref[...])
    logit_ref[...] = _dot(h_hi, w_hi) + (_dot(h_hi, w_lo) + _dot(h_lo, w_hi))


def _post_mixer(a_parts, w_out, x_parts, g_post, mod, g_pre, w_router, n_ctx, dec_seq):
    t = sum(x.shape[0] for x in x_parts)
    d = x_parts[0].shape[1]
    tm = ROW_TILE
    e = w_router.shape[1]
    assert e <= LANES
    row = lambda i: (i, 0)
    fixed = lambda i: (0, 0)
    return pl.pallas_call(
        functools.partial(_post_mixer_kernel, n_a=len(a_parts), n_x=len(x_parts), n_first=n_ctx // tm),
        grid=(t // tm,),
        in_specs=_split_specs(a_parts, tm, d) + _split_specs(x_parts, tm, d) + [
            pl.BlockSpec((d, d), fixed),
            pl.BlockSpec((1, d), fixed),
            pl.BlockSpec((None, MOD_ROWS, d), lambda i: (_cond_row(i * tm, n_ctx, dec_seq), 0, 0)),
            pl.BlockSpec((1, d), fixed),
            pl.BlockSpec((d, LANES), fixed),
        ],
        out_specs=[pl.BlockSpec((tm, d), row), pl.BlockSpec((tm, d), row), pl.BlockSpec((tm, LANES), row)],
        out_shape=[jax.ShapeDtypeStruct((t, d), _f32), jax.ShapeDtypeStruct((t, d), _f32),
                   jax.ShapeDtypeStruct((t, LANES), _f32)],
        compiler_params=_params(("parallel",)),
        name="post_mixer",
    )(*a_parts, *x_parts, w_out.astype(_bf16), g_post.reshape(1, d), mod, g_pre.reshape(1, d),
      jnp.pad(w_router, ((0, 0), (0, LANES - e))))


def _router_kernel(logit_ref, br_ref, idx_ref, wt_ref, rank_ref, cnt_ref, carry_ref, before_ref):
    i = pl.program_id(0)
    tm = logit_ref.shape[0]
    n_exp = br_ref.shape[0]

    @pl.when(i == 0)
    def _():
        carry_ref[...] = jnp.zeros_like(carry_ref)
        t_src = lax.broadcasted_iota(jnp.int32, (tm, tm), 0)
        t_dst = lax.broadcasted_iota(jnp.int32, (tm, tm), 1)
        before_ref[...] = jnp.where(t_src < t_dst, 1.0, 0.0).astype(_bf16)

    logits = logit_ref[...].T[:n_exp]
    grouped = (N_GROUPS, n_exp // N_GROUPS, tm)
    scores = jax.nn.sigmoid(logits)
    ids, ws, hits = _route((scores + br_ref[...]).reshape(grouped), scores.reshape(grouped))
    for k in range(TOP_K):
        idx_ref[pl.ds(k, 1), :] = ids[k][0]
        wt_ref[pl.ds(k, 1), :] = ws[k][0]

    any_hit = functools.reduce(lambda a, b: a | b, hits)
    mask = jnp.where(any_hit, 1.0, 0.0).reshape(n_exp, tm).astype(_bf16)
    rank = _dot(mask, before_ref[...]) + jnp.concatenate([carry_ref[...]] * (tm // LANES), axis=1)
    rank = rank.reshape(grouped)
    for k in range(TOP_K):
        rk = jnp.sum(jnp.where(hits[k], rank, 0.0), axis=1, keepdims=True).sum(axis=0, keepdims=True)
        rank_ref[pl.ds(k, 1), :] = rk[0].astype(jnp.int32)
    carry_ref[...] += _dot(mask, jnp.ones((tm, LANES), _bf16))
    cnt_ref[...] = carry_ref[...].astype(jnp.int32)


def _router(logits, b_router):
    t = logits.shape[0]
    e = b_router.shape[0]
    tm = min(ROUTER_TILE, t)
    col = lambda i: (0, i)
    fixed = lambda i: (0, 0)
    slots_i = jax.ShapeDtypeStruct((TOP_K, t), jnp.int32)
    return pl.pallas_call(
        _router_kernel,
        grid=(t // tm,),
        in_specs=[pl.BlockSpec((tm, LANES), lambda i: (i, 0)), pl.BlockSpec((e, 1), fixed)],
        out_specs=[pl.BlockSpec((TOP_K, tm), col), pl.BlockSpec((TOP_K, tm), col), pl.BlockSpec((TOP_K, tm), col),
                   pl.BlockSpec((e, LANES), fixed)],
        out_shape=[slots_i, jax.ShapeDtypeStruct((TOP_K, t), _f32), slots_i,
                   jax.ShapeDtypeStruct((e, LANES), jnp.int32)],
        scratch_shapes=[pltpu.VMEM((e, LANES), _f32), pltpu.VMEM((tm, tm), _bf16)],
        compiler_params=_params(("arbitrary",)),
        name="router",
    )(logits, b_router.reshape(e, 1))


def _row(ref, r):
    return ref.at[pl.ds(r, 1), :]


def _plan_kernel(cnt_ref, idx_ref, rank_ref, pos_ref, te_ref, tbi_ref, tbo_ref, tv_ref, padrow_ref, padn_ref,
                 offs_ref, *, tm):
    n_exp = cnt_ref.shape[0]
    n_tiles = te_ref.shape[0]

    def per_expert(e, carry):
        off, tile = carry
        cnt = cnt_ref[e]
        n_t = (cnt + tm - 1) // tm
        offs_ref[e] = off
        padrow_ref[e] = off + cnt
        padn_ref[e] = n_t * tm - cnt

        def per_tile(j, c):
            te_ref[tile + j] = e
            tbi_ref[tile + j] = tile + j
            tbo_ref[tile + j] = tile + j
            tv_ref[tile + j] = jnp.minimum(cnt - j * tm, tm)
            return c

        lax.fori_loop(0, n_t, per_tile, 0)
        return off + n_t * tm, tile + n_t

    _, live = lax.fori_loop(0, n_exp, per_expert, (jnp.int32(0), jnp.int32(0)))
    last = jnp.maximum(live - 1, 0)
    last_expert = te_ref[last]

    def dead_tile(i, c):
        te_ref[i] = last_expert
        tbi_ref[i] = last
        tbo_ref[i] = i
        tv_ref[i] = 0
        return c

    lax.fori_loop(live, n_tiles, dead_tile, 0)

    idx = idx_ref[...]
    pos = rank_ref[...]
    for e in range(n_exp):
        pos = pos + jnp.where(idx == e, offs_ref[e], 0)
    pos_ref[...] = pos


def _plan(counts, idx_t, rank_t, n_tiles):
    n_exp = counts.shape[0]
    smem = pl.BlockSpec(memory_space=pltpu.SMEM)
    vmem = pl.BlockSpec(memory_space=pltpu.VMEM)
    tiles = jax.ShapeDtypeStruct((n_tiles,), jnp.int32)
    experts = jax.ShapeDtypeStruct((n_exp,), jnp.int32)
    return pl.pallas_call(
        functools.partial(_plan_kernel, tm=EXPERT_TILE),
        in_specs=[smem, vmem, vmem],
        out_specs=[vmem, smem, smem, smem, smem, smem, smem],
        out_shape=[jax.ShapeDtypeStruct(idx_t.shape, jnp.int32), tiles, tiles, tiles, tiles, experts, experts],
        scratch_shapes=[pltpu.SMEM((n_exp,), jnp.int32)],
        name="plan",
    )(counts, idx_t, rank_t)


def _pad_copies(start, n, zero_ref, dst_ref, sem, pad_bits):
    single = n & (SUBLANES - 1)
    for s in range(SUBLANES - 1):
        yield s < single, pltpu.make_async_copy(_row(zero_ref, 0), _row(dst_ref, start + s), sem)
    base = pl.multiple_of(start + single, SUBLANES)
    groups = n // SUBLANES
    for b in range(pad_bits - (SUBLANES.bit_length() - 1)):
        rows = SUBLANES << b
        first = pl.multiple_of(base + ((groups >> (b + 1)) << (b + 1)) * SUBLANES, SUBLANES)
        copy = pltpu.make_async_copy(zero_ref.at[pl.ds(0, rows), :], dst_ref.at[pl.ds(first, rows), :], sem)
        yield ((groups >> b) & 1) == 1, copy


def _slot_specs(n_tokens, tile, step_of):
    per_k = n_tokens // tile
    return [pl.BlockSpec((tile,), lambda i, k=k: (k * per_k + step_of(i),), memory_space=pltpu.SMEM)
            for k in range(TOP_K)]


def _dispatch_kernel(*refs, pad_bits):
    pos_refs = refs[:TOP_K]
    padrow_ref, padn_ref, src_ref, dst_ref, zero_ref, sem, zero_sem = refs[TOP_K:]
    i = pl.program_id(0)
    tile = src_ref.shape[0]
    n_exp = padn_ref.shape[0]

    def for_each_pad_copy(fn):
        def body(e, c):
            for needed, copy in _pad_copies(padrow_ref[e], padn_ref[e], zero_ref, dst_ref, zero_sem, pad_bits):
                pl.when(needed)(functools.partial(fn, copy))
            return c
        lax.fori_loop(0, n_exp, body, 0)

    @pl.when(i == 0)
    def _():
        zero_ref[...] = jnp.zeros_like(zero_ref)
        for_each_pad_copy(lambda copy: copy.start())

    def issue(t, carry):
        for k in range(TOP_K):
            pltpu.make_async_copy(_row(src_ref, t), _row(dst_ref, pos_refs[k][t]), sem).start()
        return carry

    lax.fori_loop(0, tile, issue, 0)
    for k in range(TOP_K):
        pltpu.make_async_copy(src_ref, dst_ref.at[pl.ds(0, tile), :], sem).wait()

    @pl.when(i == 0)
    def _():
        for_each_pad_copy(lambda copy: copy.wait())


def _dispatch(hf, pos, padrow, padn, n_sorted):
    t, d = hf.shape
    tile = min(SCATTER_TILE, t)
    pad_bits = (EXPERT_TILE - 1).bit_length()
    smem = pl.BlockSpec(memory_space=pltpu.SMEM)
    return pl.pallas_call(
        functools.partial(_dispatch_kernel, pad_bits=pad_bits),
        grid=(t // tile,),
        in_specs=_slot_specs(t, tile, lambda i: i) + [
            smem, smem,
            pl.BlockSpec((tile, d), lambda i: (i, 0)),
        ],
        out_specs=pl.BlockSpec(memory_space=pl.ANY),
        out_shape=jax.ShapeDtypeStruct((n_sorted, d), _f32),
        scratch_shapes=[pltpu.VMEM((1 << (pad_bits - 1), d), _f32),
                        pltpu.SemaphoreType.DMA(()), pltpu.SemaphoreType.DMA(())],
        compiler_params=_params(("arbitrary",)),
        name="dispatch",
    )(*([pos] * TOP_K), padrow, padn, hf)


def _expert_kernel(te_ref, tbi_ref, tbo_ref, tv_ref, x_ref, wg_ref, wu_ref, wd_ref, o_ref,
                   wgb_ref, wub_ref, wdb_ref):
    i = pl.program_id(0)
    live = tv_ref[i] > 0
    new_expert = (i == 0) | (te_ref[i] != te_ref[jnp.maximum(i - 1, 0)])

    @pl.when(live & new_expert)
    def _():
        wgb_ref[...] = wg_ref[...].astype(_bf16)
        wub_ref[...] = wu_ref[...].astype(_bf16)
        wdb_ref[...] = wd_ref[...].astype(_bf16)

    @pl.when(live)
    def _():
        x = x_ref[...].astype(_bf16)
        g = _dot(x, wgb_ref[...])
        u = _dot(x, wub_ref[...])
        h = (g * jax.nn.sigmoid(g) * u).astype(_bf16)
        o_ref[...] = _dot(h, wdb_ref[...])

    @pl.when(jnp.logical_not(live))
    def _():
        o_ref[...] = jnp.zeros_like(o_ref)


def _experts(xs, w_gate, w_up, w_down, layer, tile_expert, tile_in, tile_out, tile_valid):
    _, n_exp, d, f = w_gate.shape
    tm = EXPERT_TILE
    n_tiles = tile_expert.shape[0]
    wsel = lambda i, te, tbi, tbo, tv: (layer, te[i], 0, 0)
    return pl.pallas_call(
        _expert_kernel,
        grid_spec=pltpu.PrefetchScalarGridSpec(
            num_scalar_prefetch=4,
            grid=(n_tiles,),
            in_specs=[
                pl.BlockSpec((tm, d), lambda i, te, tbi, tbo, tv: (tbi[i], 0)),
                pl.BlockSpec((None, None, d, f), wsel),
                pl.BlockSpec((None, None, d, f), wsel),
                pl.BlockSpec((None, None, f, d), wsel),
            ],
            out_specs=pl.BlockSpec((tm, d), lambda i, te, tbi, tbo, tv: (tbo[i], 0)),
            scratch_shapes=[pltpu.VMEM((d, f), _bf16), pltpu.VMEM((d, f), _bf16), pltpu.VMEM((f, d), _bf16)],
        ),
        out_shape=jax.ShapeDtypeStruct(xs.shape, _f32),
        compiler_params=_params(("arbitrary",)),
        name="experts",
    )(tile_expert, tile_in, tile_out, tile_valid, xs, w_gate, w_up, w_down)


def _combine_kernel(*refs, n_first):
    pos_refs, posn_refs = refs[:TOP_K], refs[TOP_K:2 * TOP_K]
    ys_ref, wt_ref, sh_ref, x1_ref, gpost_ref, mod_ref = refs[2 * TOP_K:2 * TOP_K + 6]
    rest = refs[2 * TOP_K + 6:]
    final = n_first is not None
    if final:
        yctx_ref, ylat_ref, buf_ref, f_ref, sems = rest
    else:
        gnext_ref, modn_ref, x2_ref, hn_ref, buf_ref, f_ref, sems = rest
    tm, d = x1_ref.shape
    i = pl.program_id(0)
    slot = i % 2

    def gather(p_refs, s):
        def issue(t, carry):
            for k in range(TOP_K):
                pltpu.make_async_copy(_row(ys_ref, p_refs[k][t]),
                                      buf_ref.at[s, k, pl.ds(t, 1), :], sems.at[s]).start()
            return carry
        lax.fori_loop(0, tm, issue, 0)

    @pl.when(i == 0)
    def _():
        gather(pos_refs, 0)

    @pl.when(i + 1 < pl.num_programs(0))
    def _():
        gather(posn_refs, 1 - slot)

    for k in range(TOP_K):
        pltpu.make_async_copy(ys_ref.at[pl.ds(0, tm), :], buf_ref.at[slot, k], sems.at[slot]).wait()

    def sum_rows(r, carry):
        rows = pl.ds(pl.multiple_of(r * COMBINE_SUB, COMBINE_SUB), COMBINE_SUB)
        w = wt_ref[rows, :]
        wb = [jnp.broadcast_to(w[:, k:k + 1], (COMBINE_SUB, LANES)) for k in range(TOP_K)]
        for j in range(d // LANES):
            cols = slice(j * LANES, (j + 1) * LANES)
            acc = sh_ref[rows, cols]
            for k in range(TOP_K):
                acc = acc + wb[k] * buf_ref[slot, k, rows, cols]
            f_ref[rows, cols] = acc
        return carry

    lax.fori_loop(0, tm // COMBINE_SUB, sum_rows, 0)
    f = f_ref[...]
    x2 = x1_ref[...] + mod_ref[pl.ds(MOD_GATE_FFN, 1), :] * _rms(f, gpost_ref[...])
    if final:
        in_ctx = pl.program_id(0) < n_first

        @pl.when(in_ctx)
        def _():
            yctx_ref[...] = x2

        @pl.when(jnp.logical_not(in_ctx))
        def _():
            ylat_ref[...] = x2
    else:
        x2_ref[...] = x2
        hn = (_rms(x2, gnext_ref[...]) * (1.0 + modn_ref[pl.ds(MOD_SCALE_MIX, 1), :])
              + modn_ref[pl.ds(MOD_SHIFT_MIX, 1), :])
        hn_ref[...] = hn.astype(hn_ref.dtype)


def _combine(ys, pos, wt, shared, x1, g_post, mod, n_ctx, dec_seq, g_next=None, mod_next=None):
    t, d = x1.shape
    tm = min(GATHER_TILE, t)
    n_steps = t // tm
    final = g_next is None
    row = lambda i: (i, 0)
    fixed = lambda i: (0, 0)
    cond = lambda i: (_cond_row(i * tm, n_ctx, dec_seq), 0, 0)
    in_specs = _slot_specs(t, tm, lambda i: i) + _slot_specs(t, tm, lambda i: jnp.minimum(i + 1, n_steps - 1)) + [
        pl.BlockSpec(memory_space=pl.ANY),
        pl.BlockSpec((tm, TOP_K), row),
        pl.BlockSpec((tm, d), row),
        pl.BlockSpec((tm, d), row),
        pl.BlockSpec((1, d), fixed),
        pl.BlockSpec((None, MOD_ROWS, d), cond),
    ]
    args = [pos] * (2 * TOP_K) + [ys, wt, shared, x1, g_post.reshape(1, d), mod]
    if final:
        n0, n1 = n_ctx // tm, (t - n_ctx) // tm
        out_specs = [pl.BlockSpec((tm, d), lambda i: (jnp.minimum(i, n0 - 1), 0)),
                     pl.BlockSpec((tm, d), lambda i: (jnp.clip(i - n0, 0, n1 - 1), 0))]
        out_shape = [jax.ShapeDtypeStruct((n_ctx, d), _f32), jax.ShapeDtypeStruct((t - n_ctx, d), _f32)]
    else:
        in_specs += [pl.BlockSpec((1, d), fixed), pl.BlockSpec((None, MOD_ROWS, d), cond)]
        args += [g_next.reshape(1, d), mod_next]
        out_specs = [pl.BlockSpec((tm, d), row), pl.BlockSpec((tm, d), row)]
        out_shape = [jax.ShapeDtypeStruct((t, d), _f32), jax.ShapeDtypeStruct((t, d), _bf16)]
    return pl.pallas_call(
        functools.partial(_combine_kernel, n_first=n_ctx // tm if final else None),
        grid=(n_steps,),
        in_specs=in_specs,
        out_specs=out_specs,
        out_shape=out_shape,
        scratch_shapes=[pltpu.VMEM((2, TOP_K, tm, d), _f32), pltpu.VMEM((tm, d), _f32),
                        pltpu.SemaphoreType.DMA((2,))],
        compiler_params=_params(("arbitrary",)),
        name="combine",
    )(*args)


def _moe(hf, idx_t, wt_t, rank_t, cnt, layer, w_gate, w_up, w_down, w_sh_gate, w_sh_up, w_sh_down):
    n_exp, d = w_gate.shape[1:3]
    t = hf.shape[0]
    tm = EXPERT_TILE
    n_tiles = t * TOP_K // tm + n_exp
    pos, tile_expert, tile_in, tile_out, tile_valid, padrow, padn = _plan(cnt[:, 0], idx_t, rank_t, n_tiles)
    pos = pos.reshape(-1)
    xs = _dispatch(hf, pos, padrow, padn, n_tiles * tm)
    ys = _experts(xs, w_gate, w_up, w_down, layer, tile_expert, tile_in, tile_out, tile_valid)
    n_sh = t // tm
    every = jnp.arange(n_sh, dtype=jnp.int32)
    shared = _experts(hf, w_sh_gate[:, None], w_sh_up[:, None], w_sh_down[:, None], layer,
                      jnp.zeros((n_sh,), jnp.int32), every, every, jnp.full((n_sh,), tm, jnp.int32))
    return ys, pos, wt_t.T, shared


def kernel(x_prompt, x_sample, cache_k, cache_v, c, c_ctx, w_ada, b_ada, norm_mix_pre, norm_mix_post, norm_ffn_pre, norm_ffn_post, w_qkv, w_o_attn, rpb, w_conv_in, conv_w, w_conv_out, w_router, b_router, w_exp_gate, w_exp_up, w_exp_down, w_sh_gate, w_sh_up, w_sh_down):
    batch, seq, d = x_prompt.shape
    dec_batch, dec_seq, _ = x_sample.shape
    depth = w_ada.shape[0]
    n_ctx, n_lat = batch * seq, dec_batch * dec_seq
    dh = d // N_HEADS
    past = cache_k.shape[2]

    x = [x_prompt.reshape(n_ctx, d), x_sample.reshape(n_lat, d)]
    cond = jnp.concatenate([c_ctx[None, :], c], axis=0)
    mod = _adaln(cond, w_ada, b_ada)

    new_k, new_v = [], []
    h = _modulate(x, norm_mix_pre[0], mod[0], n_ctx, dec_seq)
    for l in range(depth):
        if l % 2 == 0:
            a = l // 2
            qkv = _matmul(h, w_qkv[a])
            o_ctx, k_new, v_new = _ctx_attention(qkv, batch, seq, d)
            new_k.append(k_new.reshape(batch, seq, N_HEADS, dh))
            new_v.append(v_new.reshape(batch, seq, N_HEADS, dh))
            o_lat = _natten(qkv, n_ctx, cache_k[:, a].reshape(dec_batch, past, d),
                            cache_v[:, a].reshape(dec_batch, past, d), rpb[a], dec_batch, dec_seq, d)
            mixed = [o_ctx, o_lat]
            w_out = w_o_attn[a]
        else:
            m = l // 2
            mixed = [_conv_gate(_matmul(h, w_conv_in[m]), conv_w[m], n_ctx, seq, dec_seq, d)]
            w_out = w_conv_out[m]
        x1, hf, logits = _post_mixer(
            mixed, w_out, x, norm_mix_post[l], mod[l], norm_ffn_pre[l], w_router[l], n_ctx, dec_seq)
        idx_t, wt_t, rank_t, cnt = _router(logits, b_router[l])
        ys, pos, wt, shared = _moe(hf, idx_t, wt_t, rank_t, cnt, l, w_exp_gate, w_exp_up, w_exp_down,
                                   w_sh_gate, w_sh_up, w_sh_down)
        if l + 1 < depth:
            x2, h = _combine(ys, pos, wt, shared, x1, norm_ffn_post[l], mod[l], n_ctx, dec_seq,
                             norm_mix_pre[l + 1], mod[l + 1])
            x = [x2]
        else:
            y_ctx, y_lat = _combine(ys, pos, wt, shared, x1, norm_ffn_post[l], mod[l], n_ctx, dec_seq)

    return (y_ctx.reshape(batch, seq, d), y_lat.reshape(dec_batch, dec_seq, d),
            jnp.stack(new_k, axis=1), jnp.stack(new_v, axis=1))
```

```python
import functools

import jax
import jax.numpy as jnp
from jax import lax
from jax.experimental import pallas as pl
from jax.experimental.pallas import tpu as pltpu

N_HEADS = 16
GRID_W = 64
WIN_ROWS = 8
WIN_COLS = 16
CONV_WIDTH = 3
N_EXPERTS = 64
TOP_K = 8
N_GROUPS = 8
TOPK_GROUPS = 4
ROUTED_SCALE = 2.5
N_MOD = 6
RMS_EPS = 1e-6
NEG_INF = -1e30

LANES = 128
SUBLANES = 8
VMEM_LIMIT = 56 * 1024 * 1024

MOD_SHIFT_MIX, MOD_SCALE_MIX, MOD_GATE_MIX, MOD_SHIFT_FFN, MOD_SCALE_FFN, MOD_GATE_FFN = range(6)
MOD_ROWS = 8

ROW_TILE = 256
ROUTER_TILE = 1024
EXPERT_TILE = 256
MM_TILE_M = 2048
MM_TILE_N = 512
GATHER_TILE = 128
SCATTER_TILE = 512

_f32 = jnp.float32
_bf16 = jnp.bfloat16


def _params(sem, vmem=VMEM_LIMIT):
    return pltpu.CompilerParams(dimension_semantics=sem, vmem_limit_bytes=vmem)


def _rms(x, g):
    return x * lax.rsqrt(jnp.mean(x * x, axis=-1, keepdims=True) + RMS_EPS) * g


def _dot(a, b):
    return jnp.dot(a, b, preferred_element_type=_f32)


def _dot_nt(a, b):
    return lax.dot_general(a, b, (((1,), (1,)), ((), ())), preferred_element_type=_f32)


def _cond_row(row0, n_ctx, dec_seq):
    return jnp.where(row0 < n_ctx, 0, 1 + (row0 - n_ctx) // dec_seq)


def _adaln_kernel(cb_ref, w_ref, b_ref, o_ref, acc_ref, *, n_cond):
    k = pl.program_id(2)

    @pl.when(k == 0)
    def _():
        acc_ref[...] = jnp.zeros_like(acc_ref)

    tk, tn = w_ref.shape
    for r in range(n_cond):
        s = cb_ref[r]
        s = s * jax.nn.sigmoid(s)
        for c in range(tn // LANES):
            p = w_ref[:, c * LANES:(c + 1) * LANES] * s
            acc_ref[r, :, c * LANES:(c + 1) * LANES] += p.reshape(tk // SUBLANES, SUBLANES, LANES).sum(axis=0)

    @pl.when(k == pl.num_programs(2) - 1)
    def _():
        o_ref[...] = jnp.zeros_like(o_ref)
        for r in range(n_cond):
            o_ref[pl.ds(r, 1), :] = acc_ref[r].sum(axis=0, keepdims=True) + b_ref[...]


def _adaln(cond, w_ada, b_ada):
    n_cond, d = cond.shape
    n_layers, _, n6 = w_ada.shape
    tk, tn = min(512, d), min(2048, n6)
    cb = jnp.broadcast_to(cond[:, :, None], (n_cond, d, LANES))
    mod = pl.pallas_call(
        functools.partial(_adaln_kernel, n_cond=n_cond),
        grid=(n_layers, n6 // tn, d // tk),
        in_specs=[
            pl.BlockSpec((n_cond, tk, LANES), lambda l, n, k: (0, k, 0)),
            pl.BlockSpec((None, tk, tn), lambda l, n, k: (l, k, n)),
            pl.BlockSpec((None, 1, tn), lambda l, n, k: (l, 0, n)),
        ],
        out_specs=pl.BlockSpec((None, MOD_ROWS, tn), lambda l, n, k: (l, 0, n)),
        out_shape=jax.ShapeDtypeStruct((n_layers, MOD_ROWS, n6), _f32),
        scratch_shapes=[pltpu.VMEM((n_cond, SUBLANES, tn), _f32)],
        compiler_params=_params(("arbitrary", "arbitrary", "arbitrary")),
        name="adaln",
    )(cb, w_ada, b_ada.reshape(n_layers, 1, n6))
    mod = mod[:, :n_cond].reshape(n_layers, n_cond, N_MOD, d)
    return jnp.pad(mod, ((0, 0), (0, 0), (0, MOD_ROWS - N_MOD), (0, 0)))


def _split_specs(parts, tm, width):
    if len(parts) == 1:
        return [pl.BlockSpec((tm, width), lambda i: (i, 0))]
    n0 = parts[0].shape[0] // tm
    n1 = parts[1].shape[0] // tm
    return [pl.BlockSpec((tm, width), lambda i: (jnp.minimum(i, n0 - 1), 0)),
            pl.BlockSpec((tm, width), lambda i: (jnp.clip(i - n0, 0, n1 - 1), 0))]


def _split_read(refs, n_first):
    if len(refs) == 1:
        return refs[0][...]
    return jnp.where(pl.program_id(0) < n_first, refs[0][...], refs[1][...])


def _modulate_kernel(*refs, n_x, n_first):
    x_refs, (g_ref, mod_ref, o_ref) = refs[:n_x], refs[n_x:]
    y = _rms(_split_read(x_refs, n_first), g_ref[...])
    h = y * (1.0 + mod_ref[pl.ds(MOD_SCALE_MIX, 1), :]) + mod_ref[pl.ds(MOD_SHIFT_MIX, 1), :]
    o_ref[...] = h.astype(o_ref.dtype)


def _modulate(xs, g, mod, n_ctx, dec_seq):
    t = sum(x.shape[0] for x in xs)
    d = xs[0].shape[1]
    tm = ROW_TILE
    return pl.pallas_call(
        functools.partial(_modulate_kernel, n_x=len(xs), n_first=xs[0].shape[0] // tm),
        grid=(t // tm,),
        in_specs=_split_specs(xs, tm, d) + [
            pl.BlockSpec((1, d), lambda i: (0, 0)),
            pl.BlockSpec((None, MOD_ROWS, d), lambda i: (_cond_row(i * tm, n_ctx, dec_seq), 0, 0)),
        ],
        out_specs=pl.BlockSpec((tm, d), lambda i: (i, 0)),
        out_shape=jax.ShapeDtypeStruct((t, d), _bf16),
        compiler_params=_params(("parallel",)),
        name="modulate",
    )(*xs, g.reshape(1, d), mod)


def _matmul_kernel(a_ref, w_ref, o_ref):
    o_ref[...] = _dot(a_ref[...], w_ref[...].astype(_bf16)).astype(o_ref.dtype)


def _matmul(a, w):
    n_rows = a.shape[0]
    k, n = w.shape
    tm, tn = min(MM_TILE_M, n_rows), min(MM_TILE_N, n)
    return pl.pallas_call(
        _matmul_kernel,
        grid=(n_rows // tm, n // tn),
        in_specs=[
            pl.BlockSpec((tm, k), lambda i, j: (i, 0)),
            pl.BlockSpec((k, tn), lambda i, j: (0, j)),
        ],
        out_specs=pl.BlockSpec((tm, tn), lambda i, j: (i, j)),
        out_shape=jax.ShapeDtypeStruct((n_rows, n), _f32),
        compiler_params=_params(("parallel", "parallel")),
        name="matmul",
    )(a, w)


def _ctx_attn_kernel(q_ref, k_ref, v_ref, o_ref, kout_ref, vout_ref, *, n_heads, scale):
    seq = q_ref.shape[0]
    dh = q_ref.shape[1] // n_heads
    for h in range(n_heads):
        sl = slice(h * dh, (h + 1) * dh)
        k32 = k_ref[:, sl]
        v32 = v_ref[:, sl]
        kout_ref[pl.ds(h, seq, stride=n_heads), :] = k32
        vout_ref[pl.ds(h, seq, stride=n_heads), :] = v32
        q = q_ref[:, sl].astype(_bf16)
        s = _dot_nt(q, k32.astype(_bf16)) * scale
        p = jnp.exp(s - s.max(axis=-1, keepdims=True))
        o = _dot(p.astype(_bf16), v32.astype(_bf16)) / p.sum(axis=-1, keepdims=True)
        o_ref[:, sl] = o.astype(o_ref.dtype)


def _ctx_attention(qkv, batch, seq, d):
    dh = d // N_HEADS
    assert dh == LANES
    cache = jax.ShapeDtypeStruct((batch * seq * N_HEADS, dh), _f32)
    return pl.pallas_call(
        functools.partial(_ctx_attn_kernel, n_heads=N_HEADS, scale=dh ** -0.5),
        grid=(batch,),
        in_specs=[
            pl.BlockSpec((seq, d), lambda b: (b, 0)),
            pl.BlockSpec((seq, d), lambda b: (b, 1)),
            pl.BlockSpec((seq, d), lambda b: (b, 2)),
        ],
        out_specs=[
            pl.BlockSpec((seq, d), lambda b: (b, 0)),
            pl.BlockSpec((seq * N_HEADS, dh), lambda b: (b, 0)),
            pl.BlockSpec((seq * N_HEADS, dh), lambda b: (b, 0)),
        ],
        out_shape=[jax.ShapeDtypeStruct((batch * seq, d), _bf16), cache, cache],
        compiler_params=_params(("parallel",)),
        name="ctx_attention",
    )(qkv, qkv, qkv)


def _natten_kernel(q_ref, k_ref, v_ref, kc_ref, vc_ref, bias_ref, o_ref, sctx_ref, pctx_ref, oloc_ref, den_ref,
                   *, rows, width, kh, scale, unroll):
    n_loc = kh * width
    q_col = lax.broadcasted_iota(jnp.int32, (width, n_loc), 0)
    k_col = lax.broadcasted_iota(jnp.int32, (width, n_loc), 1) % width
    col_start = jnp.clip(q_col - WIN_COLS // 2, 0, width - WIN_COLS)
    col_mask = (k_col >= col_start) & (k_col < col_start + WIN_COLS)

    sctx_ref[...] = _dot_nt(q_ref[...].astype(_bf16), kc_ref[...].astype(_bf16)) * scale

    def one_row(r):
        r0 = jnp.clip(r - kh // 2, 0, rows - kh)
        qrows = pl.ds(pl.multiple_of(r * width, width), width)
        win = pl.ds(pl.multiple_of(r0 * width, width), n_loc)
        q = q_ref[qrows, :].astype(_bf16)
        s_loc = _dot_nt(q, k_ref[win, :].astype(_bf16)) * scale + bias_ref[r - r0]
        s_loc = jnp.where(col_mask, s_loc, NEG_INF)
        s_ctx = sctx_ref[qrows, :]
        m = jnp.maximum(s_loc.max(axis=-1, keepdims=True), s_ctx.max(axis=-1, keepdims=True))
        p_loc = jnp.exp(s_loc - m)
        p_ctx = jnp.exp(s_ctx - m)
        den = p_loc.sum(axis=-1, keepdims=True) + p_ctx.sum(axis=-1, keepdims=True)
        pctx_ref[qrows, :] = p_ctx.astype(_bf16)
        oloc_ref[qrows, :] = _dot(p_loc.astype(_bf16), v_ref[win, :].astype(_bf16))
        den_ref[qrows, :] = jnp.broadcast_to(den, (width, den_ref.shape[1]))

    def some_rows(g, carry):
        for u in range(unroll):
            one_row(g * unroll + u)
        return carry

    lax.fori_loop(0, rows // unroll, some_rows, 0)
    o = oloc_ref[...] + _dot(pctx_ref[...], vc_ref[...].astype(_bf16))
    o_ref[...] = (o / den_ref[...]).astype(o_ref.dtype)


def _natten_bias(rpb, kh):
    col = jnp.arange(GRID_W)
    dc = jnp.clip(col[None, :] - col[:, None] + WIN_COLS - 1, 0, 2 * WIN_COLS - 2)
    dr = jnp.arange(kh)[None, :] - jnp.arange(kh)[:, None] + (WIN_ROWS - 1)
    t = rpb[:, dr][..., dc]
    return t.transpose(0, 1, 3, 2, 4).reshape(rpb.shape[0], kh, GRID_W, kh * GRID_W).astype(_f32)


def _natten(qkv, n_ctx, k_ctx, v_ctx, rpb, dec_batch, dec_seq, d):
    assert n_ctx % dec_seq == 0
    dh = d // N_HEADS
    rows = dec_seq // GRID_W
    kh = min(WIN_ROWS, rows)
    past = k_ctx.shape[1]
    bias = _natten_bias(rpb, kh)
    b0 = n_ctx // dec_seq
    unroll = next(u for u in (8, 4, 2, 1) if rows % u == 0)
    return pl.pallas_call(
        functools.partial(_natten_kernel, rows=rows, width=GRID_W, kh=kh, scale=dh ** -0.5, unroll=unroll),
        grid=(dec_batch, N_HEADS),
        in_specs=[
            pl.BlockSpec((dec_seq, dh), lambda b, h: (b0 + b, h)),
            pl.BlockSpec((dec_seq, dh), lambda b, h: (b0 + b, N_HEADS + h)),
            pl.BlockSpec((dec_seq, dh), lambda b, h: (b0 + b, 2 * N_HEADS + h)),
            pl.BlockSpec((None, past, dh), lambda b, h: (b, 0, h)),
            pl.BlockSpec((None, past, dh), lambda b, h: (b, 0, h)),
            pl.BlockSpec((None, kh, GRID_W, kh * GRID_W), lambda b, h: (h, 0, 0, 0)),
        ],
        out_specs=pl.BlockSpec((dec_seq, dh), lambda b, h: (b, h)),
        out_shape=jax.ShapeDtypeStruct((dec_batch * dec_seq, d), _bf16),
        scratch_shapes=[pltpu.VMEM((dec_seq, past), _f32), pltpu.VMEM((dec_seq, past), _bf16),
                        pltpu.VMEM((dec_seq, dh), _f32), pltpu.VMEM((dec_seq, dh), _f32)],
        compiler_params=_params(("parallel", "parallel")),
        name="natten",
    )(qkv, qkv, qkv, k_ctx, v_ctx, bias)


def _conv_gate_kernel(b_ref, c_ref, u_ref, cp_ref, up_ref, cn_ref, un_ref, w_ref, o_ref,
                      *, n_ctx, seq, dec_seq):
    tm = o_ref.shape[0]
    row0 = pl.program_id(0) * tm
    in_ctx = row0 < n_ctx
    pos0 = jnp.where(in_ctx, row0 % seq, (row0 - n_ctx) % dec_seq)
    seq_len = jnp.where(in_ctx, seq, dec_seq)
    has_prev = pos0 > 0
    has_next = pos0 + tm < seq_len
    cu = c_ref[...] * u_ref[...]
    prev_row = jnp.where(has_prev, cp_ref[pl.ds(SUBLANES - 1, 1), :] * up_ref[pl.ds(SUBLANES - 1, 1), :], 0.0)
    next_row = jnp.where(has_next, cn_ref[pl.ds(0, 1), :] * un_ref[pl.ds(0, 1), :], 0.0)
    ridx = lax.broadcasted_iota(jnp.int32, cu.shape, 0)
    before = jnp.where(ridx == 0, prev_row, pltpu.roll(cu, 1, axis=0))
    after = jnp.where(ridx == tm - 1, next_row, pltpu.roll(cu, tm - 1, axis=0))
    conv = w_ref[pl.ds(0, 1), :] * before + w_ref[pl.ds(1, 1), :] * cu + w_ref[pl.ds(2, 1), :] * after
    o_ref[...] = (b_ref[...] * conv).astype(o_ref.dtype)


def _conv_gate(bcu, conv_w, n_ctx, seq, dec_seq, d):
    t = bcu.shape[0]
    tm = ROW_TILE
    halo = tm // SUBLANES
    last = t // SUBLANES - 1
    cw = jnp.pad(conv_w, ((0, SUBLANES - CONV_WIDTH), (0, 0)))
    prev_map = lambda col: (lambda i: (jnp.maximum(i * halo - 1, 0), col))
    next_map = lambda col: (lambda i: (jnp.minimum((i + 1) * halo, last), col))
    return pl.pallas_call(
        functools.partial(_conv_gate_kernel, n_ctx=n_ctx, seq=seq, dec_seq=dec_seq),
        grid=(t // tm,),
        in_specs=[
            pl.BlockSpec((tm, d), lambda i: (i, 0)),
            pl.BlockSpec((tm, d), lambda i: (i, 1)),
            pl.BlockSpec((tm, d), lambda i: (i, 2)),
            pl.BlockSpec((SUBLANES, d), prev_map(1)),
            pl.BlockSpec((SUBLANES, d), prev_map(2)),
            pl.BlockSpec((SUBLANES, d), next_map(1)),
            pl.BlockSpec((SUBLANES, d), next_map(2)),
            pl.BlockSpec((SUBLANES, d), lambda i: (0, 0)),
        ],
        out_specs=pl.BlockSpec((tm, d), lambda i: (i, 0)),
        out_shape=jax.ShapeDtypeStruct((t, d), _bf16),
        compiler_params=_params(("parallel",)),
        name="conv_gate",
    )(bcu, bcu, bcu, bcu, bcu, bcu, bcu, cw)


def _route(sel, scores):
    n_grp, eg, tm = sel.shape
    n_exp = n_grp * eg
    j_iota = lax.broadcasted_iota(jnp.int32, sel.shape, 1)
    m1 = sel.max(axis=1, keepdims=True)
    j1 = jnp.min(jnp.where(sel == m1, j_iota, eg), axis=1, keepdims=True)
    m2 = jnp.max(jnp.where(j_iota == j1, -jnp.inf, sel), axis=1, keepdims=True)
    grp = m1 + m2
    g_iota = lax.broadcasted_iota(jnp.int32, grp.shape, 0)
    g_sel = g_iota < 0
    for _ in range(TOPK_GROUPS):
        gm = grp.max(axis=0, keepdims=True)
        gi = jnp.min(jnp.where(grp == gm, g_iota, n_grp), axis=0, keepdims=True)
        hit = g_iota == gi
        g_sel = g_sel | hit
        grp = jnp.where(hit, -jnp.inf, grp)
    cur = jnp.where(jnp.broadcast_to(g_sel, sel.shape), sel, NEG_INF)
    e_iota = lax.broadcasted_iota(jnp.int32, sel.shape, 0) * eg + j_iota
    ids, ws, hits = [], [], []
    for _ in range(TOP_K):
        m = cur.max(axis=1, keepdims=True).max(axis=0, keepdims=True)
        ei = jnp.min(jnp.where(cur == m, e_iota, n_exp), axis=1, keepdims=True).min(axis=0, keepdims=True)
        hit = e_iota == ei
        ids.append(ei)
        ws.append(jnp.sum(jnp.where(hit, scores, 0.0), axis=1, keepdims=True).sum(axis=0, keepdims=True))
        hits.append(hit)
        cur = jnp.where(hit, -jnp.inf, cur)
    total = functools.reduce(lambda a, b: a + b, ws)
    ws = [w / total * ROUTED_SCALE for w in ws]
    return ids, ws, hits


def _split_bf16(x):
    hi = x.astype(_bf16)
    return hi, (x - hi.astype(_f32)).astype(_bf16)


def _post_mixer_kernel(*refs, n_a, n_x, n_first):
    a_refs, x_refs = refs[:n_a], refs[n_a:n_a + n_x]
    wo_ref, gpost_ref, mod_ref, gpre_ref, wr_ref, x1_ref, hf_ref, logit_ref = refs[n_a + n_x:]
    o = _dot(_split_read(a_refs, n_first), wo_ref[...])
    x1 = _split_read(x_refs, n_first) + mod_ref[pl.ds(MOD_GATE_MIX, 1), :] * _rms(o, gpost_ref[...])
    x1_ref[...] = x1
    hf = _rms(x1, gpre_ref[...]) * (1.0 + mod_ref[pl.ds(MOD_SCALE_FFN, 1), :]) + mod_ref[pl.ds(MOD_SHIFT_FFN, 1), :]
    hf_ref[...] = hf
    h_hi, h_lo = _split_bf16(hf)
    w_hi, w_lo = _split_bf16(wr_ref[...])
    logit_ref[...] = _dot(h_hi, w_hi) + (_dot(h_hi, w_lo) + _dot(h_lo, w_hi))


def _post_mixer(a_parts, w_out, x_parts, g_post, mod, g_pre, w_router, n_ctx, dec_seq):
    t = sum(x.shape[0] for x in x_parts)
    d = x_parts[0].shape[1]
    tm = ROW_TILE
    e = w_router.shape[1]
    assert e <= LANES
    row = lambda i: (i, 0)
    fixed = lambda i: (0, 0)
    return pl.pallas_call(
        functools.partial(_post_mixer_kernel, n_a=len(a_parts), n_x=len(x_parts), n_first=n_ctx // tm),
        grid=(t // tm,),
        in_specs=_split_specs(a_parts, tm, d) + _split_specs(x_parts, tm, d) + [
            pl.BlockSpec((d, d), fixed),
            pl.BlockSpec((1, d), fixed),
            pl.BlockSpec((None, MOD_ROWS, d), lambda i: (_cond_row(i * tm, n_ctx, dec_seq), 0, 0)),
            pl.BlockSpec((1, d), fixed),
            pl.BlockSpec((d, LANES), fixed),
        ],
        out_specs=[pl.BlockSpec((tm, d), row), pl.BlockSpec((tm, d), row), pl.BlockSpec((tm, LANES), row)],
        out_shape=[jax.ShapeDtypeStruct((t, d), _f32), jax.ShapeDtypeStruct((t, d), _f32),
                   jax.ShapeDtypeStruct((t, LANES), _f32)],
        compiler_params=_params(("parallel",)),
        name="post_mixer",
    )(*a_parts, *x_parts, w_out.astype(_bf16), g_post.reshape(1, d), mod, g_pre.reshape(1, d),
      jnp.pad(w_router, ((0, 0), (0, LANES - e))))


def _router_kernel(logit_ref, br_ref, idx_ref, wt_ref, rank_ref, cnt_ref, carry_ref, before_ref):
    i = pl.program_id(0)
    tm = logit_ref.shape[0]
    n_exp = br_ref.shape[0]

    @pl.when(i == 0)
    def _():
        carry_ref[...] = jnp.zeros_like(carry_ref)
        t_src = lax.broadcasted_iota(jnp.int32, (tm, tm), 0)
        t_dst = lax.broadcasted_iota(jnp.int32, (tm, tm), 1)
        before_ref[...] = jnp.where(t_src < t_dst, 1.0, 0.0).astype(_bf16)

    logits = logit_ref[...].T[:n_exp]
    grouped = (N_GROUPS, n_exp // N_GROUPS, tm)
    scores = jax.nn.sigmoid(logits)
    ids, ws, hits = _route((scores + br_ref[...]).reshape(grouped), scores.reshape(grouped))
    for k in range(TOP_K):
        idx_ref[pl.ds(k, 1), :] = ids[k][0]
        wt_ref[pl.ds(k, 1), :] = ws[k][0]

    any_hit = functools.reduce(lambda a, b: a | b, hits)
    mask = jnp.where(any_hit, 1.0, 0.0).reshape(n_exp, tm).astype(_bf16)
    rank = _dot(mask, before_ref[...]) + jnp.concatenate([carry_ref[...]] * (tm // LANES), axis=1)
    rank = rank.reshape(grouped)
    for k in range(TOP_K):
        rk = jnp.sum(jnp.where(hits[k], rank, 0.0), axis=1, keepdims=True).sum(axis=0, keepdims=True)
        rank_ref[pl.ds(k, 1), :] = rk[0].astype(jnp.int32)
    carry_ref[...] += _dot(mask, jnp.ones((tm, LANES), _bf16))
    cnt_ref[...] = carry_ref[...].astype(jnp.int32)


def _router(logits, b_router):
    t = logits.shape[0]
    e = b_router.shape[0]
    tm = min(ROUTER_TILE, t)
    col = lambda i: (0, i)
    fixed = lambda i: (0, 0)
    slots_i = jax.ShapeDtypeStruct((TOP_K, t), jnp.int32)
    return pl.pallas_call(
        _router_kernel,
        grid=(t // tm,),
        in_specs=[pl.BlockSpec((tm, LANES), lambda i: (i, 0)), pl.BlockSpec((e, 1), fixed)],
        out_specs=[pl.BlockSpec((TOP_K, tm), col), pl.BlockSpec((TOP_K, tm), col), pl.BlockSpec((TOP_K, tm), col),
                   pl.BlockSpec((e, LANES), fixed)],
        out_shape=[slots_i, jax.ShapeDtypeStruct((TOP_K, t), _f32), slots_i,
                   jax.ShapeDtypeStruct((e, LANES), jnp.int32)],
        scratch_shapes=[pltpu.VMEM((e, LANES), _f32), pltpu.VMEM((tm, tm), _bf16)],
        compiler_params=_params(("arbitrary",)),
        name="router",
    )(logits, b_router.reshape(e, 1))


def _row(ref, r):
    return ref.at[pl.ds(r, 1), :]


NOT_FIRST = -2
NO_NEXT = -1


def _plan_kernel(cnt_ref, idx_ref, rank_ref, pos_ref, te_ref, tbi_ref, tbo_ref, tv_ref, tnext_ref, trun_ref,
                 padrow_ref, padn_ref, offs_ref, *, tm):
    n_exp = cnt_ref.shape[0]
    n_tiles = te_ref.shape[0]

    def per_expert(e, carry):
        off, tile, run, prev_first = carry
        cnt = cnt_ref[e]
        n_t = (cnt + tm - 1) // tm
        offs_ref[e] = off
        padrow_ref[e] = off + cnt
        padn_ref[e] = n_t * tm - cnt

        def per_tile(j, c):
            te_ref[tile + j] = e
            tbi_ref[tile + j] = tile + j
            tbo_ref[tile + j] = tile + j
            tv_ref[tile + j] = jnp.minimum(cnt - j * tm, tm)
            tnext_ref[tile + j] = jnp.where(j == 0, NO_NEXT, NOT_FIRST)
            trun_ref[tile + j] = run
            return c

        lax.fori_loop(0, n_t, per_tile, 0)
        used = n_t > 0

        @pl.when(used & (prev_first >= 0))
        def _():
            tnext_ref[prev_first] = e

        return (off + n_t * tm, tile + n_t, run + used.astype(jnp.int32), jnp.where(used, tile, prev_first))

    zero = jnp.int32(0)
    _, live, _, _ = lax.fori_loop(0, n_exp, per_expert, (zero, zero, zero, jnp.int32(-1)))
    last = jnp.maximum(live - 1, 0)
    last_expert = te_ref[last]

    def dead_tile(i, c):
        te_ref[i] = last_expert
        tbi_ref[i] = last
        tbo_ref[i] = i
        tv_ref[i] = 0
        tnext_ref[i] = NOT_FIRST
        trun_ref[i] = 0
        return c

    lax.fori_loop(live, n_tiles, dead_tile, 0)

    idx = idx_ref[...]
    pos = rank_ref[...]
    for e in range(n_exp):
        pos = pos + jnp.where(idx == e, offs_ref[e], 0)
    pos_ref[...] = pos


def _plan(counts, idx_t, rank_t, n_tiles):
    n_exp = counts.shape[0]
    smem = pl.BlockSpec(memory_space=pltpu.SMEM)
    vmem = pl.BlockSpec(memory_space=pltpu.VMEM)
    tiles = jax.ShapeDtypeStruct((n_tiles,), jnp.int32)
    experts = jax.ShapeDtypeStruct((n_exp,), jnp.int32)
    return pl.pallas_call(
        functools.partial(_plan_kernel, tm=EXPERT_TILE),
        in_specs=[smem, vmem, vmem],
        out_specs=[vmem] + [smem] * 8,
        out_shape=[jax.ShapeDtypeStruct(idx_t.shape, jnp.int32)] + [tiles] * 6 + [experts] * 2,
        scratch_shapes=[pltpu.SMEM((n_exp,), jnp.int32)],
        name="plan",
    )(counts, idx_t, rank_t)


def _pad_copies(start, n, zero_ref, dst_ref, sem, pad_bits):
    single = n & (SUBLANES - 1)
    for s in range(SUBLANES - 1):
        yield s < single, pltpu.make_async_copy(_row(zero_ref, 0), _row(dst_ref, start + s), sem)
    base = pl.multiple_of(start + single, SUBLANES)
    groups = n // SUBLANES
    for b in range(pad_bits - (SUBLANES.bit_length() - 1)):
        rows = SUBLANES << b
        first = pl.multiple_of(base + ((groups >> (b + 1)) << (b + 1)) * SUBLANES, SUBLANES)
        copy = pltpu.make_async_copy(zero_ref.at[pl.ds(0, rows), :], dst_ref.at[pl.ds(first, rows), :], sem)
        yield ((groups >> b) & 1) == 1, copy


def _slot_specs(n_tokens, tile, step_of):
    per_k = n_tokens // tile
    return [pl.BlockSpec((tile,), lambda i, k=k: (k * per_k + step_of(i),), memory_space=pltpu.SMEM)
            for k in range(TOP_K)]


def _dispatch_kernel(*refs, pad_bits):
    pos_refs = refs[:TOP_K]
    padrow_ref, padn_ref, src_ref, dst_ref, zero_ref, sem, zero_sem = refs[TOP_K:]
    i = pl.program_id(0)
    tile = src_ref.shape[0]
    n_exp = padn_ref.shape[0]

    def for_each_pad_copy(fn):
        def body(e, c):
            for needed, copy in _pad_copies(padrow_ref[e], padn_ref[e], zero_ref, dst_ref, zero_sem, pad_bits):
                pl.when(needed)(functools.partial(fn, copy))
            return c
        lax.fori_loop(0, n_exp, body, 0)

    @pl.when(i == 0)
    def _():
        zero_ref[...] = jnp.zeros_like(zero_ref)
        for_each_pad_copy(lambda copy: copy.start())

    def issue(t, carry):
        for k in range(TOP_K):
            pltpu.make_async_copy(_row(src_ref, t), _row(dst_ref, pos_refs[k][t]), sem).start()
        return carry

    lax.fori_loop(0, tile, issue, 0)
    for k in range(TOP_K):
        pltpu.make_async_copy(src_ref, dst_ref.at[pl.ds(0, tile), :], sem).wait()

    @pl.when(i == 0)
    def _():
        for_each_pad_copy(lambda copy: copy.wait())


def _dispatch(hf, pos, padrow, padn, n_sorted):
    t, d = hf.shape
    tile = min(SCATTER_TILE, t)
    pad_bits = (EXPERT_TILE - 1).bit_length()
    smem = pl.BlockSpec(memory_space=pltpu.SMEM)
    return pl.pallas_call(
        functools.partial(_dispatch_kernel, pad_bits=pad_bits),
        grid=(t // tile,),
        in_specs=_slot_specs(t, tile, lambda i: i) + [
            smem, smem,
            pl.BlockSpec((tile, d), lambda i: (i, 0)),
        ],
        out_specs=pl.BlockSpec(memory_space=pl.ANY),
        out_shape=jax.ShapeDtypeStruct((n_sorted, d), _f32),
        scratch_shapes=[pltpu.VMEM((1 << (pad_bits - 1), d), _f32),
                        pltpu.SemaphoreType.DMA(()), pltpu.SemaphoreType.DMA(())],
        compiler_params=_params(("arbitrary",)),
        name="dispatch",
    )(*([pos] * TOP_K), padrow, padn, hf)


def _expert_kernel(te_ref, tbi_ref, tbo_ref, tv_ref, tnext_ref, trun_ref, x_ref, wg_hbm, wu_hbm, wd_hbm, o_ref,
                   wg_ref, wu_ref, wd_ref, wgb_ref, wub_ref, wdb_ref, sems, *, layer):
    i = pl.program_id(0)
    live = tv_ref[i] > 0
    slot = trun_ref[i] % 2

    def fetch(expert, s):
        return [pltpu.make_async_copy(src.at[layer, expert], dst.at[s], sems.at[s])
                for src, dst in ((wg_hbm, wg_ref), (wu_hbm, wu_ref), (wd_hbm, wd_ref))]

    @pl.when(i == 0)
    def _():
        for copy in fetch(te_ref[0], 0):
            copy.start()

    @pl.when(live & (tnext_ref[i] != NOT_FIRST))
    def _():
        @pl.when(tnext_ref[i] != NO_NEXT)
        def _():
            for copy in fetch(tnext_ref[i], 1 - slot):
                copy.start()

        for copy in fetch(te_ref[i], slot):
            copy.wait()
        wgb_ref[...] = wg_ref[slot].astype(_bf16)
        wub_ref[...] = wu_ref[slot].astype(_bf16)
        wdb_ref[...] = wd_ref[slot].astype(_bf16)

    @pl.when(live)
    def _():
        x = x_ref[...].astype(_bf16)
        g = _dot(x, wgb_ref[...])
        u = _dot(x, wub_ref[...])
        h = (g * jax.nn.sigmoid(g) * u).astype(_bf16)
        o_ref[...] = _dot(h, wdb_ref[...])

    @pl.when(jnp.logical_not(live))
    def _():
        o_ref[...] = jnp.zeros_like(o_ref)


def _experts(xs, w_gate, w_up, w_down, layer, tile_expert, tile_in, tile_out, tile_valid, tile_next, tile_run):
    _, n_exp, d, f = w_gate.shape
    tm = EXPERT_TILE
    n_tiles = tile_expert.shape[0]
    hbm = pl.BlockSpec(memory_space=pl.ANY)
    return pl.pallas_call(
        functools.partial(_expert_kernel, layer=layer),
        grid_spec=pltpu.PrefetchScalarGridSpec(
            num_scalar_prefetch=6,
            grid=(n_tiles,),
            in_specs=[pl.BlockSpec((tm, d), lambda i, te, tbi, tbo, tv, tn, tr: (tbi[i], 0)), hbm, hbm, hbm],
            out_specs=pl.BlockSpec((tm, d), lambda i, te, tbi, tbo, tv, tn, tr: (tbo[i], 0)),
            scratch_shapes=[pltpu.VMEM((2, d, f), _f32), pltpu.VMEM((2, d, f), _f32), pltpu.VMEM((2, f, d), _f32),
                            pltpu.VMEM((d, f), _bf16), pltpu.VMEM((d, f), _bf16), pltpu.VMEM((f, d), _bf16),
                            pltpu.SemaphoreType.DMA((2,))],
        ),
        out_shape=jax.ShapeDtypeStruct(xs.shape, _f32),
        compiler_params=_params(("arbitrary",)),
        name="experts",
    )(tile_expert, tile_in, tile_out, tile_valid, tile_next, tile_run, xs, w_gate, w_up, w_down)


def _combine_kernel(*refs, n_first):
    pos_refs, posn_refs = refs[:TOP_K], refs[TOP_K:2 * TOP_K]
    ys_ref, wt_ref, sh_ref, x1_ref, gpost_ref, mod_ref = refs[2 * TOP_K:2 * TOP_K + 6]
    rest = refs[2 * TOP_K + 6:]
    final = n_first is not None
    if final:
        yctx_ref, ylat_ref, buf0_ref, buf1_ref, f_ref, sems = rest
    else:
        gnext_ref, modn_ref, x2_ref, hn_ref, buf0_ref, buf1_ref, f_ref, sems = rest
    tm, d = x1_ref.shape
    i = pl.program_id(0)
    last = pl.num_programs(0) - 1

    def start_rows(p_refs, buf_ref, sem, t):
        for k in range(TOP_K):
            pltpu.make_async_copy(_row(ys_ref, p_refs[k][t]), buf_ref.at[k, pl.ds(t, 1), :], sem).start()

    def wait_tile(buf_ref, sem):
        for k in range(TOP_K):
            pltpu.make_async_copy(ys_ref.at[pl.ds(0, tm), :], buf_ref.at[k], sem).wait()

    @pl.when(i == 0)
    def _():
        def first(t, carry):
            start_rows(pos_refs, buf0_ref, sems.at[0], t)
            return carry
        lax.fori_loop(0, tm, first, 0)

    def step(cur_ref, cur_sem, nxt_ref, nxt_sem):
        wait_tile(cur_ref, cur_sem)

        def group(g, carry):
            for u in range(SUBLANES):
                start_rows(posn_refs, nxt_ref, nxt_sem, g * SUBLANES + u)
            rows = pl.ds(pl.multiple_of(g * SUBLANES, SUBLANES), SUBLANES)
            w = wt_ref[rows, :]
            wb = [jnp.broadcast_to(w[:, k:k + 1], (SUBLANES, LANES)) for k in range(TOP_K)]
            for j in range(d // LANES):
                cols = slice(j * LANES, (j + 1) * LANES)
                acc = sh_ref[rows, cols]
                for k in range(TOP_K):
                    acc = acc + wb[k] * cur_ref[k, rows, cols]
                f_ref[rows, cols] = acc
            return carry

        lax.fori_loop(0, tm // SUBLANES, group, 0)

        @pl.when(i == last)
        def _():
            wait_tile(nxt_ref, nxt_sem)

    pl.when(i % 2 == 0)(functools.partial(step, buf0_ref, sems.at[0], buf1_ref, sems.at[1]))
    pl.when(i % 2 == 1)(functools.partial(step, buf1_ref, sems.at[1], buf0_ref, sems.at[0]))
    f = f_ref[...]
    x2 = x1_ref[...] + mod_ref[pl.ds(MOD_GATE_FFN, 1), :] * _rms(f, gpost_ref[...])
    if final:
        in_ctx = pl.program_id(0) < n_first

        @pl.when(in_ctx)
        def _():
            yctx_ref[...] = x2

        @pl.when(jnp.logical_not(in_ctx))
        def _():
            ylat_ref[...] = x2
    else:
        x2_ref[...] = x2
        hn = (_rms(x2, gnext_ref[...]) * (1.0 + modn_ref[pl.ds(MOD_SCALE_MIX, 1), :])
              + modn_ref[pl.ds(MOD_SHIFT_MIX, 1), :])
        hn_ref[...] = hn.astype(hn_ref.dtype)


def _combine(ys, pos, wt, shared, x1, g_post, mod, n_ctx, dec_seq, g_next=None, mod_next=None):
    t, d = x1.shape
    tm = min(GATHER_TILE, t)
    n_steps = t // tm
    final = g_next is None
    row = lambda i: (i, 0)
    fixed = lambda i: (0, 0)
    cond = lambda i: (_cond_row(i * tm, n_ctx, dec_seq), 0, 0)
    in_specs = _slot_specs(t, tm, lambda i: i) + _slot_specs(t, tm, lambda i: jnp.minimum(i + 1, n_steps - 1)) + [
        pl.BlockSpec(memory_space=pl.ANY),
        pl.BlockSpec((tm, TOP_K), row),
        pl.BlockSpec((tm, d), row),
        pl.BlockSpec((tm, d), row),
        pl.BlockSpec((1, d), fixed),
        pl.BlockSpec((None, MOD_ROWS, d), cond),
    ]
    args = [pos] * (2 * TOP_K) + [ys, wt, shared, x1, g_post.reshape(1, d), mod]
    if final:
        n0, n1 = n_ctx // tm, (t - n_ctx) // tm
        out_specs = [pl.BlockSpec((tm, d), lambda i: (jnp.minimum(i, n0 - 1), 0)),
                     pl.BlockSpec((tm, d), lambda i: (jnp.clip(i - n0, 0, n1 - 1), 0))]
        out_shape = [jax.ShapeDtypeStruct((n_ctx, d), _f32), jax.ShapeDtypeStruct((t - n_ctx, d), _f32)]
    else:
        in_specs += [pl.BlockSpec((1, d), fixed), pl.BlockSpec((None, MOD_ROWS, d), cond)]
        args += [g_next.reshape(1, d), mod_next]
        out_specs = [pl.BlockSpec((tm, d), row), pl.BlockSpec((tm, d), row)]
        out_shape = [jax.ShapeDtypeStruct((t, d), _f32), jax.ShapeDtypeStruct((t, d), _bf16)]
    return pl.pallas_call(
        functools.partial(_combine_kernel, n_first=n_ctx // tm if final else None),
        grid=(n_steps,),
        in_specs=in_specs,
        out_specs=out_specs,
        out_shape=out_shape,
        scratch_shapes=[pltpu.VMEM((TOP_K, tm, d), _f32), pltpu.VMEM((TOP_K, tm, d), _f32), pltpu.VMEM((tm, d), _f32),
                        pltpu.SemaphoreType.DMA((2,))],
        compiler_params=_params(("arbitrary",)),
        name="combine",
    )(*args)


def _moe(hf, idx_t, wt_t, rank_t, cnt, layer, w_gate, w_up, w_down, w_sh_gate, w_sh_up, w_sh_down):
    n_exp, d = w_gate.shape[1:3]
    t = hf.shape[0]
    tm = EXPERT_TILE
    n_tiles = t * TOP_K // tm + n_exp
    pos, *tiles, padrow, padn = _plan(cnt[:, 0], idx_t, rank_t, n_tiles)
    pos = pos.reshape(-1)
    xs = _dispatch(hf, pos, padrow, padn, n_tiles * tm)
    ys = _experts(xs, w_gate, w_up, w_down, layer, *tiles)
    n_sh = t // tm
    every = jnp.arange(n_sh, dtype=jnp.int32)
    zeros = jnp.zeros((n_sh,), jnp.int32)
    shared = _experts(hf, w_sh_gate[:, None], w_sh_up[:, None], w_sh_down[:, None], layer,
                      zeros, every, every, jnp.full((n_sh,), tm, jnp.int32),
                      jnp.full((n_sh,), NOT_FIRST, jnp.int32).at[0].set(NO_NEXT), zeros)
    return ys, pos, wt_t.T, shared


def kernel(x_prompt, x_sample, cache_k, cache_v, c, c_ctx, w_ada, b_ada, norm_mix_pre, norm_mix_post, norm_ffn_pre, norm_ffn_post, w_qkv, w_o_attn, rpb, w_conv_in, conv_w, w_conv_out, w_router, b_router, w_exp_gate, w_exp_up, w_exp_down, w_sh_gate, w_sh_up, w_sh_down):
    batch, seq, d = x_prompt.shape
    dec_batch, dec_seq, _ = x_sample.shape
    depth = w_ada.shape[0]
    n_ctx, n_lat = batch * seq, dec_batch * dec_seq
    dh = d // N_HEADS
    past = cache_k.shape[2]

    x = [x_prompt.reshape(n_ctx, d), x_sample.reshape(n_lat, d)]
    cond = jnp.concatenate([c_ctx[None, :], c], axis=0)
    mod = _adaln(cond, w_ada, b_ada)

    new_k, new_v = [], []
    h = _modulate(x, norm_mix_pre[0], mod[0], n_ctx, dec_seq)
    for l in range(depth):
        if l % 2 == 0:
            a = l // 2
            qkv = _matmul(h, w_qkv[a])
            o_ctx, k_new, v_new = _ctx_attention(qkv, batch, seq, d)
            new_k.append(k_new.reshape(batch, seq, N_HEADS, dh))
            new_v.append(v_new.reshape(batch, seq, N_HEADS, dh))
            o_lat = _natten(qkv, n_ctx, cache_k[:, a].reshape(dec_batch, past, d),
                            cache_v[:, a].reshape(dec_batch, past, d), rpb[a], dec_batch, dec_seq, d)
            mixed = [o_ctx, o_lat]
            w_out = w_o_attn[a]
        else:
            m = l // 2
            mixed = [_conv_gate(_matmul(h, w_conv_in[m]), conv_w[m], n_ctx, seq, dec_seq, d)]
            w_out = w_conv_out[m]
        x1, hf, logits = _post_mixer(
            mixed, w_out, x, norm_mix_post[l], mod[l], norm_ffn_pre[l], w_router[l], n_ctx, dec_seq)
        idx_t, wt_t, rank_t, cnt = _router(logits, b_router[l])
        ys, pos, wt, shared = _moe(hf, idx_t, wt_t, rank_t, cnt, l, w_exp_gate, w_exp_up, w_exp_down,
                                   w_sh_gate, w_sh_up, w_sh_down)
        if l + 1 < depth:
            x2, h = _combine(ys, pos, wt, shared, x1, norm_ffn_post[l], mod[l], n_ctx, dec_seq,
                             norm_mix_pre[l + 1], mod[l + 1])
            x = [x2]
        else:
            y_ctx, y_lat = _combine(ys, pos, wt, shared, x1, norm_ffn_post[l], mod[l], n_ctx, dec_seq)

    return (y_ctx.reshape(batch, seq, d), y_lat.reshape(dec_batch, dec_seq, d),
            jnp.stack(new_k, axis=1), jnp.stack(new_v, axis=1))
```

```python
import functools

import jax
import jax.numpy as jnp
from jax import lax
from jax.experimental import pallas as pl
from jax.experimental.pallas import tpu as pltpu

N_HEADS = 16
GRID_W = 64
WIN_ROWS = 8
WIN_COLS = 16
CONV_WIDTH = 3
N_EXPERTS = 64
TOP_K = 8
N_GROUPS = 8
TOPK_GROUPS = 4
ROUTED_SCALE = 2.5
N_MOD = 6
RMS_EPS = 1e-6
NEG_INF = -1e30

LANES = 128
SUBLANES = 8
VMEM_LIMIT = 56 * 1024 * 1024

MOD_SHIFT_MIX, MOD_SCALE_MIX, MOD_GATE_MIX, MOD_SHIFT_FFN, MOD_SCALE_FFN, MOD_GATE_FFN = range(6)
MOD_ROWS = 8

ROW_TILE = 256
ROUTER_TILE = 1024
EXPERT_TILE = 256
MM_TILE_M = 2048
MM_TILE_N = 512
GATHER_TILE = 128
SCATTER_TILE = 512

_f32 = jnp.float32
_bf16 = jnp.bfloat16


def _params(sem, vmem=VMEM_LIMIT):
    return pltpu.CompilerParams(dimension_semantics=sem, vmem_limit_bytes=vmem)


def _rms(x, g):
    return x * lax.rsqrt(jnp.mean(x * x, axis=-1, keepdims=True) + RMS_EPS) * g


def _dot(a, b):
    return jnp.dot(a, b, preferred_element_type=_f32)


def _dot_nt(a, b):
    return lax.dot_general(a, b, (((1,), (1,)), ((), ())), preferred_element_type=_f32)


def _cond_row(row0, n_ctx, dec_seq):
    return jnp.where(row0 < n_ctx, 0, 1 + (row0 - n_ctx) // dec_seq)


def _adaln_kernel(cb_ref, w_ref, b_ref, o_ref, acc_ref, *, n_cond):
    k = pl.program_id(2)

    @pl.when(k == 0)
    def _():
        acc_ref[...] = jnp.zeros_like(acc_ref)

    tk, tn = w_ref.shape
    kb = min(LANES, tk)
    for k0 in range(0, tk, kb):
        s = []
        for r in range(n_cond):
            v = cb_ref[r, k0:k0 + kb, :]
            s.append(v * jax.nn.sigmoid(v))
        for c in range(tn // LANES):
            cols = slice(c * LANES, (c + 1) * LANES)
            w = w_ref[k0:k0 + kb, cols]
            for r in range(n_cond):
                acc_ref[r, :, cols] += (w * s[r]).reshape(kb // SUBLANES, SUBLANES, LANES).sum(axis=0)

    @pl.when(k == pl.num_programs(2) - 1)
    def _():
        o_ref[...] = jnp.zeros_like(o_ref)
        for r in range(n_cond):
            o_ref[pl.ds(r, 1), :] = acc_ref[r].sum(axis=0, keepdims=True) + b_ref[...]


def _adaln(cond, w_ada, b_ada):
    n_cond, d = cond.shape
    n_layers, _, n6 = w_ada.shape
    tk, tn = min(512, d), min(2048, n6)
    cb = jnp.broadcast_to(cond[:, :, None], (n_cond, d, LANES))
    mod = pl.pallas_call(
        functools.partial(_adaln_kernel, n_cond=n_cond),
        grid=(n_layers, n6 // tn, d // tk),
        in_specs=[
            pl.BlockSpec((n_cond, tk, LANES), lambda l, n, k: (0, k, 0)),
            pl.BlockSpec((None, tk, tn), lambda l, n, k: (l, k, n)),
            pl.BlockSpec((None, 1, tn), lambda l, n, k: (l, 0, n)),
        ],
        out_specs=pl.BlockSpec((None, MOD_ROWS, tn), lambda l, n, k: (l, 0, n)),
        out_shape=jax.ShapeDtypeStruct((n_layers, MOD_ROWS, n6), _f32),
        scratch_shapes=[pltpu.VMEM((n_cond, SUBLANES, tn), _f32)],
        compiler_params=_params(("arbitrary", "arbitrary", "arbitrary")),
        name="adaln",
    )(cb, w_ada, b_ada.reshape(n_layers, 1, n6))
    mod = mod[:, :n_cond].reshape(n_layers, n_cond, N_MOD, d)
    return jnp.pad(mod, ((0, 0), (0, 0), (0, MOD_ROWS - N_MOD), (0, 0)))


def _split_specs(parts, tm, width):
    if len(parts) == 1:
        return [pl.BlockSpec((tm, width), lambda i: (i, 0))]
    n0 = parts[0].shape[0] // tm
    n1 = parts[1].shape[0] // tm
    return [pl.BlockSpec((tm, width), lambda i: (jnp.minimum(i, n0 - 1), 0)),
            pl.BlockSpec((tm, width), lambda i: (jnp.clip(i - n0, 0, n1 - 1), 0))]


def _split_read(refs, n_first):
    if len(refs) == 1:
        return refs[0][...]
    return jnp.where(pl.program_id(0) < n_first, refs[0][...], refs[1][...])


def _modulate_kernel(*refs, n_x, n_first):
    x_refs, (g_ref, mod_ref, o_ref) = refs[:n_x], refs[n_x:]
    y = _rms(_split_read(x_refs, n_first), g_ref[...])
    h = y * (1.0 + mod_ref[pl.ds(MOD_SCALE_MIX, 1), :]) + mod_ref[pl.ds(MOD_SHIFT_MIX, 1), :]
    o_ref[...] = h.astype(o_ref.dtype)


def _modulate(xs, g, mod, n_ctx, dec_seq):
    t = sum(x.shape[0] for x in xs)
    d = xs[0].shape[1]
    tm = ROW_TILE
    return pl.pallas_call(
        functools.partial(_modulate_kernel, n_x=len(xs), n_first=xs[0].shape[0] // tm),
        grid=(t // tm,),
        in_specs=_split_specs(xs, tm, d) + [
            pl.BlockSpec((1, d), lambda i: (0, 0)),
            pl.BlockSpec((None, MOD_ROWS, d), lambda i: (_cond_row(i * tm, n_ctx, dec_seq), 0, 0)),
        ],
        out_specs=pl.BlockSpec((tm, d), lambda i: (i, 0)),
        out_shape=jax.ShapeDtypeStruct((t, d), _bf16),
        compiler_params=_params(("parallel",)),
        name="modulate",
    )(*xs, g.reshape(1, d), mod)


def _matmul_kernel(a_ref, w_ref, o_ref):
    o_ref[...] = _dot(a_ref[...], w_ref[...].astype(_bf16)).astype(o_ref.dtype)


def _matmul(a, w, layer):
    n_rows = a.shape[0]
    _, k, n = w.shape
    tm, tn = min(MM_TILE_M, n_rows), min(MM_TILE_N, n)
    return pl.pallas_call(
        _matmul_kernel,
        grid=(n_rows // tm, n // tn),
        in_specs=[
            pl.BlockSpec((tm, k), lambda i, j: (i, 0)),
            pl.BlockSpec((None, k, tn), lambda i, j: (layer, 0, j)),
        ],
        out_specs=pl.BlockSpec((tm, tn), lambda i, j: (i, j)),
        out_shape=jax.ShapeDtypeStruct((n_rows, n), _f32),
        compiler_params=_params(("parallel", "parallel")),
        name="matmul",
    )(a, w)


def _ctx_attn_kernel(q_ref, k_ref, v_ref, o_ref, kout_ref, vout_ref, *, n_heads, scale):
    seq = q_ref.shape[0]
    dh = q_ref.shape[1] // n_heads
    for h in range(n_heads):
        sl = slice(h * dh, (h + 1) * dh)
        k32 = k_ref[:, sl]
        v32 = v_ref[:, sl]
        kout_ref[pl.ds(h, seq, stride=n_heads), :] = k32
        vout_ref[pl.ds(h, seq, stride=n_heads), :] = v32
        q = q_ref[:, sl].astype(_bf16)
        s = _dot_nt(q, k32.astype(_bf16)) * scale
        p = jnp.exp(s - s.max(axis=-1, keepdims=True))
        o = _dot(p.astype(_bf16), v32.astype(_bf16)) / p.sum(axis=-1, keepdims=True)
        o_ref[:, sl] = o.astype(o_ref.dtype)


def _ctx_attention(qkv, batch, seq, d):
    dh = d // N_HEADS
    assert dh == LANES
    cache = jax.ShapeDtypeStruct((batch * seq * N_HEADS, dh), _f32)
    return pl.pallas_call(
        functools.partial(_ctx_attn_kernel, n_heads=N_HEADS, scale=dh ** -0.5),
        grid=(batch,),
        in_specs=[
            pl.BlockSpec((seq, d), lambda b: (b, 0)),
            pl.BlockSpec((seq, d), lambda b: (b, 1)),
            pl.BlockSpec((seq, d), lambda b: (b, 2)),
        ],
        out_specs=[
            pl.BlockSpec((seq, d), lambda b: (b, 0)),
            pl.BlockSpec((seq * N_HEADS, dh), lambda b: (b, 0)),
            pl.BlockSpec((seq * N_HEADS, dh), lambda b: (b, 0)),
        ],
        out_shape=[jax.ShapeDtypeStruct((batch * seq, d), _bf16), cache, cache],
        compiler_params=_params(("parallel",)),
        name="ctx_attention",
    )(qkv, qkv, qkv)


def _natten_kernel(q_ref, k_ref, v_ref, kc_ref, vc_ref, bias_ref, o_ref, qb_ref, kb_ref, vb_ref,
                   sctx_ref, pctx_ref, oloc_ref, den_ref, *, rows, width, kh, scale, unroll):
    n_loc = kh * width
    q_col = lax.broadcasted_iota(jnp.int32, (width, n_loc), 0)
    k_col = lax.broadcasted_iota(jnp.int32, (width, n_loc), 1) % width
    col_start = jnp.clip(q_col - WIN_COLS // 2, 0, width - WIN_COLS)
    col_mask = (k_col >= col_start) & (k_col < col_start + WIN_COLS)

    qb_ref[...] = q_ref[...].astype(_bf16)
    kb_ref[...] = k_ref[...].astype(_bf16)
    vb_ref[...] = v_ref[...].astype(_bf16)
    sctx_ref[...] = _dot_nt(qb_ref[...], kc_ref[...].astype(_bf16)) * scale

    def one_row(r):
        r0 = jnp.clip(r - kh // 2, 0, rows - kh)
        qrows = pl.ds(pl.multiple_of(r * width, width), width)
        win = pl.ds(pl.multiple_of(r0 * width, width), n_loc)
        s_loc = _dot_nt(qb_ref[qrows, :], kb_ref[win, :]) * scale + bias_ref[r - r0]
        s_loc = jnp.where(col_mask, s_loc, NEG_INF)
        s_ctx = sctx_ref[qrows, :]
        m = jnp.maximum(s_loc.max(axis=-1, keepdims=True), s_ctx.max(axis=-1, keepdims=True))
        p_loc = jnp.exp(s_loc - m)
        p_ctx = jnp.exp(s_ctx - m)
        den = p_loc.sum(axis=-1, keepdims=True) + p_ctx.sum(axis=-1, keepdims=True)
        pctx_ref[qrows, :] = p_ctx.astype(_bf16)
        oloc_ref[qrows, :] = _dot(p_loc.astype(_bf16), vb_ref[win, :])
        den_ref[qrows, :] = jnp.broadcast_to(den, (width, den_ref.shape[1]))

    def some_rows(g, carry):
        for u in range(unroll):
            one_row(g * unroll + u)
        return carry

    lax.fori_loop(0, rows // unroll, some_rows, 0)
    o = oloc_ref[...] + _dot(pctx_ref[...], vc_ref[...].astype(_bf16))
    o_ref[...] = (o / den_ref[...]).astype(o_ref.dtype)


def _natten_bias(rpb, kh):
    col = jnp.arange(GRID_W)
    dc = jnp.clip(col[None, :] - col[:, None] + WIN_COLS - 1, 0, 2 * WIN_COLS - 2)
    dr = jnp.arange(kh)[None, :] - jnp.arange(kh)[:, None] + (WIN_ROWS - 1)
    t = rpb[:, dr][..., dc]
    return t.transpose(0, 1, 3, 2, 4).reshape(rpb.shape[0], kh, GRID_W, kh * GRID_W).astype(_f32)


def _natten(qkv, n_ctx, k_ctx, v_ctx, rpb, dec_batch, dec_seq, d):
    assert n_ctx % dec_seq == 0
    dh = d // N_HEADS
    rows = dec_seq // GRID_W
    kh = min(WIN_ROWS, rows)
    past = k_ctx.shape[1]
    bias = _natten_bias(rpb, kh)
    b0 = n_ctx // dec_seq
    unroll = next(u for u in (8, 4, 2, 1) if rows % u == 0)
    return pl.pallas_call(
        functools.partial(_natten_kernel, rows=rows, width=GRID_W, kh=kh, scale=dh ** -0.5, unroll=unroll),
        grid=(dec_batch, N_HEADS),
        in_specs=[
            pl.BlockSpec((dec_seq, dh), lambda b, h: (b0 + b, h)),
            pl.BlockSpec((dec_seq, dh), lambda b, h: (b0 + b, N_HEADS + h)),
            pl.BlockSpec((dec_seq, dh), lambda b, h: (b0 + b, 2 * N_HEADS + h)),
            pl.BlockSpec((None, past, dh), lambda b, h: (b, 0, h)),
            pl.BlockSpec((None, past, dh), lambda b, h: (b, 0, h)),
            pl.BlockSpec((None, kh, GRID_W, kh * GRID_W), lambda b, h: (h, 0, 0, 0)),
        ],
        out_specs=pl.BlockSpec((dec_seq, dh), lambda b, h: (b, h)),
        out_shape=jax.ShapeDtypeStruct((dec_batch * dec_seq, d), _bf16),
        scratch_shapes=[pltpu.VMEM((dec_seq, dh), _bf16), pltpu.VMEM((dec_seq, dh), _bf16),
                        pltpu.VMEM((dec_seq, dh), _bf16),
                        pltpu.VMEM((dec_seq, past), _f32), pltpu.VMEM((dec_seq, past), _bf16),
                        pltpu.VMEM((dec_seq, dh), _f32), pltpu.VMEM((dec_seq, dh), _f32)],
        compiler_params=_params(("parallel", "parallel")),
        name="natten",
    )(qkv, qkv, qkv, k_ctx, v_ctx, bias)


def _conv_gate_kernel(b_ref, c_ref, u_ref, cp_ref, up_ref, cn_ref, un_ref, w_ref, o_ref,
                      *, n_ctx, seq, dec_seq):
    tm = o_ref.shape[0]
    row0 = pl.program_id(0) * tm
    in_ctx = row0 < n_ctx
    pos0 = jnp.where(in_ctx, row0 % seq, (row0 - n_ctx) % dec_seq)
    seq_len = jnp.where(in_ctx, seq, dec_seq)
    has_prev = pos0 > 0
    has_next = pos0 + tm < seq_len
    cu = c_ref[...] * u_ref[...]
    prev_row = jnp.where(has_prev, cp_ref[pl.ds(SUBLANES - 1, 1), :] * up_ref[pl.ds(SUBLANES - 1, 1), :], 0.0)
    next_row = jnp.where(has_next, cn_ref[pl.ds(0, 1), :] * un_ref[pl.ds(0, 1), :], 0.0)
    ridx = lax.broadcasted_iota(jnp.int32, cu.shape, 0)
    before = jnp.where(ridx == 0, prev_row, pltpu.roll(cu, 1, axis=0))
    after = jnp.where(ridx == tm - 1, next_row, pltpu.roll(cu, tm - 1, axis=0))
    conv = w_ref[pl.ds(0, 1), :] * before + w_ref[pl.ds(1, 1), :] * cu + w_ref[pl.ds(2, 1), :] * after
    o_ref[...] = (b_ref[...] * conv).astype(o_ref.dtype)


def _conv_gate(bcu, conv_w, n_ctx, seq, dec_seq, d):
    t = bcu.shape[0]
    tm = ROW_TILE
    halo = tm // SUBLANES
    last = t // SUBLANES - 1
    cw = jnp.pad(conv_w, ((0, SUBLANES - CONV_WIDTH), (0, 0)))
    prev_map = lambda col: (lambda i: (jnp.maximum(i * halo - 1, 0), col))
    next_map = lambda col: (lambda i: (jnp.minimum((i + 1) * halo, last), col))
    return pl.pallas_call(
        functools.partial(_conv_gate_kernel, n_ctx=n_ctx, seq=seq, dec_seq=dec_seq),
        grid=(t // tm,),
        in_specs=[
            pl.BlockSpec((tm, d), lambda i: (i, 0)),
            pl.BlockSpec((tm, d), lambda i: (i, 1)),
            pl.BlockSpec((tm, d), lambda i: (i, 2)),
            pl.BlockSpec((SUBLANES, d), prev_map(1)),
            pl.BlockSpec((SUBLANES, d), prev_map(2)),
            pl.BlockSpec((SUBLANES, d), next_map(1)),
            pl.BlockSpec((SUBLANES, d), next_map(2)),
            pl.BlockSpec((SUBLANES, d), lambda i: (0, 0)),
        ],
        out_specs=pl.BlockSpec((tm, d), lambda i: (i, 0)),
        out_shape=jax.ShapeDtypeStruct((t, d), _bf16),
        compiler_params=_params(("parallel",)),
        name="conv_gate",
    )(bcu, bcu, bcu, bcu, bcu, bcu, bcu, cw)


def _route(sel, scores):
    n_grp, eg, tm = sel.shape
    n_exp = n_grp * eg
    j_iota = lax.broadcasted_iota(jnp.int32, sel.shape, 1)
    m1 = sel.max(axis=1, keepdims=True)
    j1 = jnp.min(jnp.where(sel == m1, j_iota, eg), axis=1, keepdims=True)
    m2 = jnp.max(jnp.where(j_iota == j1, -jnp.inf, sel), axis=1, keepdims=True)
    grp = m1 + m2
    g_iota = lax.broadcasted_iota(jnp.int32, grp.shape, 0)
    g_sel = g_iota < 0
    for _ in range(TOPK_GROUPS):
        gm = grp.max(axis=0, keepdims=True)
        gi = jnp.min(jnp.where(grp == gm, g_iota, n_grp), axis=0, keepdims=True)
        hit = g_iota == gi
        g_sel = g_sel | hit
        grp = jnp.where(hit, -jnp.inf, grp)
    cur = jnp.where(jnp.broadcast_to(g_sel, sel.shape), sel, NEG_INF)
    e_iota = lax.broadcasted_iota(jnp.int32, sel.shape, 0) * eg + j_iota
    ids, ws, hits = [], [], []
    for _ in range(TOP_K):
        m = cur.max(axis=1, keepdims=True).max(axis=0, keepdims=True)
        ei = jnp.min(jnp.where(cur == m, e_iota, n_exp), axis=1, keepdims=True).min(axis=0, keepdims=True)
        hit = e_iota == ei
        ids.append(ei)
        ws.append(jnp.sum(jnp.where(hit, scores, 0.0), axis=1, keepdims=True).sum(axis=0, keepdims=True))
        hits.append(hit)
        cur = jnp.where(hit, -jnp.inf, cur)
    total = functools.reduce(lambda a, b: a + b, ws)
    ws = [w / total * ROUTED_SCALE for w in ws]
    return ids, ws, hits


def _split_bf16(x):
    hi = x.astype(_bf16)
    return hi, (x - hi.astype(_f32)).astype(_bf16)


def _post_mixer_kernel(*refs, n_a, n_x, n_first):
    a_refs, x_refs = refs[:n_a], refs[n_a:n_a + n_x]
    wo_ref, gpost_ref, mod_ref, gpre_ref, wr_ref, x1_ref, hf_ref, logit_ref = refs[n_a + n_x:]
    o = _dot(_split_read(a_refs, n_first), wo_ref[...])
    x1 = _split_read(x_refs, n_first) + mod_ref[pl.ds(MOD_GATE_MIX, 1), :] * _rms(o, gpost_ref[...])
    x1_ref[...] = x1
    hf = _rms(x1, gpre_ref[...]) * (1.0 + mod_ref[pl.ds(MOD_SCALE_FFN, 1), :]) + mod_ref[pl.ds(MOD_SHIFT_FFN, 1), :]
    hf_ref[...] = hf
    h_hi, h_lo = _split_bf16(hf)
    w_hi, w_lo = _split_bf16(wr_ref[...])
    logit_ref[...] = _dot(h_hi, w_hi) + (_dot(h_hi, w_lo) + _dot(h_lo, w_hi))


def _post_mixer(a_parts, w_out, layer, x_parts, g_post, mod, g_pre, w_router, n_ctx, dec_seq):
    t = sum(x.shape[0] for x in x_parts)
    d = x_parts[0].shape[1]
    tm = ROW_TILE
    e = w_router.shape[1]
    assert e <= LANES
    row = lambda i: (i, 0)
    fixed = lambda i: (0, 0)
    return pl.pallas_call(
        functools.partial(_post_mixer_kernel, n_a=len(a_parts), n_x=len(x_parts), n_first=n_ctx // tm),
        grid=(t // tm,),
        in_specs=_split_specs(a_parts, tm, d) + _split_specs(x_parts, tm, d) + [
            pl.BlockSpec((None, d, d), lambda i: (layer, 0, 0)),
            pl.BlockSpec((1, d), fixed),
            pl.BlockSpec((None, MOD_ROWS, d), lambda i: (_cond_row(i * tm, n_ctx, dec_seq), 0, 0)),
            pl.BlockSpec((1, d), fixed),
            pl.BlockSpec((d, LANES), fixed),
        ],
        out_specs=[pl.BlockSpec((tm, d), row), pl.BlockSpec((tm, d), row), pl.BlockSpec((tm, LANES), row)],
        out_shape=[jax.ShapeDtypeStruct((t, d), _f32), jax.ShapeDtypeStruct((t, d), _f32),
                   jax.ShapeDtypeStruct((t, LANES), _f32)],
        compiler_params=_params(("parallel",)),
        name="post_mixer",
    )(*a_parts, *x_parts, w_out.astype(_bf16), g_post.reshape(1, d), mod, g_pre.reshape(1, d),
      jnp.pad(w_router, ((0, 0), (0, LANES - e))))


def _router_kernel(logit_ref, br_ref, idx_ref, wt_ref, rank_ref, cnt_ref, carry_ref, before_ref):
    i = pl.program_id(0)
    tm = logit_ref.shape[0]
    n_exp = br_ref.shape[0]

    @pl.when(i == 0)
    def _():
        carry_ref[...] = jnp.zeros_like(carry_ref)
        t_src = lax.broadcasted_iota(jnp.int32, (tm, tm), 0)
        t_dst = lax.broadcasted_iota(jnp.int32, (tm, tm), 1)
        before_ref[...] = jnp.where(t_src < t_dst, 1.0, 0.0).astype(_bf16)

    logits = logit_ref[...].T[:n_exp]
    grouped = (N_GROUPS, n_exp // N_GROUPS, tm)
    scores = jax.nn.sigmoid(logits)
    ids, ws, hits = _route((scores + br_ref[...]).reshape(grouped), scores.reshape(grouped))
    for k in range(TOP_K):
        idx_ref[pl.ds(k, 1), :] = ids[k][0]
        wt_ref[pl.ds(k, 1), :] = ws[k][0]

    any_hit = functools.reduce(lambda a, b: a | b, hits)
    mask = jnp.where(any_hit, 1.0, 0.0).reshape(n_exp, tm).astype(_bf16)
    rank = _dot(mask, before_ref[...]) + jnp.concatenate([carry_ref[...]] * (tm // LANES), axis=1)
    rank = rank.reshape(grouped)
    for k in range(TOP_K):
        rk = jnp.sum(jnp.where(hits[k], rank, 0.0), axis=1, keepdims=True).sum(axis=0, keepdims=True)
        rank_ref[pl.ds(k, 1), :] = rk[0].astype(jnp.int32)
    carry_ref[...] += _dot(mask, jnp.ones((tm, LANES), _bf16))
    cnt_ref[...] = carry_ref[...].astype(jnp.int32)


def _router(logits, b_router):
    t = logits.shape[0]
    e = b_router.shape[0]
    tm = min(ROUTER_TILE, t)
    col = lambda i: (0, i)
    fixed = lambda i: (0, 0)
    slots_i = jax.ShapeDtypeStruct((TOP_K, t), jnp.int32)
    return pl.pallas_call(
        _router_kernel,
        grid=(t // tm,),
        in_specs=[pl.BlockSpec((tm, LANES), lambda i: (i, 0)), pl.BlockSpec((e, 1), fixed)],
        out_specs=[pl.BlockSpec((TOP_K, tm), col), pl.BlockSpec((TOP_K, tm), col), pl.BlockSpec((TOP_K, tm), col),
                   pl.BlockSpec((e, LANES), fixed)],
        out_shape=[slots_i, jax.ShapeDtypeStruct((TOP_K, t), _f32), slots_i,
                   jax.ShapeDtypeStruct((e, LANES), jnp.int32)],
        scratch_shapes=[pltpu.VMEM((e, LANES), _f32), pltpu.VMEM((tm, tm), _bf16)],
        compiler_params=_params(("arbitrary",)),
        name="router",
    )(logits, b_router.reshape(e, 1))


def _row(ref, r):
    return ref.at[pl.ds(r, 1), :]


NOT_FIRST = -2
NO_NEXT = -1


def _plan_kernel(cnt_ref, idx_ref, rank_ref, pos_ref, te_ref, tbi_ref, tbo_ref, tv_ref, tnext_ref, trun_ref,
                 padrow_ref, padn_ref, offs_ref, *, tm):
    n_exp = cnt_ref.shape[0]
    n_tiles = te_ref.shape[0]

    def per_expert(e, carry):
        off, tile, run, prev_first = carry
        cnt = cnt_ref[e]
        n_t = (cnt + tm - 1) // tm
        offs_ref[e] = off
        padrow_ref[e] = off + cnt
        padn_ref[e] = n_t * tm - cnt

        def per_tile(j, c):
            te_ref[tile + j] = e
            tbi_ref[tile + j] = tile + j
            tbo_ref[tile + j] = tile + j
            tv_ref[tile + j] = jnp.minimum(cnt - j * tm, tm)
            tnext_ref[tile + j] = jnp.where(j == 0, NO_NEXT, NOT_FIRST)
            trun_ref[tile + j] = run
            return c

        lax.fori_loop(0, n_t, per_tile, 0)
        used = n_t > 0

        @pl.when(used & (prev_first >= 0))
        def _():
            tnext_ref[prev_first] = e

        return (off + n_t * tm, tile + n_t, run + used.astype(jnp.int32), jnp.where(used, tile, prev_first))

    zero = jnp.int32(0)
    _, live, _, _ = lax.fori_loop(0, n_exp, per_expert, (zero, zero, zero, jnp.int32(-1)))
    last = jnp.maximum(live - 1, 0)
    last_expert = te_ref[last]

    def dead_tile(i, c):
        te_ref[i] = last_expert
        tbi_ref[i] = last
        tbo_ref[i] = i
        tv_ref[i] = 0
        tnext_ref[i] = NOT_FIRST
        trun_ref[i] = 0
        return c

    lax.fori_loop(live, n_tiles, dead_tile, 0)

    idx = idx_ref[...]
    pos = rank_ref[...]
    for e in range(n_exp):
        pos = pos + jnp.where(idx == e, offs_ref[e], 0)
    pos_ref[...] = pos


def _plan(counts, idx_t, rank_t, n_tiles):
    n_exp = counts.shape[0]
    smem = pl.BlockSpec(memory_space=pltpu.SMEM)
    vmem = pl.BlockSpec(memory_space=pltpu.VMEM)
    tiles = jax.ShapeDtypeStruct((n_tiles,), jnp.int32)
    experts = jax.ShapeDtypeStruct((n_exp,), jnp.int32)
    return pl.pallas_call(
        functools.partial(_plan_kernel, tm=EXPERT_TILE),
        in_specs=[smem, vmem, vmem],
        out_specs=[vmem] + [smem] * 8,
        out_shape=[jax.ShapeDtypeStruct(idx_t.shape, jnp.int32)] + [tiles] * 6 + [experts] * 2,
        scratch_shapes=[pltpu.SMEM((n_exp,), jnp.int32)],
        name="plan",
    )(counts, idx_t, rank_t)


def _pad_copies(start, n, zero_ref, dst_ref, sem, pad_bits):
    single = n & (SUBLANES - 1)
    for s in range(SUBLANES - 1):
        yield s < single, pltpu.make_async_copy(_row(zero_ref, 0), _row(dst_ref, start + s), sem)
    base = pl.multiple_of(start + single, SUBLANES)
    groups = n // SUBLANES
    for b in range(pad_bits - (SUBLANES.bit_length() - 1)):
        rows = SUBLANES << b
        first = pl.multiple_of(base + ((groups >> (b + 1)) << (b + 1)) * SUBLANES, SUBLANES)
        copy = pltpu.make_async_copy(zero_ref.at[pl.ds(0, rows), :], dst_ref.at[pl.ds(first, rows), :], sem)
        yield ((groups >> b) & 1) == 1, copy


def _slot_specs(n_tokens, tile, step_of):
    per_k = n_tokens // tile
    return [pl.BlockSpec((tile,), lambda i, k=k: (k * per_k + step_of(i),), memory_space=pltpu.SMEM)
            for k in range(TOP_K)]


def _dispatch_kernel(*refs, pad_bits):
    pos_refs = refs[:TOP_K]
    padrow_ref, padn_ref, src_ref, dst_ref, zero_ref, sem, zero_sem = refs[TOP_K:]
    i = pl.program_id(0)
    tile = src_ref.shape[0]
    n_exp = padn_ref.shape[0]

    def for_each_pad_copy(fn):
        def body(e, c):
            for needed, copy in _pad_copies(padrow_ref[e], padn_ref[e], zero_ref, dst_ref, zero_sem, pad_bits):
                pl.when(needed)(functools.partial(fn, copy))
            return c
        lax.fori_loop(0, n_exp, body, 0)

    @pl.when(i == 0)
    def _():
        zero_ref[...] = jnp.zeros_like(zero_ref)
        for_each_pad_copy(lambda copy: copy.start())

    def issue(t, carry):
        for k in range(TOP_K):
            pltpu.make_async_copy(_row(src_ref, t), _row(dst_ref, pos_refs[k][t]), sem).start()
        return carry

    lax.fori_loop(0, tile, issue, 0)
    for k in range(TOP_K):
        pltpu.make_async_copy(src_ref, dst_ref.at[pl.ds(0, tile), :], sem).wait()

    @pl.when(i == 0)
    def _():
        for_each_pad_copy(lambda copy: copy.wait())


def _dispatch(hf, pos, padrow, padn, n_sorted):
    t, d = hf.shape
    tile = min(SCATTER_TILE, t)
    pad_bits = (EXPERT_TILE - 1).bit_length()
    smem = pl.BlockSpec(memory_space=pltpu.SMEM)
    return pl.pallas_call(
        functools.partial(_dispatch_kernel, pad_bits=pad_bits),
        grid=(t // tile,),
        in_specs=_slot_specs(t, tile, lambda i: i) + [
            smem, smem,
            pl.BlockSpec((tile, d), lambda i: (i, 0)),
        ],
        out_specs=pl.BlockSpec(memory_space=pl.ANY),
        out_shape=jax.ShapeDtypeStruct((n_sorted, d), _f32),
        scratch_shapes=[pltpu.VMEM((1 << (pad_bits - 1), d), _f32),
                        pltpu.SemaphoreType.DMA(()), pltpu.SemaphoreType.DMA(())],
        compiler_params=_params(("arbitrary",)),
        name="dispatch",
    )(*([pos] * TOP_K), padrow, padn, hf)


def _expert_kernel(te_ref, tbi_ref, tbo_ref, tv_ref, tnext_ref, trun_ref, x_ref, wg_hbm, wu_hbm, wd_hbm, o_ref,
                   wg_ref, wu_ref, wd_ref, wgb_ref, wub_ref, wdb_ref, sems, *, layer):
    i = pl.program_id(0)
    live = tv_ref[i] > 0
    slot = trun_ref[i] % 2

    def fetch(expert, s):
        return [pltpu.make_async_copy(src.at[layer, expert], dst.at[s], sems.at[s])
                for src, dst in ((wg_hbm, wg_ref), (wu_hbm, wu_ref), (wd_hbm, wd_ref))]

    @pl.when(i == 0)
    def _():
        for copy in fetch(te_ref[0], 0):
            copy.start()

    @pl.when(live & (tnext_ref[i] != NOT_FIRST))
    def _():
        @pl.when(tnext_ref[i] != NO_NEXT)
        def _():
            for copy in fetch(tnext_ref[i], 1 - slot):
                copy.start()

        for copy in fetch(te_ref[i], slot):
            copy.wait()
        wgb_ref[...] = wg_ref[slot].astype(_bf16)
        wub_ref[...] = wu_ref[slot].astype(_bf16)
        wdb_ref[...] = wd_ref[slot].astype(_bf16)

    @pl.when(live)
    def _():
        x = x_ref[...].astype(_bf16)
        g = _dot(x, wgb_ref[...])
        u = _dot(x, wub_ref[...])
        h = (g * jax.nn.sigmoid(g) * u).astype(_bf16)
        o_ref[...] = _dot(h, wdb_ref[...])

    @pl.when(jnp.logical_not(live))
    def _():
        o_ref[...] = jnp.zeros_like(o_ref)


def _experts(xs, w_gate, w_up, w_down, layer, tile_expert, tile_in, tile_out, tile_valid, tile_next, tile_run):
    _, n_exp, d, f = w_gate.shape
    tm = EXPERT_TILE
    n_tiles = tile_expert.shape[0]
    hbm = pl.BlockSpec(memory_space=pl.ANY)
    return pl.pallas_call(
        functools.partial(_expert_kernel, layer=layer),
        grid_spec=pltpu.PrefetchScalarGridSpec(
            num_scalar_prefetch=6,
            grid=(n_tiles,),
            in_specs=[pl.BlockSpec((tm, d), lambda i, te, tbi, tbo, tv, tn, tr: (tbi[i], 0)), hbm, hbm, hbm],
            out_specs=pl.BlockSpec((tm, d), lambda i, te, tbi, tbo, tv, tn, tr: (tbo[i], 0)),
            scratch_shapes=[pltpu.VMEM((2, d, f), _f32), pltpu.VMEM((2, d, f), _f32), pltpu.VMEM((2, f, d), _f32),
                            pltpu.VMEM((d, f), _bf16), pltpu.VMEM((d, f), _bf16), pltpu.VMEM((f, d), _bf16),
                            pltpu.SemaphoreType.DMA((2,))],
        ),
        out_shape=jax.ShapeDtypeStruct(xs.shape, _f32),
        compiler_params=_params(("arbitrary",)),
        name="experts",
    )(tile_expert, tile_in, tile_out, tile_valid, tile_next, tile_run, xs, w_gate, w_up, w_down)


def _combine_kernel(*refs, n_first):
    pos_refs, posn_refs = refs[:TOP_K], refs[TOP_K:2 * TOP_K]
    ys_ref, wt_ref, sh_ref, x1_ref, gpost_ref, mod_ref = refs[2 * TOP_K:2 * TOP_K + 6]
    rest = refs[2 * TOP_K + 6:]
    final = n_first is not None
    if final:
        yctx_ref, ylat_ref, buf0_ref, buf1_ref, f_ref, sems = rest
    else:
        gnext_ref, modn_ref, x2_ref, hn_ref, buf0_ref, buf1_ref, f_ref, sems = rest
    tm, d = x1_ref.shape
    i = pl.program_id(0)
    last = pl.num_programs(0) - 1

    def start_rows(p_refs, buf_ref, sem, t):
        for k in range(TOP_K):
            pltpu.make_async_copy(_row(ys_ref, p_refs[k][t]), buf_ref.at[k, pl.ds(t, 1), :], sem).start()

    def wait_tile(buf_ref, sem):
        for k in range(TOP_K):
            pltpu.make_async_copy(ys_ref.at[pl.ds(0, tm), :], buf_ref.at[k], sem).wait()

    @pl.when(i == 0)
    def _():
        def first(t, carry):
            start_rows(pos_refs, buf0_ref, sems.at[0], t)
            return carry
        lax.fori_loop(0, tm, first, 0)

    def step(cur_ref, cur_sem, nxt_ref, nxt_sem):
        wait_tile(cur_ref, cur_sem)

        def group(g, carry):
            for u in range(SUBLANES):
                start_rows(posn_refs, nxt_ref, nxt_sem, g * SUBLANES + u)
            rows = pl.ds(pl.multiple_of(g * SUBLANES, SUBLANES), SUBLANES)
            w = wt_ref[rows, :]
            wb = [jnp.broadcast_to(w[:, k:k + 1], (SUBLANES, LANES)) for k in range(TOP_K)]
            for j in range(d // LANES):
                cols = slice(j * LANES, (j + 1) * LANES)
                acc = sh_ref[rows, cols]
                for k in range(TOP_K):
                    acc = acc + wb[k] * cur_ref[k, rows, cols]
                f_ref[rows, cols] = acc
            return carry

        lax.fori_loop(0, tm // SUBLANES, group, 0)

        @pl.when(i == last)
        def _():
            wait_tile(nxt_ref, nxt_sem)

    pl.when(i % 2 == 0)(functools.partial(step, buf0_ref, sems.at[0], buf1_ref, sems.at[1]))
    pl.when(i % 2 == 1)(functools.partial(step, buf1_ref, sems.at[1], buf0_ref, sems.at[0]))
    f = f_ref[...]
    x2 = x1_ref[...] + mod_ref[pl.ds(MOD_GATE_FFN, 1), :] * _rms(f, gpost_ref[...])
    if final:
        in_ctx = pl.program_id(0) < n_first

        @pl.when(in_ctx)
        def _():
            yctx_ref[...] = x2

        @pl.when(jnp.logical_not(in_ctx))
        def _():
            ylat_ref[...] = x2
    else:
        x2_ref[...] = x2
        hn = (_rms(x2, gnext_ref[...]) * (1.0 + modn_ref[pl.ds(MOD_SCALE_MIX, 1), :])
              + modn_ref[pl.ds(MOD_SHIFT_MIX, 1), :])
        hn_ref[...] = hn.astype(hn_ref.dtype)


def _combine(ys, pos, wt, shared, x1, g_post, mod, n_ctx, dec_seq, g_next=None, mod_next=None):
    t, d = x1.shape
    tm = min(GATHER_TILE, t)
    n_steps = t // tm
    final = g_next is None
    row = lambda i: (i, 0)
    fixed = lambda i: (0, 0)
    cond = lambda i: (_cond_row(i * tm, n_ctx, dec_seq), 0, 0)
    in_specs = _slot_specs(t, tm, lambda i: i) + _slot_specs(t, tm, lambda i: jnp.minimum(i + 1, n_steps - 1)) + [
        pl.BlockSpec(memory_space=pl.ANY),
        pl.BlockSpec((tm, TOP_K), row),
        pl.BlockSpec((tm, d), row),
        pl.BlockSpec((tm, d), row),
        pl.BlockSpec((1, d), fixed),
        pl.BlockSpec((None, MOD_ROWS, d), cond),
    ]
    args = [pos] * (2 * TOP_K) + [ys, wt, shared, x1, g_post.reshape(1, d), mod]
    if final:
        n0, n1 = n_ctx // tm, (t - n_ctx) // tm
        out_specs = [pl.BlockSpec((tm, d), lambda i: (jnp.minimum(i, n0 - 1), 0)),
                     pl.BlockSpec((tm, d), lambda i: (jnp.clip(i - n0, 0, n1 - 1), 0))]
        out_shape = [jax.ShapeDtypeStruct((n_ctx, d), _f32), jax.ShapeDtypeStruct((t - n_ctx, d), _f32)]
    else:
        in_specs += [pl.BlockSpec((1, d), fixed), pl.BlockSpec((None, MOD_ROWS, d), cond)]
        args += [g_next.reshape(1, d), mod_next]
        out_specs = [pl.BlockSpec((tm, d), row), pl.BlockSpec((tm, d), row)]
        out_shape = [jax.ShapeDtypeStruct((t, d), _f32), jax.ShapeDtypeStruct((t, d), _bf16)]
    return pl.pallas_call(
        functools.partial(_combine_kernel, n_first=n_ctx // tm if final else None),
        grid=(n_steps,),
        in_specs=in_specs,
        out_specs=out_specs,
        out_shape=out_shape,
        scratch_shapes=[pltpu.VMEM((TOP_K, tm, d), _f32), pltpu.VMEM((TOP_K, tm, d), _f32), pltpu.VMEM((tm, d), _f32),
                        pltpu.SemaphoreType.DMA((2,))],
        compiler_params=_params(("arbitrary",)),
        name="combine",
    )(*args)


def _moe(hf, idx_t, wt_t, rank_t, cnt, layer, w_gate, w_up, w_down, w_sh_gate, w_sh_up, w_sh_down):
    n_exp, d = w_gate.shape[1:3]
    t = hf.shape[0]
    tm = EXPERT_TILE
    n_tiles = t * TOP_K // tm + n_exp
    pos, *tiles, padrow, padn = _plan(cnt[:, 0], idx_t, rank_t, n_tiles)
    pos = pos.reshape(-1)
    xs = _dispatch(hf, pos, padrow, padn, n_tiles * tm)
    ys = _experts(xs, w_gate, w_up, w_down, layer, *tiles)
    n_sh = t // tm
    every = jnp.arange(n_sh, dtype=jnp.int32)
    zeros = jnp.zeros((n_sh,), jnp.int32)
    shared = _experts(hf, w_sh_gate[:, None], w_sh_up[:, None], w_sh_down[:, None], layer,
                      zeros, every, every, jnp.full((n_sh,), tm, jnp.int32),
                      jnp.full((n_sh,), NOT_FIRST, jnp.int32).at[0].set(NO_NEXT), zeros)
    return ys, pos, wt_t.T, shared


def kernel(x_prompt, x_sample, cache_k, cache_v, c, c_ctx, w_ada, b_ada, norm_mix_pre, norm_mix_post, norm_ffn_pre, norm_ffn_post, w_qkv, w_o_attn, rpb, w_conv_in, conv_w, w_conv_out, w_router, b_router, w_exp_gate, w_exp_up, w_exp_down, w_sh_gate, w_sh_up, w_sh_down):
    batch, seq, d = x_prompt.shape
    dec_batch, dec_seq, _ = x_sample.shape
    depth = w_ada.shape[0]
    n_ctx, n_lat = batch * seq, dec_batch * dec_seq
    dh = d // N_HEADS
    past = cache_k.shape[2]

    x = [x_prompt.reshape(n_ctx, d), x_sample.reshape(n_lat, d)]
    cond = jnp.concatenate([c_ctx[None, :], c], axis=0)
    mod = _adaln(cond, w_ada, b_ada)

    new_k, new_v = [], []
    h = _modulate(x, norm_mix_pre[0], mod[0], n_ctx, dec_seq)
    for l in range(depth):
        if l % 2 == 0:
            a = l // 2
            qkv = _matmul(h, w_qkv, a)
            o_ctx, k_new, v_new = _ctx_attention(qkv, batch, seq, d)
            new_k.append(k_new.reshape(batch, seq, N_HEADS, dh))
            new_v.append(v_new.reshape(batch, seq, N_HEADS, dh))
            o_lat = _natten(qkv, n_ctx, cache_k[:, a].reshape(dec_batch, past, d),
                            cache_v[:, a].reshape(dec_batch, past, d), rpb[a], dec_batch, dec_seq, d)
            mixed, w_out, w_layer = [o_ctx, o_lat], w_o_attn, a
        else:
            m = l // 2
            mixed = [_conv_gate(_matmul(h, w_conv_in, m), conv_w[m], n_ctx, seq, dec_seq, d)]
            w_out, w_layer = w_conv_out, m
        x1, hf, logits = _post_mixer(
            mixed, w_out, w_layer, x, norm_mix_post[l], mod[l], norm_ffn_pre[l], w_router[l], n_ctx, dec_seq)
        idx_t, wt_t, rank_t, cnt = _router(logits, b_router[l])
        ys, pos, wt, shared = _moe(hf, idx_t, wt_t, rank_t, cnt, l, w_exp_gate, w_exp_up, w_exp_down,
                                   w_sh_gate, w_sh_up, w_sh_down)
        if l + 1 < depth:
            x2, h = _combine(ys, pos, wt, shared, x1, norm_ffn_post[l], mod[l], n_ctx, dec_seq,
                             norm_mix_pre[l + 1], mod[l + 1])
            x = [x2]
        else:
            y_ctx, y_lat = _combine(ys, pos, wt, shared, x1, norm_ffn_post[l], mod[l], n_ctx, dec_seq)

    return (y_ctx.reshape(batch, seq, d), y_lat.reshape(dec_batch, dec_seq, d),
            jnp.stack(new_k, axis=1), jnp.stack(new_v, axis=1))
```

```python
import functools

import jax
import jax.numpy as jnp
from jax import lax
from jax.experimental import pallas as pl
from jax.experimental.pallas import tpu as pltpu

N_HEADS = 16
GRID_W = 64
WIN_ROWS = 8
WIN_COLS = 16
CONV_WIDTH = 3
N_EXPERTS = 64
TOP_K = 8
N_GROUPS = 8
TOPK_GROUPS = 4
ROUTED_SCALE = 2.5
N_MOD = 6
RMS_EPS = 1e-6
NEG_INF = -1e30

LANES = 128
SUBLANES = 8
VMEM_LIMIT = 56 * 1024 * 1024

MOD_SHIFT_MIX, MOD_SCALE_MIX, MOD_GATE_MIX, MOD_SHIFT_FFN, MOD_SCALE_FFN, MOD_GATE_FFN = range(6)
MOD_ROWS = 8

ROW_TILE = 256
ROUTER_TILE = 1024
EXPERT_TILE = 256
MM_TILE_M = 2048
MM_TILE_N = 512
GATHER_TILE = 128
SCATTER_TILE = 1024

_f32 = jnp.float32
_bf16 = jnp.bfloat16


def _params(sem, vmem=VMEM_LIMIT):
    return pltpu.CompilerParams(dimension_semantics=sem, vmem_limit_bytes=vmem)


def _rms(x, g):
    return x * lax.rsqrt(jnp.mean(x * x, axis=-1, keepdims=True) + RMS_EPS) * g


def _dot(a, b):
    return jnp.dot(a, b, preferred_element_type=_f32)


def _dot_nt(a, b):
    return lax.dot_general(a, b, (((1,), (1,)), ((), ())), preferred_element_type=_f32)


def _cond_row(row0, n_ctx, dec_seq):
    return jnp.where(row0 < n_ctx, 0, 1 + (row0 - n_ctx) // dec_seq)


def _adaln_kernel(cb_ref, w_ref, b_ref, o_ref, acc_ref, *, n_cond):
    k = pl.program_id(2)

    @pl.when(k == 0)
    def _():
        acc_ref[...] = jnp.zeros_like(acc_ref)

    tk, tn = w_ref.shape
    kb = min(LANES, tk)
    for k0 in range(0, tk, kb):
        s = []
        for r in range(n_cond):
            v = cb_ref[r, k0:k0 + kb, :]
            s.append(v * jax.nn.sigmoid(v))
        for c in range(tn // LANES):
            cols = slice(c * LANES, (c + 1) * LANES)
            w = w_ref[k0:k0 + kb, cols]
            for r in range(n_cond):
                acc_ref[r, :, cols] += (w * s[r]).reshape(kb // SUBLANES, SUBLANES, LANES).sum(axis=0)

    @pl.when(k == pl.num_programs(2) - 1)
    def _():
        o_ref[...] = jnp.zeros_like(o_ref)
        for r in range(n_cond):
            o_ref[pl.ds(r, 1), :] = acc_ref[r].sum(axis=0, keepdims=True) + b_ref[...]


def _adaln(cond, w_ada, b_ada):
    n_cond, d = cond.shape
    n_layers, _, n6 = w_ada.shape
    tk, tn = min(1024, d), min(2048, n6)
    cb = jnp.broadcast_to(cond[:, :, None], (n_cond, d, LANES))
    mod = pl.pallas_call(
        functools.partial(_adaln_kernel, n_cond=n_cond),
        grid=(n_layers, n6 // tn, d // tk),
        in_specs=[
            pl.BlockSpec((n_cond, tk, LANES), lambda l, n, k: (0, k, 0)),
            pl.BlockSpec((None, tk, tn), lambda l, n, k: (l, k, n)),
            pl.BlockSpec((None, 1, tn), lambda l, n, k: (l, 0, n)),
        ],
        out_specs=pl.BlockSpec((None, MOD_ROWS, tn), lambda l, n, k: (l, 0, n)),
        out_shape=jax.ShapeDtypeStruct((n_layers, MOD_ROWS, n6), _f32),
        scratch_shapes=[pltpu.VMEM((n_cond, SUBLANES, tn), _f32)],
        compiler_params=_params(("arbitrary", "arbitrary", "arbitrary")),
        name="adaln",
    )(cb, w_ada, b_ada.reshape(n_layers, 1, n6))
    mod = mod[:, :n_cond].reshape(n_layers, n_cond, N_MOD, d)
    return jnp.pad(mod, ((0, 0), (0, 0), (0, MOD_ROWS - N_MOD), (0, 0)))


def _split_specs(parts, tm, width):
    if len(parts) == 1:
        return [pl.BlockSpec((tm, width), lambda i: (i, 0))]
    n0 = parts[0].shape[0] // tm
    n1 = parts[1].shape[0] // tm
    return [pl.BlockSpec((tm, width), lambda i: (jnp.minimum(i, n0 - 1), 0)),
            pl.BlockSpec((tm, width), lambda i: (jnp.clip(i - n0, 0, n1 - 1), 0))]


def _split_read(refs, n_first):
    if len(refs) == 1:
        return refs[0][...]
    return jnp.where(pl.program_id(0) < n_first, refs[0][...], refs[1][...])


def _modulate_kernel(*refs, n_x, n_first):
    x_refs, (g_ref, mod_ref, o_ref) = refs[:n_x], refs[n_x:]
    y = _rms(_split_read(x_refs, n_first), g_ref[...])
    h = y * (1.0 + mod_ref[pl.ds(MOD_SCALE_MIX, 1), :]) + mod_ref[pl.ds(MOD_SHIFT_MIX, 1), :]
    o_ref[...] = h.astype(o_ref.dtype)


def _modulate(xs, g, mod, n_ctx, dec_seq):
    t = sum(x.shape[0] for x in xs)
    d = xs[0].shape[1]
    tm = ROW_TILE
    return pl.pallas_call(
        functools.partial(_modulate_kernel, n_x=len(xs), n_first=xs[0].shape[0] // tm),
        grid=(t // tm,),
        in_specs=_split_specs(xs, tm, d) + [
            pl.BlockSpec((1, d), lambda i: (0, 0)),
            pl.BlockSpec((None, MOD_ROWS, d), lambda i: (_cond_row(i * tm, n_ctx, dec_seq), 0, 0)),
        ],
        out_specs=pl.BlockSpec((tm, d), lambda i: (i, 0)),
        out_shape=jax.ShapeDtypeStruct((t, d), _bf16),
        compiler_params=_params(("parallel",)),
        name="modulate",
    )(*xs, g.reshape(1, d), mod)


def _matmul_kernel(a_ref, w_ref, o_ref):
    o_ref[...] = _dot(a_ref[...], w_ref[...].astype(_bf16)).astype(o_ref.dtype)


def _matmul(a, w, layer):
    n_rows = a.shape[0]
    _, k, n = w.shape
    tm, tn = min(MM_TILE_M, n_rows), min(MM_TILE_N, n)
    return pl.pallas_call(
        _matmul_kernel,
        grid=(n_rows // tm, n // tn),
        in_specs=[
            pl.BlockSpec((tm, k), lambda i, j: (i, 0)),
            pl.BlockSpec((None, k, tn), lambda i, j: (layer, 0, j)),
        ],
        out_specs=pl.BlockSpec((tm, tn), lambda i, j: (i, j)),
        out_shape=jax.ShapeDtypeStruct((n_rows, n), _f32),
        compiler_params=_params(("parallel", "parallel")),
        name="matmul",
    )(a, w)


def _ctx_attn_kernel(q_ref, k_ref, v_ref, o_ref, kout_ref, vout_ref, *, n_heads, scale):
    seq = q_ref.shape[0]
    dh = q_ref.shape[1] // n_heads
    for h in range(n_heads):
        sl = slice(h * dh, (h + 1) * dh)
        k32 = k_ref[:, sl]
        v32 = v_ref[:, sl]
        kout_ref[pl.ds(h, seq, stride=n_heads), :] = k32
        vout_ref[pl.ds(h, seq, stride=n_heads), :] = v32
        q = q_ref[:, sl].astype(_bf16)
        s = _dot_nt(q, k32.astype(_bf16)) * scale
        p = jnp.exp(s - s.max(axis=-1, keepdims=True))
        o = _dot(p.astype(_bf16), v32.astype(_bf16)) / p.sum(axis=-1, keepdims=True)
        o_ref[:, sl] = o.astype(o_ref.dtype)


def _ctx_attention(qkv, batch, seq, d):
    dh = d // N_HEADS
    assert dh == LANES
    cache = jax.ShapeDtypeStruct((batch * seq * N_HEADS, dh), _f32)
    return pl.pallas_call(
        functools.partial(_ctx_attn_kernel, n_heads=N_HEADS, scale=dh ** -0.5),
        grid=(batch,),
        in_specs=[
            pl.BlockSpec((seq, d), lambda b: (b, 0)),
            pl.BlockSpec((seq, d), lambda b: (b, 1)),
            pl.BlockSpec((seq, d), lambda b: (b, 2)),
        ],
        out_specs=[
            pl.BlockSpec((seq, d), lambda b: (b, 0)),
            pl.BlockSpec((seq * N_HEADS, dh), lambda b: (b, 0)),
            pl.BlockSpec((seq * N_HEADS, dh), lambda b: (b, 0)),
        ],
        out_shape=[jax.ShapeDtypeStruct((batch * seq, d), _bf16), cache, cache],
        compiler_params=_params(("parallel",)),
        name="ctx_attention",
    )(qkv, qkv, qkv)


def _natten_kernel(q_ref, k_ref, v_ref, kc_ref, vc_ref, rpb_ref, o_ref, qb_ref, kb_ref, vb_ref,
                   sctx_ref, pctx_ref, oloc_ref, den_ref, blo_ref, bhi_ref, *, rows, width, kh, scale, unroll):
    n_loc = kh * width
    q_col = lax.broadcasted_iota(jnp.int32, (width, n_loc), 0)
    k_col = lax.broadcasted_iota(jnp.int32, (width, n_loc), 1) % width
    col_start = jnp.clip(q_col - WIN_COLS // 2, 0, width - WIN_COLS)
    col_mask = (k_col >= col_start) & (k_col < col_start + WIN_COLS)
    low_half = lax.broadcasted_iota(jnp.int32, (width, LANES), 1) < width

    for dr in range(rpb_ref.shape[0]):
        row = jnp.broadcast_to(rpb_ref[pl.ds(dr, 1), :], (width, LANES))
        blo_ref[dr] = pltpu.roll(row, LANES - (WIN_COLS - 1), axis=1, stride=1, stride_axis=0)
        bhi_ref[dr] = pltpu.roll(row, (LANES - (WIN_COLS - 1) + width) % LANES, axis=1, stride=1, stride_axis=0)

    def bias_of(delta):
        cols = []
        for i in range(0, kh, 2):
            dr = i - delta + (WIN_ROWS - 1)
            cols.append(jnp.where(low_half, blo_ref[dr], bhi_ref[dr + 1]))
        return jnp.concatenate(cols, axis=1)

    qb_ref[...] = q_ref[...].astype(_bf16)
    kb_ref[...] = k_ref[...].astype(_bf16)
    vb_ref[...] = v_ref[...].astype(_bf16)
    sctx_ref[...] = _dot_nt(qb_ref[...], kc_ref[...].astype(_bf16)) * scale

    def one_row(r):
        r0 = jnp.clip(r - kh // 2, 0, rows - kh)
        qrows = pl.ds(pl.multiple_of(r * width, width), width)
        win = pl.ds(pl.multiple_of(r0 * width, width), n_loc)
        s_loc = _dot_nt(qb_ref[qrows, :], kb_ref[win, :]) * scale + bias_of(r - r0)
        s_loc = jnp.where(col_mask, s_loc, NEG_INF)
        s_ctx = sctx_ref[qrows, :]
        m = jnp.maximum(s_loc.max(axis=-1, keepdims=True), s_ctx.max(axis=-1, keepdims=True))
        p_loc = jnp.exp(s_loc - m)
        p_ctx = jnp.exp(s_ctx - m)
        den = p_loc.sum(axis=-1, keepdims=True) + p_ctx.sum(axis=-1, keepdims=True)
        pctx_ref[qrows, :] = p_ctx.astype(_bf16)
        oloc_ref[qrows, :] = _dot(p_loc.astype(_bf16), vb_ref[win, :])
        den_ref[qrows, :] = jnp.broadcast_to(den, (width, den_ref.shape[1]))

    def some_rows(g, carry):
        for u in range(unroll):
            one_row(g * unroll + u)
        return carry

    lax.fori_loop(0, rows // unroll, some_rows, 0)
    o = oloc_ref[...] + _dot(pctx_ref[...], vc_ref[...].astype(_bf16))
    o_ref[...] = (o / den_ref[...]).astype(o_ref.dtype)


def _natten(qkv, n_ctx, k_ctx, v_ctx, rpb, dec_batch, dec_seq, d):
    assert n_ctx % dec_seq == 0
    dh = d // N_HEADS
    rows = dec_seq // GRID_W
    kh = min(WIN_ROWS, rows)
    assert 2 * GRID_W == LANES and kh % 2 == 0 and 2 * WIN_COLS - 1 <= LANES
    past = k_ctx.shape[1]
    n_dr = rpb.shape[1]
    rpb = jnp.pad(rpb.astype(_f32), ((0, 0), (0, 0), (0, LANES - rpb.shape[2])))
    b0 = n_ctx // dec_seq
    unroll = next(u for u in (8, 4, 2, 1) if rows % u == 0)
    return pl.pallas_call(
        functools.partial(_natten_kernel, rows=rows, width=GRID_W, kh=kh, scale=dh ** -0.5, unroll=unroll),
        grid=(dec_batch, N_HEADS),
        in_specs=[
            pl.BlockSpec((dec_seq, dh), lambda b, h: (b0 + b, h)),
            pl.BlockSpec((dec_seq, dh), lambda b, h: (b0 + b, N_HEADS + h)),
            pl.BlockSpec((dec_seq, dh), lambda b, h: (b0 + b, 2 * N_HEADS + h)),
            pl.BlockSpec((None, past, dh), lambda b, h: (b, 0, h)),
            pl.BlockSpec((None, past, dh), lambda b, h: (b, 0, h)),
            pl.BlockSpec((None, n_dr, LANES), lambda b, h: (h, 0, 0)),
        ],
        out_specs=pl.BlockSpec((dec_seq, dh), lambda b, h: (b, h)),
        out_shape=jax.ShapeDtypeStruct((dec_batch * dec_seq, d), _bf16),
        scratch_shapes=[pltpu.VMEM((dec_seq, dh), _bf16), pltpu.VMEM((dec_seq, dh), _bf16),
                        pltpu.VMEM((dec_seq, dh), _bf16),
                        pltpu.VMEM((dec_seq, past), _f32), pltpu.VMEM((dec_seq, past), _bf16),
                        pltpu.VMEM((dec_seq, dh), _f32), pltpu.VMEM((dec_seq, dh), _f32),
                        pltpu.VMEM((n_dr, GRID_W, LANES), _f32), pltpu.VMEM((n_dr, GRID_W, LANES), _f32)],
        compiler_params=_params(("parallel", "parallel")),
        name="natten",
    )(qkv, qkv, qkv, k_ctx, v_ctx, rpb)


def _conv_gate_kernel(b_ref, c_ref, u_ref, cp_ref, up_ref, cn_ref, un_ref, w_ref, o_ref,
                      *, n_ctx, seq, dec_seq):
    tm = o_ref.shape[0]
    row0 = pl.program_id(0) * tm
    in_ctx = row0 < n_ctx
    pos0 = jnp.where(in_ctx, row0 % seq, (row0 - n_ctx) % dec_seq)
    seq_len = jnp.where(in_ctx, seq, dec_seq)
    has_prev = pos0 > 0
    has_next = pos0 + tm < seq_len
    cu = c_ref[...] * u_ref[...]
    prev_row = jnp.where(has_prev, cp_ref[pl.ds(SUBLANES - 1, 1), :] * up_ref[pl.ds(SUBLANES - 1, 1), :], 0.0)
    next_row = jnp.where(has_next, cn_ref[pl.ds(0, 1), :] * un_ref[pl.ds(0, 1), :], 0.0)
    ridx = lax.broadcasted_iota(jnp.int32, cu.shape, 0)
    before = jnp.where(ridx == 0, prev_row, pltpu.roll(cu, 1, axis=0))
    after = jnp.where(ridx == tm - 1, next_row, pltpu.roll(cu, tm - 1, axis=0))
    conv = w_ref[pl.ds(0, 1), :] * before + w_ref[pl.ds(1, 1), :] * cu + w_ref[pl.ds(2, 1), :] * after
    o_ref[...] = (b_ref[...] * conv).astype(o_ref.dtype)


def _conv_gate(bcu, conv_w, n_ctx, seq, dec_seq, d):
    t = bcu.shape[0]
    tm = ROW_TILE
    halo = tm // SUBLANES
    last = t // SUBLANES - 1
    cw = jnp.pad(conv_w, ((0, SUBLANES - CONV_WIDTH), (0, 0)))
    prev_map = lambda col: (lambda i: (jnp.maximum(i * halo - 1, 0), col))
    next_map = lambda col: (lambda i: (jnp.minimum((i + 1) * halo, last), col))
    return pl.pallas_call(
        functools.partial(_conv_gate_kernel, n_ctx=n_ctx, seq=seq, dec_seq=dec_seq),
        grid=(t // tm,),
        in_specs=[
            pl.BlockSpec((tm, d), lambda i: (i, 0)),
            pl.BlockSpec((tm, d), lambda i: (i, 1)),
            pl.BlockSpec((tm, d), lambda i: (i, 2)),
            pl.BlockSpec((SUBLANES, d), prev_map(1)),
            pl.BlockSpec((SUBLANES, d), prev_map(2)),
            pl.BlockSpec((SUBLANES, d), next_map(1)),
            pl.BlockSpec((SUBLANES, d), next_map(2)),
            pl.BlockSpec((SUBLANES, d), lambda i: (0, 0)),
        ],
        out_specs=pl.BlockSpec((tm, d), lambda i: (i, 0)),
        out_shape=jax.ShapeDtypeStruct((t, d), _bf16),
        compiler_params=_params(("parallel",)),
        name="conv_gate",
    )(bcu, bcu, bcu, bcu, bcu, bcu, bcu, cw)


def _route(sel, scores):
    n_grp, eg, tm = sel.shape
    n_exp = n_grp * eg
    j_iota = lax.broadcasted_iota(jnp.int32, sel.shape, 1)
    m1 = sel.max(axis=1, keepdims=True)
    j1 = jnp.min(jnp.where(sel == m1, j_iota, eg), axis=1, keepdims=True)
    m2 = jnp.max(jnp.where(j_iota == j1, -jnp.inf, sel), axis=1, keepdims=True)
    grp = m1 + m2
    g_iota = lax.broadcasted_iota(jnp.int32, grp.shape, 0)
    g_sel = g_iota < 0
    for _ in range(TOPK_GROUPS):
        gm = grp.max(axis=0, keepdims=True)
        gi = jnp.min(jnp.where(grp == gm, g_iota, n_grp), axis=0, keepdims=True)
        hit = g_iota == gi
        g_sel = g_sel | hit
        grp = jnp.where(hit, -jnp.inf, grp)
    cur = jnp.where(jnp.broadcast_to(g_sel, sel.shape), sel, NEG_INF)
    e_iota = lax.broadcasted_iota(jnp.int32, sel.shape, 0) * eg + j_iota
    ids, ws, hits = [], [], []
    for _ in range(TOP_K):
        m = cur.max(axis=1, keepdims=True).max(axis=0, keepdims=True)
        ei = jnp.min(jnp.where(cur == m, e_iota, n_exp), axis=1, keepdims=True).min(axis=0, keepdims=True)
        hit = e_iota == ei
        ids.append(ei)
        ws.append(jnp.sum(jnp.where(hit, scores, 0.0), axis=1, keepdims=True).sum(axis=0, keepdims=True))
        hits.append(hit)
        cur = jnp.where(hit, -jnp.inf, cur)
    total = functools.reduce(lambda a, b: a + b, ws)
    ws = [w / total * ROUTED_SCALE for w in ws]
    return ids, ws, hits


def _split_bf16(x):
    hi = x.astype(_bf16)
    return hi, (x - hi.astype(_f32)).astype(_bf16)


def _post_mixer_kernel(*refs, n_a, n_x, n_first):
    a_refs, x_refs = refs[:n_a], refs[n_a:n_a + n_x]
    wo_ref, gpost_ref, mod_ref, gpre_ref, wr_ref, x1_ref, hf_ref, logit_ref = refs[n_a + n_x:]
    o = _dot(_split_read(a_refs, n_first), wo_ref[...])
    x1 = _split_read(x_refs, n_first) + mod_ref[pl.ds(MOD_GATE_MIX, 1), :] * _rms(o, gpost_ref[...])
    x1_ref[...] = x1
    hf = _rms(x1, gpre_ref[...]) * (1.0 + mod_ref[pl.ds(MOD_SCALE_FFN, 1), :]) + mod_ref[pl.ds(MOD_SHIFT_FFN, 1), :]
    hf_ref[...] = hf
    h_hi, h_lo = _split_bf16(hf)
    w_hi, w_lo = _split_bf16(wr_ref[...])
    logit_ref[...] = _dot(h_hi, w_hi) + (_dot(h_hi, w_lo) + _dot(h_lo, w_hi))


def _post_mixer(a_parts, w_out, layer, x_parts, g_post, mod, g_pre, w_router, n_ctx, dec_seq):
    t = sum(x.shape[0] for x in x_parts)
    d = x_parts[0].shape[1]
    tm = ROW_TILE
    e = w_router.shape[1]
    assert e <= LANES
    row = lambda i: (i, 0)
    fixed = lambda i: (0, 0)
    return pl.pallas_call(
        functools.partial(_post_mixer_kernel, n_a=len(a_parts), n_x=len(x_parts), n_first=n_ctx // tm),
        grid=(t // tm,),
        in_specs=_split_specs(a_parts, tm, d) + _split_specs(x_parts, tm, d) + [
            pl.BlockSpec((None, d, d), lambda i: (layer, 0, 0)),
            pl.BlockSpec((1, d), fixed),
            pl.BlockSpec((None, MOD_ROWS, d), lambda i: (_cond_row(i * tm, n_ctx, dec_seq), 0, 0)),
            pl.BlockSpec((1, d), fixed),
            pl.BlockSpec((d, LANES), fixed),
        ],
        out_specs=[pl.BlockSpec((tm, d), row), pl.BlockSpec((tm, d), row), pl.BlockSpec((tm, LANES), row)],
        out_shape=[jax.ShapeDtypeStruct((t, d), _f32), jax.ShapeDtypeStruct((t, d), _f32),
                   jax.ShapeDtypeStruct((t, LANES), _f32)],
        compiler_params=_params(("parallel",)),
        name="post_mixer",
    )(*a_parts, *x_parts, w_out.astype(_bf16), g_post.reshape(1, d), mod, g_pre.reshape(1, d),
      jnp.pad(w_router, ((0, 0), (0, LANES - e))))


def _router_kernel(logit_ref, br_ref, idx_ref, wt_ref, rank_ref, cnt_ref, carry_ref, before_ref):
    i = pl.program_id(0)
    tm = logit_ref.shape[0]
    n_exp = br_ref.shape[0]

    @pl.when(i == 0)
    def _():
        carry_ref[...] = jnp.zeros_like(carry_ref)
        t_src = lax.broadcasted_iota(jnp.int32, (tm, tm), 0)
        t_dst = lax.broadcasted_iota(jnp.int32, (tm, tm), 1)
        before_ref[...] = jnp.where(t_src < t_dst, 1.0, 0.0).astype(_bf16)

    logits = logit_ref[...].T[:n_exp]
    grouped = (N_GROUPS, n_exp // N_GROUPS, tm)
    scores = jax.nn.sigmoid(logits)
    ids, ws, hits = _route((scores + br_ref[...]).reshape(grouped), scores.reshape(grouped))
    for k in range(TOP_K):
        idx_ref[pl.ds(k, 1), :] = ids[k][0]
        wt_ref[pl.ds(k, 1), :] = ws[k][0]

    any_hit = functools.reduce(lambda a, b: a | b, hits)
    mask = jnp.where(any_hit, 1.0, 0.0).reshape(n_exp, tm).astype(_bf16)
    rank = _dot(mask, before_ref[...]) + jnp.concatenate([carry_ref[...]] * (tm // LANES), axis=1)
    rank = rank.reshape(grouped)
    for k in range(TOP_K):
        rk = jnp.sum(jnp.where(hits[k], rank, 0.0), axis=1, keepdims=True).sum(axis=0, keepdims=True)
        rank_ref[pl.ds(k, 1), :] = rk[0].astype(jnp.int32)
    carry_ref[...] += _dot(mask, jnp.ones((tm, LANES), _bf16))
    cnt_ref[...] = carry_ref[...].astype(jnp.int32)


def _router(logits, b_router):
    t = logits.shape[0]
    e = b_router.shape[0]
    tm = min(ROUTER_TILE, t)
    col = lambda i: (0, i)
    fixed = lambda i: (0, 0)
    slots_i = jax.ShapeDtypeStruct((TOP_K, t), jnp.int32)
    return pl.pallas_call(
        _router_kernel,
        grid=(t // tm,),
        in_specs=[pl.BlockSpec((tm, LANES), lambda i: (i, 0)), pl.BlockSpec((e, 1), fixed)],
        out_specs=[pl.BlockSpec((TOP_K, tm), col), pl.BlockSpec((TOP_K, tm), col), pl.BlockSpec((TOP_K, tm), col),
                   pl.BlockSpec((e, LANES), fixed)],
        out_shape=[slots_i, jax.ShapeDtypeStruct((TOP_K, t), _f32), slots_i,
                   jax.ShapeDtypeStruct((e, LANES), jnp.int32)],
        scratch_shapes=[pltpu.VMEM((e, LANES), _f32), pltpu.VMEM((tm, tm), _bf16)],
        compiler_params=_params(("arbitrary",)),
        name="router",
    )(logits, b_router.reshape(e, 1))


def _row(ref, r):
    return ref.at[pl.ds(r, 1), :]


NOT_FIRST = -2
NO_NEXT = -1


def _plan_kernel(cnt_ref, idx_ref, rank_ref, pos_ref, te_ref, tbi_ref, tbo_ref, tv_ref, tnext_ref, trun_ref,
                 padrow_ref, padn_ref, offs_ref, *, tm):
    n_exp = cnt_ref.shape[0]
    n_tiles = te_ref.shape[0]

    def per_expert(e, carry):
        off, tile, run, prev_first = carry
        cnt = cnt_ref[e]
        n_t = (cnt + tm - 1) // tm
        offs_ref[e] = off
        padrow_ref[e] = off + cnt
        padn_ref[e] = n_t * tm - cnt

        def per_tile(j, c):
            te_ref[tile + j] = e
            tbi_ref[tile + j] = tile + j
            tbo_ref[tile + j] = tile + j
            tv_ref[tile + j] = jnp.minimum(cnt - j * tm, tm)
            tnext_ref[tile + j] = jnp.where(j == 0, NO_NEXT, NOT_FIRST)
            trun_ref[tile + j] = run
            return c

        lax.fori_loop(0, n_t, per_tile, 0)
        used = n_t > 0

        @pl.when(used & (prev_first >= 0))
        def _():
            tnext_ref[prev_first] = e

        return (off + n_t * tm, tile + n_t, run + used.astype(jnp.int32), jnp.where(used, tile, prev_first))

    zero = jnp.int32(0)
    _, live, _, _ = lax.fori_loop(0, n_exp, per_expert, (zero, zero, zero, jnp.int32(-1)))
    last = jnp.maximum(live - 1, 0)
    last_expert = te_ref[last]

    def dead_tile(i, c):
        te_ref[i] = last_expert
        tbi_ref[i] = last
        tbo_ref[i] = i
        tv_ref[i] = 0
        tnext_ref[i] = NOT_FIRST
        trun_ref[i] = 0
        return c

    lax.fori_loop(live, n_tiles, dead_tile, 0)

    idx = idx_ref[...]
    pos = rank_ref[...]
    for e in range(n_exp):
        pos = pos + jnp.where(idx == e, offs_ref[e], 0)
    pos_ref[...] = pos


def _plan(counts, idx_t, rank_t, n_tiles):
    n_exp = counts.shape[0]
    smem = pl.BlockSpec(memory_space=pltpu.SMEM)
    vmem = pl.BlockSpec(memory_space=pltpu.VMEM)
    tiles = jax.ShapeDtypeStruct((n_tiles,), jnp.int32)
    experts = jax.ShapeDtypeStruct((n_exp,), jnp.int32)
    return pl.pallas_call(
        functools.partial(_plan_kernel, tm=EXPERT_TILE),
        in_specs=[smem, vmem, vmem],
        out_specs=[vmem] + [smem] * 8,
        out_shape=[jax.ShapeDtypeStruct(idx_t.shape, jnp.int32)] + [tiles] * 6 + [experts] * 2,
        scratch_shapes=[pltpu.SMEM((n_exp,), jnp.int32)],
        name="plan",
    )(counts, idx_t, rank_t)


def _pad_copies(start, n, zero_ref, dst_ref, sem, pad_bits):
    single = n & (SUBLANES - 1)
    for s in range(SUBLANES - 1):
        yield s < single, pltpu.make_async_copy(_row(zero_ref, 0), _row(dst_ref, start + s), sem)
    base = pl.multiple_of(start + single, SUBLANES)
    groups = n // SUBLANES
    for b in range(pad_bits - (SUBLANES.bit_length() - 1)):
        rows = SUBLANES << b
        first = pl.multiple_of(base + ((groups >> (b + 1)) << (b + 1)) * SUBLANES, SUBLANES)
        copy = pltpu.make_async_copy(zero_ref.at[pl.ds(0, rows), :], dst_ref.at[pl.ds(first, rows), :], sem)
        yield ((groups >> b) & 1) == 1, copy


def _slot_specs(n_tokens, tile, step_of):
    per_k = n_tokens // tile
    return [pl.BlockSpec((tile,), lambda i, k=k: (k * per_k + step_of(i),), memory_space=pltpu.SMEM)
            for k in range(TOP_K)]


def _dispatch_kernel(*refs, pad_bits):
    pos_refs = refs[:TOP_K]
    padrow_ref, padn_ref, src_ref, dst_ref, zero_ref, sem, zero_sem = refs[TOP_K:]
    i = pl.program_id(0)
    tile = src_ref.shape[0]
    n_exp = padn_ref.shape[0]

    def for_each_pad_copy(fn):
        def body(e, c):
            for needed, copy in _pad_copies(padrow_ref[e], padn_ref[e], zero_ref, dst_ref, zero_sem, pad_bits):
                pl.when(needed)(functools.partial(fn, copy))
            return c
        lax.fori_loop(0, n_exp, body, 0)

    @pl.when(i == 0)
    def _():
        zero_ref[...] = jnp.zeros_like(zero_ref)
        for_each_pad_copy(lambda copy: copy.start())

    def issue(g, carry):
        for t in (2 * g, 2 * g + 1):
            for k in range(TOP_K):
                pltpu.make_async_copy(_row(src_ref, t), _row(dst_ref, pos_refs[k][t]), sem).start()
        return carry

    lax.fori_loop(0, tile // 2, issue, 0)
    for k in range(TOP_K):
        pltpu.make_async_copy(src_ref, dst_ref.at[pl.ds(0, tile), :], sem).wait()

    @pl.when(i == 0)
    def _():
        for_each_pad_copy(lambda copy: copy.wait())


def _dispatch(hf, pos, padrow, padn, n_sorted):
    t, d = hf.shape
    tile = min(SCATTER_TILE, t)
    pad_bits = (EXPERT_TILE - 1).bit_length()
    smem = pl.BlockSpec(memory_space=pltpu.SMEM)
    return pl.pallas_call(
        functools.partial(_dispatch_kernel, pad_bits=pad_bits),
        grid=(t // tile,),
        in_specs=_slot_specs(t, tile, lambda i: i) + [
            smem, smem,
            pl.BlockSpec((tile, d), lambda i: (i, 0)),
        ],
        out_specs=pl.BlockSpec(memory_space=pl.ANY),
        out_shape=jax.ShapeDtypeStruct((n_sorted, d), _f32),
        scratch_shapes=[pltpu.VMEM((1 << (pad_bits - 1), d), _f32),
                        pltpu.SemaphoreType.DMA(()), pltpu.SemaphoreType.DMA(())],
        compiler_params=_params(("arbitrary",)),
        name="dispatch",
    )(*([pos] * TOP_K), padrow, padn, hf)


def _expert_kernel(te_ref, tbi_ref, tbo_ref, tv_ref, tnext_ref, trun_ref, x_ref, wg_hbm, wu_hbm, wd_hbm, o_ref,
                   wg_ref, wu_ref, wd_ref, wgb_ref, wub_ref, wdb_ref, sems, *, layer):
    i = pl.program_id(0)
    live = tv_ref[i] > 0
    slot = trun_ref[i] % 2

    def fetch(expert, s):
        return [pltpu.make_async_copy(src.at[layer, expert], dst.at[s], sems.at[s])
                for src, dst in ((wg_hbm, wg_ref), (wu_hbm, wu_ref), (wd_hbm, wd_ref))]

    @pl.when(i == 0)
    def _():
        for copy in fetch(te_ref[0], 0):
            copy.start()

    @pl.when(live & (tnext_ref[i] != NOT_FIRST))
    def _():
        @pl.when(tnext_ref[i] != NO_NEXT)
        def _():
            for copy in fetch(tnext_ref[i], 1 - slot):
                copy.start()

        for copy in fetch(te_ref[i], slot):
            copy.wait()
        wgb_ref[...] = wg_ref[slot].astype(_bf16)
        wub_ref[...] = wu_ref[slot].astype(_bf16)
        wdb_ref[...] = wd_ref[slot].astype(_bf16)

    @pl.when(live)
    def _():
        x = x_ref[...].astype(_bf16)
        g = _dot(x, wgb_ref[...])
        u = _dot(x, wub_ref[...])
        h = (g * jax.nn.sigmoid(g) * u).astype(_bf16)
        o_ref[...] = _dot(h, wdb_ref[...])

    @pl.when(jnp.logical_not(live))
    def _():
        o_ref[...] = jnp.zeros_like(o_ref)


def _experts(xs, w_gate, w_up, w_down, layer, tile_expert, tile_in, tile_out, tile_valid, tile_next, tile_run):
    _, n_exp, d, f = w_gate.shape
    tm = EXPERT_TILE
    n_tiles = tile_expert.shape[0]
    hbm = pl.BlockSpec(memory_space=pl.ANY)
    return pl.pallas_call(
        functools.partial(_expert_kernel, layer=layer),
        grid_spec=pltpu.PrefetchScalarGridSpec(
            num_scalar_prefetch=6,
            grid=(n_tiles,),
            in_specs=[pl.BlockSpec((tm, d), lambda i, te, tbi, tbo, tv, tn, tr: (tbi[i], 0)), hbm, hbm, hbm],
            out_specs=pl.BlockSpec((tm, d), lambda i, te, tbi, tbo, tv, tn, tr: (tbo[i], 0)),
            scratch_shapes=[pltpu.VMEM((2, d, f), _f32), pltpu.VMEM((2, d, f), _f32), pltpu.VMEM((2, f, d), _f32),
                            pltpu.VMEM((d, f), _bf16), pltpu.VMEM((d, f), _bf16), pltpu.VMEM((f, d), _bf16),
                            pltpu.SemaphoreType.DMA((2,))],
        ),
        out_shape=jax.ShapeDtypeStruct(xs.shape, _f32),
        compiler_params=_params(("arbitrary",)),
        name="experts",
    )(tile_expert, tile_in, tile_out, tile_valid, tile_next, tile_run, xs, w_gate, w_up, w_down)


def _combine_kernel(*refs, n_first):
    pos_refs, posn_refs = refs[:TOP_K], refs[TOP_K:2 * TOP_K]
    ys_ref, wt_ref, sh_ref, x1_ref, gpost_ref, mod_ref = refs[2 * TOP_K:2 * TOP_K + 6]
    rest = refs[2 * TOP_K + 6:]
    final = n_first is not None
    if final:
        yctx_ref, ylat_ref, buf0_ref, buf1_ref, f_ref, sems = rest
    else:
        gnext_ref, modn_ref, x2_ref, hn_ref, buf0_ref, buf1_ref, f_ref, sems = rest
    tm, d = x1_ref.shape
    i = pl.program_id(0)
    last = pl.num_programs(0) - 1

    def start_rows(p_refs, buf_ref, sem, t):
        for k in range(TOP_K):
            pltpu.make_async_copy(_row(ys_ref, p_refs[k][t]), buf_ref.at[k, pl.ds(t, 1), :], sem).start()

    def wait_tile(buf_ref, sem):
        for k in range(TOP_K):
            pltpu.make_async_copy(ys_ref.at[pl.ds(0, tm), :], buf_ref.at[k], sem).wait()

    @pl.when(i == 0)
    def _():
        def first(t, carry):
            start_rows(pos_refs, buf0_ref, sems.at[0], t)
            return carry
        lax.fori_loop(0, tm, first, 0)

    def step(cur_ref, cur_sem, nxt_ref, nxt_sem):
        wait_tile(cur_ref, cur_sem)

        def group(g, carry):
            for u in range(SUBLANES):
                start_rows(posn_refs, nxt_ref, nxt_sem, g * SUBLANES + u)
            rows = pl.ds(pl.multiple_of(g * SUBLANES, SUBLANES), SUBLANES)
            w = wt_ref[rows, :]
            wb = [jnp.broadcast_to(w[:, k:k + 1], (SUBLANES, LANES)) for k in range(TOP_K)]
            for j in range(d // LANES):
                cols = slice(j * LANES, (j + 1) * LANES)
                acc = sh_ref[rows, cols]
                for k in range(TOP_K):
                    acc = acc + wb[k] * cur_ref[k, rows, cols]
                f_ref[rows, cols] = acc
            return carry

        lax.fori_loop(0, tm // SUBLANES, group, 0)

        @pl.when(i == last)
        def _():
            wait_tile(nxt_ref, nxt_sem)

    pl.when(i % 2 == 0)(functools.partial(step, buf0_ref, sems.at[0], buf1_ref, sems.at[1]))
    pl.when(i % 2 == 1)(functools.partial(step, buf1_ref, sems.at[1], buf0_ref, sems.at[0]))
    f = f_ref[...]
    x2 = x1_ref[...] + mod_ref[pl.ds(MOD_GATE_FFN, 1), :] * _rms(f, gpost_ref[...])
    if final:
        in_ctx = pl.program_id(0) < n_first

        @pl.when(in_ctx)
        def _():
            yctx_ref[...] = x2

        @pl.when(jnp.logical_not(in_ctx))
        def _():
            ylat_ref[...] = x2
    else:
        x2_ref[...] = x2
        hn = (_rms(x2, gnext_ref[...]) * (1.0 + modn_ref[pl.ds(MOD_SCALE_MIX, 1), :])
              + modn_ref[pl.ds(MOD_SHIFT_MIX, 1), :])
        hn_ref[...] = hn.astype(hn_ref.dtype)


def _combine(ys, pos, wt, shared, x1, g_post, mod, n_ctx, dec_seq, g_next=None, mod_next=None):
    t, d = x1.shape
    tm = min(GATHER_TILE, t)
    n_steps = t // tm
    final = g_next is None
    row = lambda i: (i, 0)
    fixed = lambda i: (0, 0)
    cond = lambda i: (_cond_row(i * tm, n_ctx, dec_seq), 0, 0)
    in_specs = _slot_specs(t, tm, lambda i: i) + _slot_specs(t, tm, lambda i: jnp.minimum(i + 1, n_steps - 1)) + [
        pl.BlockSpec(memory_space=pl.ANY),
        pl.BlockSpec((tm, TOP_K), row),
        pl.BlockSpec((tm, d), row),
        pl.BlockSpec((tm, d), row),
        pl.BlockSpec((1, d), fixed),
        pl.BlockSpec((None, MOD_ROWS, d), cond),
    ]
    args = [pos] * (2 * TOP_K) + [ys, wt, shared, x1, g_post.reshape(1, d), mod]
    if final:
        n0, n1 = n_ctx // tm, (t - n_ctx) // tm
        out_specs = [pl.BlockSpec((tm, d), lambda i: (jnp.minimum(i, n0 - 1), 0)),
                     pl.BlockSpec((tm, d), lambda i: (jnp.clip(i - n0, 0, n1 - 1), 0))]
        out_shape = [jax.ShapeDtypeStruct((n_ctx, d), _f32), jax.ShapeDtypeStruct((t - n_ctx, d), _f32)]
    else:
        in_specs += [pl.BlockSpec((1, d), fixed), pl.BlockSpec((None, MOD_ROWS, d), cond)]
        args += [g_next.reshape(1, d), mod_next]
        out_specs = [pl.BlockSpec((tm, d), row), pl.BlockSpec((tm, d), row)]
        out_shape = [jax.ShapeDtypeStruct((t, d), _f32), jax.ShapeDtypeStruct((t, d), _bf16)]
    return pl.pallas_call(
        functools.partial(_combine_kernel, n_first=n_ctx // tm if final else None),
        grid=(n_steps,),
        in_specs=in_specs,
        out_specs=out_specs,
        out_shape=out_shape,
        scratch_shapes=[pltpu.VMEM((TOP_K, tm, d), _f32), pltpu.VMEM((TOP_K, tm, d), _f32), pltpu.VMEM((tm, d), _f32),
                        pltpu.SemaphoreType.DMA((2,))],
        compiler_params=_params(("arbitrary",)),
        name="combine",
    )(*args)


def _moe(hf, idx_t, wt_t, rank_t, cnt, layer, w_gate, w_up, w_down, w_sh_gate, w_sh_up, w_sh_down):
    n_exp, d = w_gate.shape[1:3]
    t = hf.shape[0]
    tm = EXPERT_TILE
    n_tiles = t * TOP_K // tm + n_exp
    pos, *tiles, padrow, padn = _plan(cnt[:, 0], idx_t, rank_t, n_tiles)
    pos = pos.reshape(-1)
    xs = _dispatch(hf, pos, padrow, padn, n_tiles * tm)
    ys = _experts(xs, w_gate, w_up, w_down, layer, *tiles)
    n_sh = t // tm
    every = jnp.arange(n_sh, dtype=jnp.int32)
    zeros = jnp.zeros((n_sh,), jnp.int32)
    shared = _experts(hf, w_sh_gate[:, None], w_sh_up[:, None], w_sh_down[:, None], layer,
                      zeros, every, every, jnp.full((n_sh,), tm, jnp.int32),
                      jnp.full((n_sh,), NOT_FIRST, jnp.int32).at[0].set(NO_NEXT), zeros)
    return ys, pos, wt_t.T, shared


def kernel(x_prompt, x_sample, cache_k, cache_v, c, c_ctx, w_ada, b_ada, norm_mix_pre, norm_mix_post, norm_ffn_pre, norm_ffn_post, w_qkv, w_o_attn, rpb, w_conv_in, conv_w, w_conv_out, w_router, b_router, w_exp_gate, w_exp_up, w_exp_down, w_sh_gate, w_sh_up, w_sh_down):
    batch, seq, d = x_prompt.shape
    dec_batch, dec_seq, _ = x_sample.shape
    depth = w_ada.shape[0]
    n_ctx, n_lat = batch * seq, dec_batch * dec_seq
    dh = d // N_HEADS
    past = cache_k.shape[2]

    x = [x_prompt.reshape(n_ctx, d), x_sample.reshape(n_lat, d)]
    cond = jnp.concatenate([c_ctx[None, :], c], axis=0)
    mod = _adaln(cond, w_ada, b_ada)

    new_k, new_v = [], []
    h = _modulate(x, norm_mix_pre[0], mod[0], n_ctx, dec_seq)
    for l in range(depth):
        if l % 2 == 0:
            a = l // 2
            qkv = _matmul(h, w_qkv, a)
            o_ctx, k_new, v_new = _ctx_attention(qkv, batch, seq, d)
            new_k.append(k_new.reshape(batch, seq, N_HEADS, dh))
            new_v.append(v_new.reshape(batch, seq, N_HEADS, dh))
            o_lat = _natten(qkv, n_ctx, cache_k[:, a].reshape(dec_batch, past, d),
                            cache_v[:, a].reshape(dec_batch, past, d), rpb[a], dec_batch, dec_seq, d)
            mixed, w_out, w_layer = [o_ctx, o_lat], w_o_attn, a
        else:
            m = l // 2
            mixed = [_conv_gate(_matmul(h, w_conv_in, m), conv_w[m], n_ctx, seq, dec_seq, d)]
            w_out, w_layer = w_conv_out, m
        x1, hf, logits = _post_mixer(
            mixed, w_out, w_layer, x, norm_mix_post[l], mod[l], norm_ffn_pre[l], w_router[l], n_ctx, dec_seq)
        idx_t, wt_t, rank_t, cnt = _router(logits, b_router[l])
        ys, pos, wt, shared = _moe(hf, idx_t, wt_t, rank_t, cnt, l, w_exp_gate, w_exp_up, w_exp_down,
                                   w_sh_gate, w_sh_up, w_sh_down)
        if l + 1 < depth:
            x2, h = _combine(ys, pos, wt, shared, x1, norm_ffn_post[l], mod[l], n_ctx, dec_seq,
                             norm_mix_pre[l + 1], mod[l + 1])
            x = [x2]
        else:
            y_ctx, y_lat = _combine(ys, pos, wt, shared, x1, norm_ffn_post[l], mod[l], n_ctx, dec_seq)

    return (y_ctx.reshape(batch, seq, d), y_lat.reshape(dec_batch, dec_seq, d),
            jnp.stack(new_k, axis=1), jnp.stack(new_v, axis=1))
```

```python
import functools

import jax
import jax.numpy as jnp
from jax import lax
from jax.experimental import pallas as pl
from jax.experimental.pallas import tpu as pltpu

N_HEADS = 16
GRID_W = 64
WIN_ROWS = 8
WIN_COLS = 16
CONV_WIDTH = 3
N_EXPERTS = 64
TOP_K = 8
N_GROUPS = 8
TOPK_GROUPS = 4
ROUTED_SCALE = 2.5
N_MOD = 6
RMS_EPS = 1e-6
NEG_INF = -1e30

LANES = 128
SUBLANES = 8
VMEM_LIMIT = 56 * 1024 * 1024

MOD_SHIFT_MIX, MOD_SCALE_MIX, MOD_GATE_MIX, MOD_SHIFT_FFN, MOD_SCALE_FFN, MOD_GATE_FFN = range(6)
MOD_ROWS = 8

ROW_TILE = 256
ROUTER_TILE = 1024
EXPERT_TILE = 512
MM_TILE_M = 2048
MM_TILE_N = 512
GATHER_TILE = 128
SCATTER_TILE = 1024

_f32 = jnp.float32
_bf16 = jnp.bfloat16


def _params(sem, vmem=VMEM_LIMIT):
    return pltpu.CompilerParams(dimension_semantics=sem, vmem_limit_bytes=vmem)


def _rms(x, g):
    return x * lax.rsqrt(jnp.mean(x * x, axis=-1, keepdims=True) + RMS_EPS) * g


def _dot(a, b):
    return jnp.dot(a, b, preferred_element_type=_f32)


def _dot_nt(a, b):
    return lax.dot_general(a, b, (((1,), (1,)), ((), ())), preferred_element_type=_f32)


def _cond_row(row0, n_ctx, dec_seq):
    return jnp.where(row0 < n_ctx, 0, 1 + (row0 - n_ctx) // dec_seq)


def _adaln_kernel(cb_ref, w_ref, b_ref, o_ref, acc_ref, *, n_cond):
    k = pl.program_id(2)

    @pl.when(k == 0)
    def _():
        acc_ref[...] = jnp.zeros_like(acc_ref)

    tk, tn = w_ref.shape
    kb = min(LANES, tk)
    for k0 in range(0, tk, kb):
        s = []
        for r in range(n_cond):
            v = cb_ref[r, k0:k0 + kb, :]
            s.append(v * jax.nn.sigmoid(v))
        for c in range(tn // LANES):
            cols = slice(c * LANES, (c + 1) * LANES)
            w = w_ref[k0:k0 + kb, cols]
            for r in range(n_cond):
                acc_ref[r, :, cols] += (w * s[r]).reshape(kb // SUBLANES, SUBLANES, LANES).sum(axis=0)

    @pl.when(k == pl.num_programs(2) - 1)
    def _():
        o_ref[...] = jnp.zeros_like(o_ref)
        for r in range(n_cond):
            o_ref[pl.ds(r, 1), :] = acc_ref[r].sum(axis=0, keepdims=True) + b_ref[...]


def _adaln(cond, w_ada, b_ada):
    n_cond, d = cond.shape
    n_layers, _, n6 = w_ada.shape
    tk, tn = min(1024, d), min(2048, n6)
    cb = jnp.broadcast_to(cond[:, :, None], (n_cond, d, LANES))
    mod = pl.pallas_call(
        functools.partial(_adaln_kernel, n_cond=n_cond),
        grid=(n_layers, n6 // tn, d // tk),
        in_specs=[
            pl.BlockSpec((n_cond, tk, LANES), lambda l, n, k: (0, k, 0)),
            pl.BlockSpec((None, tk, tn), lambda l, n, k: (l, k, n)),
            pl.BlockSpec((None, 1, tn), lambda l, n, k: (l, 0, n)),
        ],
        out_specs=pl.BlockSpec((None, MOD_ROWS, tn), lambda l, n, k: (l, 0, n)),
        out_shape=jax.ShapeDtypeStruct((n_layers, MOD_ROWS, n6), _f32),
        scratch_shapes=[pltpu.VMEM((n_cond, SUBLANES, tn), _f32)],
        compiler_params=_params(("arbitrary", "arbitrary", "arbitrary")),
        name="adaln",
    )(cb, w_ada, b_ada.reshape(n_layers, 1, n6))
    mod = mod[:, :n_cond].reshape(n_layers, n_cond, N_MOD, d)
    return jnp.pad(mod, ((0, 0), (0, 0), (0, MOD_ROWS - N_MOD), (0, 0)))


def _split_specs(parts, tm, width):
    if len(parts) == 1:
        return [pl.BlockSpec((tm, width), lambda i: (i, 0))]
    n0 = parts[0].shape[0] // tm
    n1 = parts[1].shape[0] // tm
    return [pl.BlockSpec((tm, width), lambda i: (jnp.minimum(i, n0 - 1), 0)),
            pl.BlockSpec((tm, width), lambda i: (jnp.clip(i - n0, 0, n1 - 1), 0))]


def _split_read(refs, n_first):
    if len(refs) == 1:
        return refs[0][...]
    return jnp.where(pl.program_id(0) < n_first, refs[0][...], refs[1][...])


def _modulate_kernel(*refs, n_x, n_first):
    x_refs, (g_ref, mod_ref, o_ref) = refs[:n_x], refs[n_x:]
    y = _rms(_split_read(x_refs, n_first), g_ref[...])
    h = y * (1.0 + mod_ref[pl.ds(MOD_SCALE_MIX, 1), :]) + mod_ref[pl.ds(MOD_SHIFT_MIX, 1), :]
    o_ref[...] = h.astype(o_ref.dtype)


def _modulate(xs, g, mod, n_ctx, dec_seq):
    t = sum(x.shape[0] for x in xs)
    d = xs[0].shape[1]
    tm = ROW_TILE
    return pl.pallas_call(
        functools.partial(_modulate_kernel, n_x=len(xs), n_first=xs[0].shape[0] // tm),
        grid=(t // tm,),
        in_specs=_split_specs(xs, tm, d) + [
            pl.BlockSpec((1, d), lambda i: (0, 0)),
            pl.BlockSpec((None, MOD_ROWS, d), lambda i: (_cond_row(i * tm, n_ctx, dec_seq), 0, 0)),
        ],
        out_specs=pl.BlockSpec((tm, d), lambda i: (i, 0)),
        out_shape=jax.ShapeDtypeStruct((t, d), _bf16),
        compiler_params=_params(("parallel",)),
        name="modulate",
    )(*xs, g.reshape(1, d), mod)


def _matmul_kernel(a_ref, w_ref, o_ref):
    o_ref[...] = _dot(a_ref[...], w_ref[...].astype(_bf16)).astype(o_ref.dtype)


def _matmul(a, w, layer):
    n_rows = a.shape[0]
    _, k, n = w.shape
    tm, tn = min(MM_TILE_M, n_rows), min(MM_TILE_N, n)
    return pl.pallas_call(
        _matmul_kernel,
        grid=(n_rows // tm, n // tn),
        in_specs=[
            pl.BlockSpec((tm, k), lambda i, j: (i, 0)),
            pl.BlockSpec((None, k, tn), lambda i, j: (layer, 0, j)),
        ],
        out_specs=pl.BlockSpec((tm, tn), lambda i, j: (i, j)),
        out_shape=jax.ShapeDtypeStruct((n_rows, n), _f32),
        compiler_params=_params(("parallel", "parallel")),
        name="matmul",
    )(a, w)


def _ctx_attn_kernel(q_ref, k_ref, v_ref, o_ref, kout_ref, vout_ref, *, n_heads, scale):
    seq = q_ref.shape[0]
    dh = q_ref.shape[1] // n_heads
    for h in range(n_heads):
        sl = slice(h * dh, (h + 1) * dh)
        k32 = k_ref[:, sl]
        v32 = v_ref[:, sl]
        kout_ref[pl.ds(h, seq, stride=n_heads), :] = k32
        vout_ref[pl.ds(h, seq, stride=n_heads), :] = v32
        q = q_ref[:, sl].astype(_bf16)
        s = _dot_nt(q, k32.astype(_bf16)) * scale
        p = jnp.exp(s - s.max(axis=-1, keepdims=True))
        o = _dot(p.astype(_bf16), v32.astype(_bf16)) / p.sum(axis=-1, keepdims=True)
        o_ref[:, sl] = o.astype(o_ref.dtype)


def _ctx_attention(qkv, batch, seq, d):
    dh = d // N_HEADS
    assert dh == LANES
    cache = jax.ShapeDtypeStruct((batch * seq * N_HEADS, dh), _f32)
    return pl.pallas_call(
        functools.partial(_ctx_attn_kernel, n_heads=N_HEADS, scale=dh ** -0.5),
        grid=(batch,),
        in_specs=[
            pl.BlockSpec((seq, d), lambda b: (b, 0)),
            pl.BlockSpec((seq, d), lambda b: (b, 1)),
            pl.BlockSpec((seq, d), lambda b: (b, 2)),
        ],
        out_specs=[
            pl.BlockSpec((seq, d), lambda b: (b, 0)),
            pl.BlockSpec((seq * N_HEADS, dh), lambda b: (b, 0)),
            pl.BlockSpec((seq * N_HEADS, dh), lambda b: (b, 0)),
        ],
        out_shape=[jax.ShapeDtypeStruct((batch * seq, d), _bf16), cache, cache],
        compiler_params=_params(("parallel",)),
        name="ctx_attention",
    )(qkv, qkv, qkv)


def _natten_kernel(q_ref, k_ref, v_ref, kc_ref, vc_ref, rpb_ref, o_ref, qb_ref, kb_ref, vb_ref,
                   sctx_ref, pctx_ref, oloc_ref, den_ref, blo_ref, bhi_ref, *, rows, width, kh, scale, unroll):
    n_loc = kh * width
    q_col = lax.broadcasted_iota(jnp.int32, (width, n_loc), 0)
    k_col = lax.broadcasted_iota(jnp.int32, (width, n_loc), 1) % width
    col_start = jnp.clip(q_col - WIN_COLS // 2, 0, width - WIN_COLS)
    col_mask = (k_col >= col_start) & (k_col < col_start + WIN_COLS)
    low_half = lax.broadcasted_iota(jnp.int32, (width, LANES), 1) < width

    for dr in range(rpb_ref.shape[0]):
        row = jnp.broadcast_to(rpb_ref[pl.ds(dr, 1), :], (width, LANES))
        blo_ref[dr] = pltpu.roll(row, LANES - (WIN_COLS - 1), axis=1, stride=1, stride_axis=0)
        bhi_ref[dr] = pltpu.roll(row, (LANES - (WIN_COLS - 1) + width) % LANES, axis=1, stride=1, stride_axis=0)

    def bias_of(delta):
        cols = []
        for i in range(0, kh, 2):
            dr = i - delta + (WIN_ROWS - 1)
            cols.append(jnp.where(low_half, blo_ref[dr], bhi_ref[dr + 1]))
        return jnp.concatenate(cols, axis=1)

    qb_ref[...] = q_ref[...].astype(_bf16)
    kb_ref[...] = k_ref[...].astype(_bf16)
    vb_ref[...] = v_ref[...].astype(_bf16)
    sctx_ref[...] = _dot_nt(qb_ref[...], kc_ref[...].astype(_bf16)) * scale

    def one_row(r):
        r0 = jnp.clip(r - kh // 2, 0, rows - kh)
        qrows = pl.ds(pl.multiple_of(r * width, width), width)
        win = pl.ds(pl.multiple_of(r0 * width, width), n_loc)
        s_loc = _dot_nt(qb_ref[qrows, :], kb_ref[win, :]) * scale + bias_of(r - r0)
        s_loc = jnp.where(col_mask, s_loc, NEG_INF)
        s_ctx = sctx_ref[qrows, :]
        m = jnp.maximum(s_loc.max(axis=-1, keepdims=True), s_ctx.max(axis=-1, keepdims=True))
        p_loc = jnp.exp(s_loc - m)
        p_ctx = jnp.exp(s_ctx - m)
        den = p_loc.sum(axis=-1, keepdims=True) + p_ctx.sum(axis=-1, keepdims=True)
        pctx_ref[qrows, :] = p_ctx.astype(_bf16)
        oloc_ref[qrows, :] = _dot(p_loc.astype(_bf16), vb_ref[win, :])
        den_ref[qrows, :] = jnp.broadcast_to(den, (width, den_ref.shape[1]))

    def some_rows(g, carry):
        for u in range(unroll):
            one_row(g * unroll + u)
        return carry

    lax.fori_loop(0, rows // unroll, some_rows, 0)
    o = oloc_ref[...] + _dot(pctx_ref[...], vc_ref[...].astype(_bf16))
    o_ref[...] = (o / den_ref[...]).astype(o_ref.dtype)


def _natten(qkv, n_ctx, k_ctx, v_ctx, rpb, dec_batch, dec_seq, d):
    assert n_ctx % dec_seq == 0
    dh = d // N_HEADS
    rows = dec_seq // GRID_W
    kh = min(WIN_ROWS, rows)
    assert 2 * GRID_W == LANES and kh % 2 == 0 and 2 * WIN_COLS - 1 <= LANES
    past = k_ctx.shape[1]
    n_dr = rpb.shape[1]
    rpb = jnp.pad(rpb.astype(_f32), ((0, 0), (0, 0), (0, LANES - rpb.shape[2])))
    b0 = n_ctx // dec_seq
    unroll = next(u for u in (8, 4, 2, 1) if rows % u == 0)
    return pl.pallas_call(
        functools.partial(_natten_kernel, rows=rows, width=GRID_W, kh=kh, scale=dh ** -0.5, unroll=unroll),
        grid=(dec_batch, N_HEADS),
        in_specs=[
            pl.BlockSpec((dec_seq, dh), lambda b, h: (b0 + b, h)),
            pl.BlockSpec((dec_seq, dh), lambda b, h: (b0 + b, N_HEADS + h)),
            pl.BlockSpec((dec_seq, dh), lambda b, h: (b0 + b, 2 * N_HEADS + h)),
            pl.BlockSpec((None, past, dh), lambda b, h: (b, 0, h)),
            pl.BlockSpec((None, past, dh), lambda b, h: (b, 0, h)),
            pl.BlockSpec((None, n_dr, LANES), lambda b, h: (h, 0, 0)),
        ],
        out_specs=pl.BlockSpec((dec_seq, dh), lambda b, h: (b, h)),
        out_shape=jax.ShapeDtypeStruct((dec_batch * dec_seq, d), _bf16),
        scratch_shapes=[pltpu.VMEM((dec_seq, dh), _bf16), pltpu.VMEM((dec_seq, dh), _bf16),
                        pltpu.VMEM((dec_seq, dh), _bf16),
                        pltpu.VMEM((dec_seq, past), _f32), pltpu.VMEM((dec_seq, past), _bf16),
                        pltpu.VMEM((dec_seq, dh), _f32), pltpu.VMEM((dec_seq, dh), _f32),
                        pltpu.VMEM((n_dr, GRID_W, LANES), _f32), pltpu.VMEM((n_dr, GRID_W, LANES), _f32)],
        compiler_params=_params(("parallel", "parallel")),
        name="natten",
    )(qkv, qkv, qkv, k_ctx, v_ctx, rpb)


def _conv_gate_kernel(b_ref, c_ref, u_ref, cp_ref, up_ref, cn_ref, un_ref, w_ref, o_ref,
                      *, n_ctx, seq, dec_seq):
    tm = o_ref.shape[0]
    row0 = pl.program_id(0) * tm
    in_ctx = row0 < n_ctx
    pos0 = jnp.where(in_ctx, row0 % seq, (row0 - n_ctx) % dec_seq)
    seq_len = jnp.where(in_ctx, seq, dec_seq)
    has_prev = pos0 > 0
    has_next = pos0 + tm < seq_len
    cu = c_ref[...] * u_ref[...]
    prev_row = jnp.where(has_prev, cp_ref[pl.ds(SUBLANES - 1, 1), :] * up_ref[pl.ds(SUBLANES - 1, 1), :], 0.0)
    next_row = jnp.where(has_next, cn_ref[pl.ds(0, 1), :] * un_ref[pl.ds(0, 1), :], 0.0)
    ridx = lax.broadcasted_iota(jnp.int32, cu.shape, 0)
    before = jnp.where(ridx == 0, prev_row, pltpu.roll(cu, 1, axis=0))
    after = jnp.where(ridx == tm - 1, next_row, pltpu.roll(cu, tm - 1, axis=0))
    conv = w_ref[pl.ds(0, 1), :] * before + w_ref[pl.ds(1, 1), :] * cu + w_ref[pl.ds(2, 1), :] * after
    o_ref[...] = (b_ref[...] * conv).astype(o_ref.dtype)


def _conv_gate(bcu, conv_w, n_ctx, seq, dec_seq, d):
    t = bcu.shape[0]
    tm = ROW_TILE
    halo = tm // SUBLANES
    last = t // SUBLANES - 1
    cw = jnp.pad(conv_w, ((0, SUBLANES - CONV_WIDTH), (0, 0)))
    prev_map = lambda col: (lambda i: (jnp.maximum(i * halo - 1, 0), col))
    next_map = lambda col: (lambda i: (jnp.minimum((i + 1) * halo, last), col))
    return pl.pallas_call(
        functools.partial(_conv_gate_kernel, n_ctx=n_ctx, seq=seq, dec_seq=dec_seq),
        grid=(t // tm,),
        in_specs=[
            pl.BlockSpec((tm, d), lambda i: (i, 0)),
            pl.BlockSpec((tm, d), lambda i: (i, 1)),
            pl.BlockSpec((tm, d), lambda i: (i, 2)),
            pl.BlockSpec((SUBLANES, d), prev_map(1)),
            pl.BlockSpec((SUBLANES, d), prev_map(2)),
            pl.BlockSpec((SUBLANES, d), next_map(1)),
            pl.BlockSpec((SUBLANES, d), next_map(2)),
            pl.BlockSpec((SUBLANES, d), lambda i: (0, 0)),
        ],
        out_specs=pl.BlockSpec((tm, d), lambda i: (i, 0)),
        out_shape=jax.ShapeDtypeStruct((t, d), _bf16),
        compiler_params=_params(("parallel",)),
        name="conv_gate",
    )(bcu, bcu, bcu, bcu, bcu, bcu, bcu, cw)


def _route(sel, scores):
    n_grp, eg, tm = sel.shape
    n_exp = n_grp * eg
    j_iota = lax.broadcasted_iota(jnp.int32, sel.shape, 1)
    m1 = sel.max(axis=1, keepdims=True)
    j1 = jnp.min(jnp.where(sel == m1, j_iota, eg), axis=1, keepdims=True)
    m2 = jnp.max(jnp.where(j_iota == j1, -jnp.inf, sel), axis=1, keepdims=True)
    grp = m1 + m2
    g_iota = lax.broadcasted_iota(jnp.int32, grp.shape, 0)
    g_sel = g_iota < 0
    for _ in range(TOPK_GROUPS):
        gm = grp.max(axis=0, keepdims=True)
        gi = jnp.min(jnp.where(grp == gm, g_iota, n_grp), axis=0, keepdims=True)
        hit = g_iota == gi
        g_sel = g_sel | hit
        grp = jnp.where(hit, -jnp.inf, grp)
    cur = jnp.where(jnp.broadcast_to(g_sel, sel.shape), sel, NEG_INF)
    e_iota = lax.broadcasted_iota(jnp.int32, sel.shape, 0) * eg + j_iota
    ids, ws, hits = [], [], []
    for _ in range(TOP_K):
        m = cur.max(axis=1, keepdims=True).max(axis=0, keepdims=True)
        ei = jnp.min(jnp.where(cur == m, e_iota, n_exp), axis=1, keepdims=True).min(axis=0, keepdims=True)
        hit = e_iota == ei
        ids.append(ei)
        ws.append(jnp.sum(jnp.where(hit, scores, 0.0), axis=1, keepdims=True).sum(axis=0, keepdims=True))
        hits.append(hit)
        cur = jnp.where(hit, -jnp.inf, cur)
    total = functools.reduce(lambda a, b: a + b, ws)
    ws = [w / total * ROUTED_SCALE for w in ws]
    return ids, ws, hits


def _split_bf16(x):
    hi = x.astype(_bf16)
    return hi, (x - hi.astype(_f32)).astype(_bf16)


def _post_mixer_kernel(*refs, n_a, n_x, n_first):
    a_refs, x_refs = refs[:n_a], refs[n_a:n_a + n_x]
    wo_ref, gpost_ref, mod_ref, gpre_ref, wr_ref, x1_ref, hf_ref, logit_ref = refs[n_a + n_x:]
    o = _dot(_split_read(a_refs, n_first), wo_ref[...])
    x1 = _split_read(x_refs, n_first) + mod_ref[pl.ds(MOD_GATE_MIX, 1), :] * _rms(o, gpost_ref[...])
    x1_ref[...] = x1
    hf = _rms(x1, gpre_ref[...]) * (1.0 + mod_ref[pl.ds(MOD_SCALE_FFN, 1), :]) + mod_ref[pl.ds(MOD_SHIFT_FFN, 1), :]
    hf_ref[...] = hf
    h_hi, h_lo = _split_bf16(hf)
    w_hi, w_lo = _split_bf16(wr_ref[...])
    logit_ref[...] = _dot(h_hi, w_hi) + (_dot(h_hi, w_lo) + _dot(h_lo, w_hi))


def _post_mixer(a_parts, w_out, layer, x_parts, g_post, mod, g_pre, w_router, n_ctx, dec_seq):
    t = sum(x.shape[0] for x in x_parts)
    d = x_parts[0].shape[1]
    tm = ROW_TILE
    e = w_router.shape[1]
    assert e <= LANES
    row = lambda i: (i, 0)
    fixed = lambda i: (0, 0)
    return pl.pallas_call(
        functools.partial(_post_mixer_kernel, n_a=len(a_parts), n_x=len(x_parts), n_first=n_ctx // tm),
        grid=(t // tm,),
        in_specs=_split_specs(a_parts, tm, d) + _split_specs(x_parts, tm, d) + [
            pl.BlockSpec((None, d, d), lambda i: (layer, 0, 0)),
            pl.BlockSpec((1, d), fixed),
            pl.BlockSpec((None, MOD_ROWS, d), lambda i: (_cond_row(i * tm, n_ctx, dec_seq), 0, 0)),
            pl.BlockSpec((1, d), fixed),
            pl.BlockSpec((d, LANES), fixed),
        ],
        out_specs=[pl.BlockSpec((tm, d), row), pl.BlockSpec((tm, d), row), pl.BlockSpec((tm, LANES), row)],
        out_shape=[jax.ShapeDtypeStruct((t, d), _f32), jax.ShapeDtypeStruct((t, d), _f32),
                   jax.ShapeDtypeStruct((t, LANES), _f32)],
        compiler_params=_params(("parallel",)),
        name="post_mixer",
    )(*a_parts, *x_parts, w_out.astype(_bf16), g_post.reshape(1, d), mod, g_pre.reshape(1, d),
      jnp.pad(w_router, ((0, 0), (0, LANES - e))))


def _router_kernel(logit_ref, br_ref, idx_ref, wt_ref, rank_ref, cnt_ref, carry_ref, before_ref):
    i = pl.program_id(0)
    tm = logit_ref.shape[0]
    n_exp = br_ref.shape[0]

    @pl.when(i == 0)
    def _():
        carry_ref[...] = jnp.zeros_like(carry_ref)
        t_src = lax.broadcasted_iota(jnp.int32, (tm, tm), 0)
        t_dst = lax.broadcasted_iota(jnp.int32, (tm, tm), 1)
        before_ref[...] = jnp.where(t_src < t_dst, 1.0, 0.0).astype(_bf16)

    logits = logit_ref[...].T[:n_exp]
    grouped = (N_GROUPS, n_exp // N_GROUPS, tm)
    scores = jax.nn.sigmoid(logits)
    ids, ws, hits = _route((scores + br_ref[...]).reshape(grouped), scores.reshape(grouped))
    for k in range(TOP_K):
        idx_ref[pl.ds(k, 1), :] = ids[k][0]
        wt_ref[pl.ds(k, 1), :] = ws[k][0]

    any_hit = functools.reduce(lambda a, b: a | b, hits)
    mask = jnp.where(any_hit, 1.0, 0.0).reshape(n_exp, tm).astype(_bf16)
    rank = _dot(mask, before_ref[...]) + jnp.concatenate([carry_ref[...]] * (tm // LANES), axis=1)
    rank = rank.reshape(grouped)
    for k in range(TOP_K):
        rk = jnp.sum(jnp.where(hits[k], rank, 0.0), axis=1, keepdims=True).sum(axis=0, keepdims=True)
        rank_ref[pl.ds(k, 1), :] = rk[0].astype(jnp.int32)
    carry_ref[...] += _dot(mask, jnp.ones((tm, LANES), _bf16))
    cnt_ref[...] = carry_ref[...].astype(jnp.int32)


def _router(logits, b_router):
    t = logits.shape[0]
    e = b_router.shape[0]
    tm = min(ROUTER_TILE, t)
    col = lambda i: (0, i)
    fixed = lambda i: (0, 0)
    slots_i = jax.ShapeDtypeStruct((TOP_K, t), jnp.int32)
    return pl.pallas_call(
        _router_kernel,
        grid=(t // tm,),
        in_specs=[pl.BlockSpec((tm, LANES), lambda i: (i, 0)), pl.BlockSpec((e, 1), fixed)],
        out_specs=[pl.BlockSpec((TOP_K, tm), col), pl.BlockSpec((TOP_K, tm), col), pl.BlockSpec((TOP_K, tm), col),
                   pl.BlockSpec((e, LANES), fixed)],
        out_shape=[slots_i, jax.ShapeDtypeStruct((TOP_K, t), _f32), slots_i,
                   jax.ShapeDtypeStruct((e, LANES), jnp.int32)],
        scratch_shapes=[pltpu.VMEM((e, LANES), _f32), pltpu.VMEM((tm, tm), _bf16)],
        compiler_params=_params(("arbitrary",)),
        name="router",
    )(logits, b_router.reshape(e, 1))


def _row(ref, r):
    return ref.at[pl.ds(r, 1), :]


NOT_FIRST = -2
NO_NEXT = -1


def _plan_kernel(cnt_ref, idx_ref, rank_ref, pos_ref, te_ref, tbi_ref, tbo_ref, tv_ref, tnext_ref, trun_ref,
                 padrow_ref, padn_ref, offs_ref, *, tm):
    n_exp = cnt_ref.shape[0]
    n_tiles = te_ref.shape[0]

    def per_expert(e, carry):
        off, tile, run, prev_first = carry
        cnt = cnt_ref[e]
        n_t = (cnt + tm - 1) // tm
        offs_ref[e] = off
        padrow_ref[e] = off + cnt
        padn_ref[e] = n_t * tm - cnt

        def per_tile(j, c):
            te_ref[tile + j] = e
            tbi_ref[tile + j] = tile + j
            tbo_ref[tile + j] = tile + j
            tv_ref[tile + j] = jnp.minimum(cnt - j * tm, tm)
            tnext_ref[tile + j] = jnp.where(j == 0, NO_NEXT, NOT_FIRST)
            trun_ref[tile + j] = run
            return c

        lax.fori_loop(0, n_t, per_tile, 0)
        used = n_t > 0

        @pl.when(used & (prev_first >= 0))
        def _():
            tnext_ref[prev_first] = e

        return (off + n_t * tm, tile + n_t, run + used.astype(jnp.int32), jnp.where(used, tile, prev_first))

    zero = jnp.int32(0)
    _, live, _, _ = lax.fori_loop(0, n_exp, per_expert, (zero, zero, zero, jnp.int32(-1)))
    last = jnp.maximum(live - 1, 0)
    last_expert = te_ref[last]

    def dead_tile(i, c):
        te_ref[i] = last_expert
        tbi_ref[i] = last
        tbo_ref[i] = i
        tv_ref[i] = 0
        tnext_ref[i] = NOT_FIRST
        trun_ref[i] = 0
        return c

    lax.fori_loop(live, n_tiles, dead_tile, 0)

    idx = idx_ref[...]
    pos = rank_ref[...]
    for e in range(n_exp):
        pos = pos + jnp.where(idx == e, offs_ref[e], 0)
    pos_ref[...] = pos


def _plan(counts, idx_t, rank_t, n_tiles):
    n_exp = counts.shape[0]
    smem = pl.BlockSpec(memory_space=pltpu.SMEM)
    vmem = pl.BlockSpec(memory_space=pltpu.VMEM)
    tiles = jax.ShapeDtypeStruct((n_tiles,), jnp.int32)
    experts = jax.ShapeDtypeStruct((n_exp,), jnp.int32)
    return pl.pallas_call(
        functools.partial(_plan_kernel, tm=EXPERT_TILE),
        in_specs=[smem, vmem, vmem],
        out_specs=[vmem] + [smem] * 8,
        out_shape=[jax.ShapeDtypeStruct(idx_t.shape, jnp.int32)] + [tiles] * 6 + [experts] * 2,
        scratch_shapes=[pltpu.SMEM((n_exp,), jnp.int32)],
        name="plan",
    )(counts, idx_t, rank_t)


def _pad_copies(start, n, zero_ref, dst_ref, sem, pad_bits):
    single = n & (SUBLANES - 1)
    for s in range(SUBLANES - 1):
        yield s < single, pltpu.make_async_copy(_row(zero_ref, 0), _row(dst_ref, start + s), sem)
    base = pl.multiple_of(start + single, SUBLANES)
    groups = n // SUBLANES
    for b in range(pad_bits - (SUBLANES.bit_length() - 1)):
        rows = SUBLANES << b
        first = pl.multiple_of(base + ((groups >> (b + 1)) << (b + 1)) * SUBLANES, SUBLANES)
        copy = pltpu.make_async_copy(zero_ref.at[pl.ds(0, rows), :], dst_ref.at[pl.ds(first, rows), :], sem)
        yield ((groups >> b) & 1) == 1, copy


def _slot_specs(n_tokens, tile, step_of):
    per_k = n_tokens // tile
    return [pl.BlockSpec((tile,), lambda i, k=k: (k * per_k + step_of(i),), memory_space=pltpu.SMEM)
            for k in range(TOP_K)]


def _dispatch_kernel(*refs, pad_bits):
    pos_refs = refs[:TOP_K]
    padrow_ref, padn_ref, src_ref, dst_ref, zero_ref, sem, zero_sem = refs[TOP_K:]
    i = pl.program_id(0)
    tile = src_ref.shape[0]
    n_exp = padn_ref.shape[0]

    def for_each_pad_copy(fn):
        def body(e, c):
            for needed, copy in _pad_copies(padrow_ref[e], padn_ref[e], zero_ref, dst_ref, zero_sem, pad_bits):
                pl.when(needed)(functools.partial(fn, copy))
            return c
        lax.fori_loop(0, n_exp, body, 0)

    @pl.when(i == 0)
    def _():
        zero_ref[...] = jnp.zeros_like(zero_ref)
        for_each_pad_copy(lambda copy: copy.start())

    def issue(g, carry):
        for t in (2 * g, 2 * g + 1):
            for k in range(TOP_K):
                pltpu.make_async_copy(_row(src_ref, t), _row(dst_ref, pos_refs[k][t]), sem).start()
        return carry

    lax.fori_loop(0, tile // 2, issue, 0)
    for k in range(TOP_K):
        pltpu.make_async_copy(src_ref, dst_ref.at[pl.ds(0, tile), :], sem).wait()

    @pl.when(i == 0)
    def _():
        for_each_pad_copy(lambda copy: copy.wait())


def _dispatch(hf, pos, padrow, padn, n_sorted):
    t, d = hf.shape
    tile = min(SCATTER_TILE, t)
    pad_bits = (EXPERT_TILE - 1).bit_length()
    smem = pl.BlockSpec(memory_space=pltpu.SMEM)
    return pl.pallas_call(
        functools.partial(_dispatch_kernel, pad_bits=pad_bits),
        grid=(t // tile,),
        in_specs=_slot_specs(t, tile, lambda i: i) + [
            smem, smem,
            pl.BlockSpec((tile, d), lambda i: (i, 0)),
        ],
        out_specs=pl.BlockSpec(memory_space=pl.ANY),
        out_shape=jax.ShapeDtypeStruct((n_sorted, d), _f32),
        scratch_shapes=[pltpu.VMEM((1 << (pad_bits - 1), d), _f32),
                        pltpu.SemaphoreType.DMA(()), pltpu.SemaphoreType.DMA(())],
        compiler_params=_params(("arbitrary",)),
        name="dispatch",
    )(*([pos] * TOP_K), padrow, padn, hf)


def _expert_kernel(te_ref, tbi_ref, tbo_ref, tv_ref, tnext_ref, trun_ref, x_ref, wg_hbm, wu_hbm, wd_hbm, o_ref,
                   wg_ref, wu_ref, wd_ref, wgb_ref, wub_ref, wdb_ref, sems, *, layer):
    i = pl.program_id(0)
    live = tv_ref[i] > 0
    slot = trun_ref[i] % 2

    def fetch(expert, s):
        return [pltpu.make_async_copy(src.at[layer, expert], dst.at[s], sems.at[s])
                for src, dst in ((wg_hbm, wg_ref), (wu_hbm, wu_ref), (wd_hbm, wd_ref))]

    @pl.when(i == 0)
    def _():
        for copy in fetch(te_ref[0], 0):
            copy.start()

    @pl.when(live & (tnext_ref[i] != NOT_FIRST))
    def _():
        @pl.when(tnext_ref[i] != NO_NEXT)
        def _():
            for copy in fetch(tnext_ref[i], 1 - slot):
                copy.start()

        for copy in fetch(te_ref[i], slot):
            copy.wait()
        wgb_ref[...] = wg_ref[slot].astype(_bf16)
        wub_ref[...] = wu_ref[slot].astype(_bf16)
        wdb_ref[...] = wd_ref[slot].astype(_bf16)

    @pl.when(live)
    def _():
        x = x_ref[...].astype(_bf16)
        g = _dot(x, wgb_ref[...])
        u = _dot(x, wub_ref[...])
        h = (g * jax.nn.sigmoid(g) * u).astype(_bf16)
        o_ref[...] = _dot(h, wdb_ref[...])

    @pl.when(jnp.logical_not(live))
    def _():
        o_ref[...] = jnp.zeros_like(o_ref)


def _experts(xs, w_gate, w_up, w_down, layer, tile_expert, tile_in, tile_out, tile_valid, tile_next, tile_run):
    _, n_exp, d, f = w_gate.shape
    tm = EXPERT_TILE
    n_tiles = tile_expert.shape[0]
    hbm = pl.BlockSpec(memory_space=pl.ANY)
    return pl.pallas_call(
        functools.partial(_expert_kernel, layer=layer),
        grid_spec=pltpu.PrefetchScalarGridSpec(
            num_scalar_prefetch=6,
            grid=(n_tiles,),
            in_specs=[pl.BlockSpec((tm, d), lambda i, te, tbi, tbo, tv, tn, tr: (tbi[i], 0)), hbm, hbm, hbm],
            out_specs=pl.BlockSpec((tm, d), lambda i, te, tbi, tbo, tv, tn, tr: (tbo[i], 0)),
            scratch_shapes=[pltpu.VMEM((2, d, f), _f32), pltpu.VMEM((2, d, f), _f32), pltpu.VMEM((2, f, d), _f32),
                            pltpu.VMEM((d, f), _bf16), pltpu.VMEM((d, f), _bf16), pltpu.VMEM((f, d), _bf16),
                            pltpu.SemaphoreType.DMA((2,))],
        ),
        out_shape=jax.ShapeDtypeStruct(xs.shape, _f32),
        compiler_params=_params(("arbitrary",)),
        name="experts",
    )(tile_expert, tile_in, tile_out, tile_valid, tile_next, tile_run, xs, w_gate, w_up, w_down)


def _combine_kernel(*refs, n_first):
    pos_refs, posn_refs = refs[:TOP_K], refs[TOP_K:2 * TOP_K]
    ys_ref, wt_ref, sh_ref, x1_ref, gpost_ref, mod_ref = refs[2 * TOP_K:2 * TOP_K + 6]
    rest = refs[2 * TOP_K + 6:]
    final = n_first is not None
    if final:
        yctx_ref, ylat_ref, buf0_ref, buf1_ref, f_ref, sems = rest
    else:
        gnext_ref, modn_ref, x2_ref, hn_ref, buf0_ref, buf1_ref, f_ref, sems = rest
    tm, d = x1_ref.shape
    i = pl.program_id(0)
    last = pl.num_programs(0) - 1

    def start_rows(p_refs, buf_ref, sem, t):
        for k in range(TOP_K):
            pltpu.make_async_copy(_row(ys_ref, p_refs[k][t]), buf_ref.at[k, pl.ds(t, 1), :], sem).start()

    def wait_tile(buf_ref, sem):
        for k in range(TOP_K):
            pltpu.make_async_copy(ys_ref.at[pl.ds(0, tm), :], buf_ref.at[k], sem).wait()

    @pl.when(i == 0)
    def _():
        def first(t, carry):
            start_rows(pos_refs, buf0_ref, sems.at[0], t)
            return carry
        lax.fori_loop(0, tm, first, 0)

    def step(cur_ref, cur_sem, nxt_ref, nxt_sem):
        wait_tile(cur_ref, cur_sem)

        def group(g, carry):
            for u in range(SUBLANES):
                start_rows(posn_refs, nxt_ref, nxt_sem, g * SUBLANES + u)
            rows = pl.ds(pl.multiple_of(g * SUBLANES, SUBLANES), SUBLANES)
            w = wt_ref[rows, :]
            wb = [jnp.broadcast_to(w[:, k:k + 1], (SUBLANES, LANES)) for k in range(TOP_K)]
            for j in range(d // LANES):
                cols = slice(j * LANES, (j + 1) * LANES)
                acc = sh_ref[rows, cols]
                for k in range(TOP_K):
                    acc = acc + wb[k] * cur_ref[k, rows, cols]
                f_ref[rows, cols] = acc
            return carry

        lax.fori_loop(0, tm // SUBLANES, group, 0)

        @pl.when(i == last)
        def _():
            wait_tile(nxt_ref, nxt_sem)

    pl.when(i % 2 == 0)(functools.partial(step, buf0_ref, sems.at[0], buf1_ref, sems.at[1]))
    pl.when(i % 2 == 1)(functools.partial(step, buf1_ref, sems.at[1], buf0_ref, sems.at[0]))
    f = f_ref[...]
    x2 = x1_ref[...] + mod_ref[pl.ds(MOD_GATE_FFN, 1), :] * _rms(f, gpost_ref[...])
    if final:
        in_ctx = pl.program_id(0) < n_first

        @pl.when(in_ctx)
        def _():
            yctx_ref[...] = x2

        @pl.when(jnp.logical_not(in_ctx))
        def _():
            ylat_ref[...] = x2
    else:
        x2_ref[...] = x2
        hn = (_rms(x2, gnext_ref[...]) * (1.0 + modn_ref[pl.ds(MOD_SCALE_MIX, 1), :])
              + modn_ref[pl.ds(MOD_SHIFT_MIX, 1), :])
        hn_ref[...] = hn.astype(hn_ref.dtype)


def _combine(ys, pos, wt, shared, x1, g_post, mod, n_ctx, dec_seq, g_next=None, mod_next=None):
    t, d = x1.shape
    tm = min(GATHER_TILE, t)
    n_steps = t // tm
    final = g_next is None
    row = lambda i: (i, 0)
    fixed = lambda i: (0, 0)
    cond = lambda i: (_cond_row(i * tm, n_ctx, dec_seq), 0, 0)
    in_specs = _slot_specs(t, tm, lambda i: i) + _slot_specs(t, tm, lambda i: jnp.minimum(i + 1, n_steps - 1)) + [
        pl.BlockSpec(memory_space=pl.ANY),
        pl.BlockSpec((tm, TOP_K), row),
        pl.BlockSpec((tm, d), row),
        pl.BlockSpec((tm, d), row),
        pl.BlockSpec((1, d), fixed),
        pl.BlockSpec((None, MOD_ROWS, d), cond),
    ]
    args = [pos] * (2 * TOP_K) + [ys, wt, shared, x1, g_post.reshape(1, d), mod]
    if final:
        n0, n1 = n_ctx // tm, (t - n_ctx) // tm
        out_specs = [pl.BlockSpec((tm, d), lambda i: (jnp.minimum(i, n0 - 1), 0)),
                     pl.BlockSpec((tm, d), lambda i: (jnp.clip(i - n0, 0, n1 - 1), 0))]
        out_shape = [jax.ShapeDtypeStruct((n_ctx, d), _f32), jax.ShapeDtypeStruct((t - n_ctx, d), _f32)]
    else:
        in_specs += [pl.BlockSpec((1, d), fixed), pl.BlockSpec((None, MOD_ROWS, d), cond)]
        args += [g_next.reshape(1, d), mod_next]
        out_specs = [pl.BlockSpec((tm, d), row), pl.BlockSpec((tm, d), row)]
        out_shape = [jax.ShapeDtypeStruct((t, d), _f32), jax.ShapeDtypeStruct((t, d), _bf16)]
    return pl.pallas_call(
        functools.partial(_combine_kernel, n_first=n_ctx // tm if final else None),
        grid=(n_steps,),
        in_specs=in_specs,
        out_specs=out_specs,
        out_shape=out_shape,
        scratch_shapes=[pltpu.VMEM((TOP_K, tm, d), _f32), pltpu.VMEM((TOP_K, tm, d), _f32), pltpu.VMEM((tm, d), _f32),
                        pltpu.SemaphoreType.DMA((2,))],
        compiler_params=_params(("arbitrary",)),
        name="combine",
    )(*args)


def _moe(hf, idx_t, wt_t, rank_t, cnt, layer, w_gate, w_up, w_down, w_sh_gate, w_sh_up, w_sh_down):
    n_exp, d = w_gate.shape[1:3]
    t = hf.shape[0]
    tm = EXPERT_TILE
    n_tiles = t * TOP_K // tm + n_exp
    pos, *tiles, padrow, padn = _plan(cnt[:, 0], idx_t, rank_t, n_tiles)
    pos = pos.reshape(-1)
    xs = _dispatch(hf, pos, padrow, padn, n_tiles * tm)
    ys = _experts(xs, w_gate, w_up, w_down, layer, *tiles)
    n_sh = t // tm
    every = jnp.arange(n_sh, dtype=jnp.int32)
    zeros = jnp.zeros((n_sh,), jnp.int32)
    shared = _experts(hf, w_sh_gate[:, None], w_sh_up[:, None], w_sh_down[:, None], layer,
                      zeros, every, every, jnp.full((n_sh,), tm, jnp.int32),
                      jnp.full((n_sh,), NOT_FIRST, jnp.int32).at[0].set(NO_NEXT), zeros)
    return ys, pos, wt_t.T, shared


def kernel(x_prompt, x_sample, cache_k, cache_v, c, c_ctx, w_ada, b_ada, norm_mix_pre, norm_mix_post, norm_ffn_pre, norm_ffn_post, w_qkv, w_o_attn, rpb, w_conv_in, conv_w, w_conv_out, w_router, b_router, w_exp_gate, w_exp_up, w_exp_down, w_sh_gate, w_sh_up, w_sh_down):
    batch, seq, d = x_prompt.shape
    dec_batch, dec_seq, _ = x_sample.shape
    depth = w_ada.shape[0]
    n_ctx, n_lat = batch * seq, dec_batch * dec_seq
    dh = d // N_HEADS
    past = cache_k.shape[2]

    x = [x_prompt.reshape(n_ctx, d), x_sample.reshape(n_lat, d)]
    cond = jnp.concatenate([c_ctx[None, :], c], axis=0)
    mod = _adaln(cond, w_ada, b_ada)

    new_k, new_v = [], []
    h = _modulate(x, norm_mix_pre[0], mod[0], n_ctx, dec_seq)
    for l in range(depth):
        if l % 2 == 0:
            a = l // 2
            qkv = _matmul(h, w_qkv, a)
            o_ctx, k_new, v_new = _ctx_attention(qkv, batch, seq, d)
            new_k.append(k_new.reshape(batch, seq, N_HEADS, dh))
            new_v.append(v_new.reshape(batch, seq, N_HEADS, dh))
            o_lat = _natten(qkv, n_ctx, cache_k[:, a].reshape(dec_batch, past, d),
                            cache_v[:, a].reshape(dec_batch, past, d), rpb[a], dec_batch, dec_seq, d)
            mixed, w_out, w_layer = [o_ctx, o_lat], w_o_attn, a
        else:
            m = l // 2
            mixed = [_conv_gate(_matmul(h, w_conv_in, m), conv_w[m], n_ctx, seq, dec_seq, d)]
            w_out, w_layer = w_conv_out, m
        x1, hf, logits = _post_mixer(
            mixed, w_out, w_layer, x, norm_mix_post[l], mod[l], norm_ffn_pre[l], w_router[l], n_ctx, dec_seq)
        idx_t, wt_t, rank_t, cnt = _router(logits, b_router[l])
        ys, pos, wt, shared = _moe(hf, idx_t, wt_t, rank_t, cnt, l, w_exp_gate, w_exp_up, w_exp_down,
                                   w_sh_gate, w_sh_up, w_sh_down)
        if l + 1 < depth:
            x2, h = _combine(ys, pos, wt, shared, x1, norm_ffn_post[l], mod[l], n_ctx, dec_seq,
                             norm_mix_pre[l + 1], mod[l + 1])
            x = [x2]
        else:
            y_ctx, y_lat = _combine(ys, pos, wt, shared, x1, norm_ffn_post[l], mod[l], n_ctx, dec_seq)

    return (y_ctx.reshape(batch, seq, d), y_lat.reshape(dec_batch, dec_seq, d),
            jnp.stack(new_k, axis=1), jnp.stack(new_v, axis=1))
```

```python
import functools

import jax
import jax.numpy as jnp
from jax import lax
from jax.experimental import pallas as pl
from jax.experimental.pallas import tpu as pltpu

N_HEADS = 16
GRID_W = 64
WIN_ROWS = 8
WIN_COLS = 16
CONV_WIDTH = 3
N_EXPERTS = 64
TOP_K = 8
N_GROUPS = 8
TOPK_GROUPS = 4
ROUTED_SCALE = 2.5
N_MOD = 6
RMS_EPS = 1e-6
NEG_INF = -1e30

LANES = 128
SUBLANES = 8
VMEM_LIMIT = 56 * 1024 * 1024

MOD_SHIFT_MIX, MOD_SCALE_MIX, MOD_GATE_MIX, MOD_SHIFT_FFN, MOD_SCALE_FFN, MOD_GATE_FFN = range(6)
MOD_ROWS = 8

ROW_TILE = 256
ROUTER_TILE = 1024
EXPERT_TILE = 512
MM_TILE_M = 2048
MM_TILE_N = 512
GATHER_TILE = 128
SCATTER_TILE = 1024

_f32 = jnp.float32
_bf16 = jnp.bfloat16


def _params(sem, vmem=VMEM_LIMIT):
    return pltpu.CompilerParams(dimension_semantics=sem, vmem_limit_bytes=vmem)


def _rms(x, g):
    return x * lax.rsqrt(jnp.mean(x * x, axis=-1, keepdims=True) + RMS_EPS) * g


def _dot(a, b):
    return jnp.dot(a, b, preferred_element_type=_f32)


def _dot_nt(a, b):
    return lax.dot_general(a, b, (((1,), (1,)), ((), ())), preferred_element_type=_f32)


def _cond_row(row0, n_ctx, dec_seq):
    return jnp.where(row0 < n_ctx, 0, 1 + (row0 - n_ctx) // dec_seq)


def _adaln_kernel(cb_ref, w_ref, b_ref, o_ref, acc_ref, *, n_cond):
    k = pl.program_id(2)

    @pl.when(k == 0)
    def _():
        acc_ref[...] = jnp.zeros_like(acc_ref)

    tk, tn = w_ref.shape
    kb = min(LANES, tk)
    for k0 in range(0, tk, kb):
        s = []
        for r in range(n_cond):
            v = cb_ref[r, k0:k0 + kb, :]
            s.append(v * jax.nn.sigmoid(v))
        for c in range(tn // LANES):
            cols = slice(c * LANES, (c + 1) * LANES)
            w = w_ref[k0:k0 + kb, cols]
            for r in range(n_cond):
                acc_ref[r, :, cols] += (w * s[r]).reshape(kb // SUBLANES, SUBLANES, LANES).sum(axis=0)

    @pl.when(k == pl.num_programs(2) - 1)
    def _():
        o_ref[...] = jnp.zeros_like(o_ref)
        for r in range(n_cond):
            o_ref[pl.ds(r, 1), :] = acc_ref[r].sum(axis=0, keepdims=True) + b_ref[...]


def _adaln(cond, w_ada, b_ada):
    n_cond, d = cond.shape
    n_layers, _, n6 = w_ada.shape
    tk, tn = min(1024, d), min(2048, n6)
    cb = jnp.broadcast_to(cond[:, :, None], (n_cond, d, LANES))
    mod = pl.pallas_call(
        functools.partial(_adaln_kernel, n_cond=n_cond),
        grid=(n_layers, n6 // tn, d // tk),
        in_specs=[
            pl.BlockSpec((n_cond, tk, LANES), lambda l, n, k: (0, k, 0)),
            pl.BlockSpec((None, tk, tn), lambda l, n, k: (l, k, n)),
            pl.BlockSpec((None, 1, tn), lambda l, n, k: (l, 0, n)),
        ],
        out_specs=pl.BlockSpec((None, MOD_ROWS, tn), lambda l, n, k: (l, 0, n)),
        out_shape=jax.ShapeDtypeStruct((n_layers, MOD_ROWS, n6), _f32),
        scratch_shapes=[pltpu.VMEM((n_cond, SUBLANES, tn), _f32)],
        compiler_params=_params(("arbitrary", "arbitrary", "arbitrary")),
        name="adaln",
    )(cb, w_ada, b_ada.reshape(n_layers, 1, n6))
    mod = mod[:, :n_cond].reshape(n_layers, n_cond, N_MOD, d)
    return jnp.pad(mod, ((0, 0), (0, 0), (0, MOD_ROWS - N_MOD), (0, 0)))


def _split_specs(parts, tm, width):
    if len(parts) == 1:
        return [pl.BlockSpec((tm, width), lambda i: (i, 0))]
    n0 = parts[0].shape[0] // tm
    n1 = parts[1].shape[0] // tm
    return [pl.BlockSpec((tm, width), lambda i: (jnp.minimum(i, n0 - 1), 0)),
            pl.BlockSpec((tm, width), lambda i: (jnp.clip(i - n0, 0, n1 - 1), 0))]


def _split_read(refs, n_first):
    if len(refs) == 1:
        return refs[0][...]
    return jnp.where(pl.program_id(0) < n_first, refs[0][...], refs[1][...])


def _modulate_kernel(*refs, n_x, n_first):
    x_refs, (g_ref, mod_ref, o_ref) = refs[:n_x], refs[n_x:]
    y = _rms(_split_read(x_refs, n_first), g_ref[...])
    h = y * (1.0 + mod_ref[pl.ds(MOD_SCALE_MIX, 1), :]) + mod_ref[pl.ds(MOD_SHIFT_MIX, 1), :]
    o_ref[...] = h.astype(o_ref.dtype)


def _modulate(xs, g, mod, n_ctx, dec_seq):
    t = sum(x.shape[0] for x in xs)
    d = xs[0].shape[1]
    tm = ROW_TILE
    return pl.pallas_call(
        functools.partial(_modulate_kernel, n_x=len(xs), n_first=xs[0].shape[0] // tm),
        grid=(t // tm,),
        in_specs=_split_specs(xs, tm, d) + [
            pl.BlockSpec((1, d), lambda i: (0, 0)),
            pl.BlockSpec((None, MOD_ROWS, d), lambda i: (_cond_row(i * tm, n_ctx, dec_seq), 0, 0)),
        ],
        out_specs=pl.BlockSpec((tm, d), lambda i: (i, 0)),
        out_shape=jax.ShapeDtypeStruct((t, d), _bf16),
        compiler_params=_params(("parallel",)),
        name="modulate",
    )(*xs, g.reshape(1, d), mod)


def _matmul_kernel(a_ref, w_ref, o_ref):
    o_ref[...] = _dot(a_ref[...], w_ref[...].astype(_bf16)).astype(o_ref.dtype)


def _matmul(a, w, layer):
    n_rows = a.shape[0]
    _, k, n = w.shape
    tm, tn = min(MM_TILE_M, n_rows), min(MM_TILE_N, n)
    return pl.pallas_call(
        _matmul_kernel,
        grid=(n_rows // tm, n // tn),
        in_specs=[
            pl.BlockSpec((tm, k), lambda i, j: (i, 0)),
            pl.BlockSpec((None, k, tn), lambda i, j: (layer, 0, j)),
        ],
        out_specs=pl.BlockSpec((tm, tn), lambda i, j: (i, j)),
        out_shape=jax.ShapeDtypeStruct((n_rows, n), _f32),
        compiler_params=_params(("parallel", "parallel")),
        name="matmul",
    )(a, w)


def _ctx_attn_kernel(q_ref, k_ref, v_ref, o_ref, kout_ref, vout_ref, *, n_heads, scale):
    seq = q_ref.shape[0]
    dh = q_ref.shape[1] // n_heads
    for h in range(n_heads):
        sl = slice(h * dh, (h + 1) * dh)
        k32 = k_ref[:, sl]
        v32 = v_ref[:, sl]
        kout_ref[pl.ds(h, seq, stride=n_heads), :] = k32
        vout_ref[pl.ds(h, seq, stride=n_heads), :] = v32
        q = q_ref[:, sl].astype(_bf16)
        s = _dot_nt(q, k32.astype(_bf16)) * scale
        p = jnp.exp(s - s.max(axis=-1, keepdims=True))
        o = _dot(p.astype(_bf16), v32.astype(_bf16)) / p.sum(axis=-1, keepdims=True)
        o_ref[:, sl] = o.astype(o_ref.dtype)


def _ctx_attention(qkv, batch, seq, d):
    dh = d // N_HEADS
    assert dh == LANES
    cache = jax.ShapeDtypeStruct((batch * seq * N_HEADS, dh), _f32)
    return pl.pallas_call(
        functools.partial(_ctx_attn_kernel, n_heads=N_HEADS, scale=dh ** -0.5),
        grid=(batch,),
        in_specs=[
            pl.BlockSpec((seq, d), lambda b: (b, 0)),
            pl.BlockSpec((seq, d), lambda b: (b, 1)),
            pl.BlockSpec((seq, d), lambda b: (b, 2)),
        ],
        out_specs=[
            pl.BlockSpec((seq, d), lambda b: (b, 0)),
            pl.BlockSpec((seq * N_HEADS, dh), lambda b: (b, 0)),
            pl.BlockSpec((seq * N_HEADS, dh), lambda b: (b, 0)),
        ],
        out_shape=[jax.ShapeDtypeStruct((batch * seq, d), _bf16), cache, cache],
        compiler_params=_params(("parallel",)),
        name="ctx_attention",
    )(qkv, qkv, qkv)


def _natten_kernel(q_ref, k_ref, v_ref, kc_ref, vc_ref, rpb_ref, o_ref, qb_ref, kb_ref, vb_ref,
                   sctx_ref, pctx_ref, oloc_ref, den_ref, blo_ref, bhi_ref, *, rows, width, kh, scale, unroll):
    n_loc = kh * width
    q_col = lax.broadcasted_iota(jnp.int32, (width, n_loc), 0)
    k_col = lax.broadcasted_iota(jnp.int32, (width, n_loc), 1) % width
    col_start = jnp.clip(q_col - WIN_COLS // 2, 0, width - WIN_COLS)
    col_mask = (k_col >= col_start) & (k_col < col_start + WIN_COLS)
    low_half = lax.broadcasted_iota(jnp.int32, (width, LANES), 1) < width

    for dr in range(rpb_ref.shape[0]):
        row = jnp.broadcast_to(rpb_ref[pl.ds(dr, 1), :], (width, LANES))
        blo_ref[dr] = pltpu.roll(row, LANES - (WIN_COLS - 1), axis=1, stride=1, stride_axis=0)
        bhi_ref[dr] = pltpu.roll(row, (LANES - (WIN_COLS - 1) + width) % LANES, axis=1, stride=1, stride_axis=0)

    def bias_of(delta):
        cols = []
        for i in range(0, kh, 2):
            dr = i - delta + (WIN_ROWS - 1)
            cols.append(jnp.where(low_half, blo_ref[dr], bhi_ref[dr + 1]))
        return jnp.concatenate(cols, axis=1)

    qb_ref[...] = q_ref[...].astype(_bf16)
    kb_ref[...] = k_ref[...].astype(_bf16)
    vb_ref[...] = v_ref[...].astype(_bf16)
    sctx_ref[...] = _dot_nt(qb_ref[...], kc_ref[...].astype(_bf16)) * scale

    def one_row(r):
        r0 = jnp.clip(r - kh // 2, 0, rows - kh)
        qrows = pl.ds(pl.multiple_of(r * width, width), width)
        win = pl.ds(pl.multiple_of(r0 * width, width), n_loc)
        s_loc = _dot_nt(qb_ref[qrows, :], kb_ref[win, :]) * scale + bias_of(r - r0)
        s_loc = jnp.where(col_mask, s_loc, NEG_INF)
        s_ctx = sctx_ref[qrows, :]
        m = jnp.maximum(s_loc.max(axis=-1, keepdims=True), s_ctx.max(axis=-1, keepdims=True))
        p_loc = jnp.exp(s_loc - m)
        p_ctx = jnp.exp(s_ctx - m)
        den = p_loc.sum(axis=-1, keepdims=True) + p_ctx.sum(axis=-1, keepdims=True)
        pctx_ref[qrows, :] = p_ctx.astype(_bf16)
        oloc_ref[qrows, :] = _dot(p_loc.astype(_bf16), vb_ref[win, :])
        den_ref[qrows, :] = jnp.broadcast_to(den, (width, den_ref.shape[1]))

    def some_rows(g, carry):
        for u in range(unroll):
            one_row(g * unroll + u)
        return carry

    lax.fori_loop(0, rows // unroll, some_rows, 0)
    o = oloc_ref[...] + _dot(pctx_ref[...], vc_ref[...].astype(_bf16))
    o_ref[...] = (o / den_ref[...]).astype(o_ref.dtype)


def _natten(qkv, n_ctx, k_ctx, v_ctx, rpb, dec_batch, dec_seq, d):
    assert n_ctx % dec_seq == 0
    dh = d // N_HEADS
    rows = dec_seq // GRID_W
    kh = min(WIN_ROWS, rows)
    assert 2 * GRID_W == LANES and kh % 2 == 0 and 2 * WIN_COLS - 1 <= LANES
    past = k_ctx.shape[1]
    n_dr = rpb.shape[1]
    rpb = jnp.pad(rpb.astype(_f32), ((0, 0), (0, 0), (0, LANES - rpb.shape[2])))
    b0 = n_ctx // dec_seq
    unroll = next(u for u in (8, 4, 2, 1) if rows % u == 0)
    return pl.pallas_call(
        functools.partial(_natten_kernel, rows=rows, width=GRID_W, kh=kh, scale=dh ** -0.5, unroll=unroll),
        grid=(dec_batch, N_HEADS),
        in_specs=[
            pl.BlockSpec((dec_seq, dh), lambda b, h: (b0 + b, h)),
            pl.BlockSpec((dec_seq, dh), lambda b, h: (b0 + b, N_HEADS + h)),
            pl.BlockSpec((dec_seq, dh), lambda b, h: (b0 + b, 2 * N_HEADS + h)),
            pl.BlockSpec((None, past, dh), lambda b, h: (b, 0, h)),
            pl.BlockSpec((None, past, dh), lambda b, h: (b, 0, h)),
            pl.BlockSpec((None, n_dr, LANES), lambda b, h: (h, 0, 0)),
        ],
        out_specs=pl.BlockSpec((dec_seq, dh), lambda b, h: (b, h)),
        out_shape=jax.ShapeDtypeStruct((dec_batch * dec_seq, d), _bf16),
        scratch_shapes=[pltpu.VMEM((dec_seq, dh), _bf16), pltpu.VMEM((dec_seq, dh), _bf16),
                        pltpu.VMEM((dec_seq, dh), _bf16),
                        pltpu.VMEM((dec_seq, past), _f32), pltpu.VMEM((dec_seq, past), _bf16),
                        pltpu.VMEM((dec_seq, dh), _f32), pltpu.VMEM((dec_seq, dh), _f32),
                        pltpu.VMEM((n_dr, GRID_W, LANES), _f32), pltpu.VMEM((n_dr, GRID_W, LANES), _f32)],
        compiler_params=_params(("parallel", "parallel")),
        name="natten",
    )(qkv, qkv, qkv, k_ctx, v_ctx, rpb)


def _conv_gate_kernel(b_ref, c_ref, u_ref, cp_ref, up_ref, cn_ref, un_ref, w_ref, o_ref,
                      *, n_ctx, seq, dec_seq):
    tm = o_ref.shape[0]
    row0 = pl.program_id(0) * tm
    in_ctx = row0 < n_ctx
    pos0 = jnp.where(in_ctx, row0 % seq, (row0 - n_ctx) % dec_seq)
    seq_len = jnp.where(in_ctx, seq, dec_seq)
    has_prev = pos0 > 0
    has_next = pos0 + tm < seq_len
    cu = c_ref[...] * u_ref[...]
    prev_row = jnp.where(has_prev, cp_ref[pl.ds(SUBLANES - 1, 1), :] * up_ref[pl.ds(SUBLANES - 1, 1), :], 0.0)
    next_row = jnp.where(has_next, cn_ref[pl.ds(0, 1), :] * un_ref[pl.ds(0, 1), :], 0.0)
    ridx = lax.broadcasted_iota(jnp.int32, cu.shape, 0)
    before = jnp.where(ridx == 0, prev_row, pltpu.roll(cu, 1, axis=0))
    after = jnp.where(ridx == tm - 1, next_row, pltpu.roll(cu, tm - 1, axis=0))
    conv = w_ref[pl.ds(0, 1), :] * before + w_ref[pl.ds(1, 1), :] * cu + w_ref[pl.ds(2, 1), :] * after
    o_ref[...] = (b_ref[...] * conv).astype(o_ref.dtype)


def _conv_gate(bcu, conv_w, n_ctx, seq, dec_seq, d):
    t = bcu.shape[0]
    tm = ROW_TILE
    halo = tm // SUBLANES
    last = t // SUBLANES - 1
    cw = jnp.pad(conv_w, ((0, SUBLANES - CONV_WIDTH), (0, 0)))
    prev_map = lambda col: (lambda i: (jnp.maximum(i * halo - 1, 0), col))
    next_map = lambda col: (lambda i: (jnp.minimum((i + 1) * halo, last), col))
    return pl.pallas_call(
        functools.partial(_conv_gate_kernel, n_ctx=n_ctx, seq=seq, dec_seq=dec_seq),
        grid=(t // tm,),
        in_specs=[
            pl.BlockSpec((tm, d), lambda i: (i, 0)),
            pl.BlockSpec((tm, d), lambda i: (i, 1)),
            pl.BlockSpec((tm, d), lambda i: (i, 2)),
            pl.BlockSpec((SUBLANES, d), prev_map(1)),
            pl.BlockSpec((SUBLANES, d), prev_map(2)),
            pl.BlockSpec((SUBLANES, d), next_map(1)),
            pl.BlockSpec((SUBLANES, d), next_map(2)),
            pl.BlockSpec((SUBLANES, d), lambda i: (0, 0)),
        ],
        out_specs=pl.BlockSpec((tm, d), lambda i: (i, 0)),
        out_shape=jax.ShapeDtypeStruct((t, d), _bf16),
        compiler_params=_params(("parallel",)),
        name="conv_gate",
    )(bcu, bcu, bcu, bcu, bcu, bcu, bcu, cw)


def _route(sel, scores):
    n_grp, eg, tm = sel.shape
    n_exp = n_grp * eg
    j_iota = lax.broadcasted_iota(jnp.int32, sel.shape, 1)
    m1 = sel.max(axis=1, keepdims=True)
    j1 = jnp.min(jnp.where(sel == m1, j_iota, eg), axis=1, keepdims=True)
    m2 = jnp.max(jnp.where(j_iota == j1, -jnp.inf, sel), axis=1, keepdims=True)
    grp = m1 + m2
    g_iota = lax.broadcasted_iota(jnp.int32, grp.shape, 0)
    g_sel = g_iota < 0
    for _ in range(TOPK_GROUPS):
        gm = grp.max(axis=0, keepdims=True)
        gi = jnp.min(jnp.where(grp == gm, g_iota, n_grp), axis=0, keepdims=True)
        hit = g_iota == gi
        g_sel = g_sel | hit
        grp = jnp.where(hit, -jnp.inf, grp)
    cur = jnp.where(jnp.broadcast_to(g_sel, sel.shape), sel, NEG_INF)
    e_iota = lax.broadcasted_iota(jnp.int32, sel.shape, 0) * eg + j_iota
    ids, ws, hits = [], [], []
    for _ in range(TOP_K):
        m = cur.max(axis=1, keepdims=True).max(axis=0, keepdims=True)
        ei = jnp.min(jnp.where(cur == m, e_iota, n_exp), axis=1, keepdims=True).min(axis=0, keepdims=True)
        hit = e_iota == ei
        ids.append(ei)
        ws.append(jnp.sum(jnp.where(hit, scores, 0.0), axis=1, keepdims=True).sum(axis=0, keepdims=True))
        hits.append(hit)
        cur = jnp.where(hit, -jnp.inf, cur)
    total = functools.reduce(lambda a, b: a + b, ws)
    ws = [w / total * ROUTED_SCALE for w in ws]
    return ids, ws, hits


def _split_bf16(x):
    hi = x.astype(_bf16)
    return hi, (x - hi.astype(_f32)).astype(_bf16)


def _post_mixer_kernel(*refs, n_a, n_x, n_first):
    a_refs, x_refs = refs[:n_a], refs[n_a:n_a + n_x]
    wo_ref, gpost_ref, mod_ref, gpre_ref, wr_ref, x1_ref, hf_ref, logit_ref = refs[n_a + n_x:]
    o = _dot(_split_read(a_refs, n_first), wo_ref[...])
    x1 = _split_read(x_refs, n_first) + mod_ref[pl.ds(MOD_GATE_MIX, 1), :] * _rms(o, gpost_ref[...])
    x1_ref[...] = x1
    hf = _rms(x1, gpre_ref[...]) * (1.0 + mod_ref[pl.ds(MOD_SCALE_FFN, 1), :]) + mod_ref[pl.ds(MOD_SHIFT_FFN, 1), :]
    hf_ref[...] = hf
    h_hi, h_lo = _split_bf16(hf)
    w_hi, w_lo = _split_bf16(wr_ref[...])
    logit_ref[...] = _dot(h_hi, w_hi) + (_dot(h_hi, w_lo) + _dot(h_lo, w_hi))


def _post_mixer(a_parts, w_out, layer, x_parts, g_post, mod, g_pre, w_router, n_ctx, dec_seq):
    t = sum(x.shape[0] for x in x_parts)
    d = x_parts[0].shape[1]
    tm = ROW_TILE
    e = w_router.shape[1]
    assert e <= LANES
    row = lambda i: (i, 0)
    fixed = lambda i: (0, 0)
    return pl.pallas_call(
        functools.partial(_post_mixer_kernel, n_a=len(a_parts), n_x=len(x_parts), n_first=n_ctx // tm),
        grid=(t // tm,),
        in_specs=_split_specs(a_parts, tm, d) + _split_specs(x_parts, tm, d) + [
            pl.BlockSpec((None, d, d), lambda i: (layer, 0, 0)),
            pl.BlockSpec((1, d), fixed),
            pl.BlockSpec((None, MOD_ROWS, d), lambda i: (_cond_row(i * tm, n_ctx, dec_seq), 0, 0)),
            pl.BlockSpec((1, d), fixed),
            pl.BlockSpec((d, LANES), fixed),
        ],
        out_specs=[pl.BlockSpec((tm, d), row), pl.BlockSpec((tm, d), row), pl.BlockSpec((tm, LANES), row)],
        out_shape=[jax.ShapeDtypeStruct((t, d), _f32), jax.ShapeDtypeStruct((t, d), _f32),
                   jax.ShapeDtypeStruct((t, LANES), _f32)],
        compiler_params=_params(("parallel",)),
        name="post_mixer",
    )(*a_parts, *x_parts, w_out.astype(_bf16), g_post.reshape(1, d), mod, g_pre.reshape(1, d),
      jnp.pad(w_router, ((0, 0), (0, LANES - e))))


def _router_kernel(logit_ref, br_ref, idx_ref, wt_ref, rank_ref, cnt_ref, carry_ref, before_ref):
    i = pl.program_id(0)
    tm = logit_ref.shape[0]
    n_exp = br_ref.shape[0]

    @pl.when(i == 0)
    def _():
        carry_ref[...] = jnp.zeros_like(carry_ref)
        t_src = lax.broadcasted_iota(jnp.int32, (tm, tm), 0)
        t_dst = lax.broadcasted_iota(jnp.int32, (tm, tm), 1)
        before_ref[...] = jnp.where(t_src < t_dst, 1.0, 0.0).astype(_bf16)

    logits = logit_ref[...].T[:n_exp]
    grouped = (N_GROUPS, n_exp // N_GROUPS, tm)
    scores = jax.nn.sigmoid(logits)
    ids, ws, hits = _route((scores + br_ref[...]).reshape(grouped), scores.reshape(grouped))
    for k in range(TOP_K):
        idx_ref[pl.ds(k, 1), :] = ids[k][0]
        wt_ref[pl.ds(k, 1), :] = ws[k][0]

    any_hit = functools.reduce(lambda a, b: a | b, hits)
    mask = jnp.where(any_hit, 1.0, 0.0).reshape(n_exp, tm).astype(_bf16)
    rank = _dot(mask, before_ref[...]) + jnp.concatenate([carry_ref[...]] * (tm // LANES), axis=1)
    rank = rank.reshape(grouped)
    for k in range(TOP_K):
        rk = jnp.sum(jnp.where(hits[k], rank, 0.0), axis=1, keepdims=True).sum(axis=0, keepdims=True)
        rank_ref[pl.ds(k, 1), :] = rk[0].astype(jnp.int32)
    carry_ref[...] += _dot(mask, jnp.ones((tm, LANES), _bf16))
    cnt_ref[...] = carry_ref[...].astype(jnp.int32)


def _router(logits, b_router):
    t = logits.shape[0]
    e = b_router.shape[0]
    tm = min(ROUTER_TILE, t)
    col = lambda i: (0, i)
    fixed = lambda i: (0, 0)
    slots_i = jax.ShapeDtypeStruct((TOP_K, t), jnp.int32)
    return pl.pallas_call(
        _router_kernel,
        grid=(t // tm,),
        in_specs=[pl.BlockSpec((tm, LANES), lambda i: (i, 0)), pl.BlockSpec((e, 1), fixed)],
        out_specs=[pl.BlockSpec((TOP_K, tm), col), pl.BlockSpec((TOP_K, tm), col), pl.BlockSpec((TOP_K, tm), col),
                   pl.BlockSpec((e, LANES), fixed)],
        out_shape=[slots_i, jax.ShapeDtypeStruct((TOP_K, t), _f32), slots_i,
                   jax.ShapeDtypeStruct((e, LANES), jnp.int32)],
        scratch_shapes=[pltpu.VMEM((e, LANES), _f32), pltpu.VMEM((tm, tm), _bf16)],
        compiler_params=_params(("arbitrary",)),
        name="router",
    )(logits, b_router.reshape(e, 1))


def _row(ref, r):
    return ref.at[pl.ds(r, 1), :]


NOT_FIRST = -2
NO_NEXT = -1


def _plan_kernel(cnt_ref, idx_ref, rank_ref, pos_ref, te_ref, tbi_ref, tbo_ref, tv_ref, tnext_ref, trun_ref,
                 padrow_ref, padn_ref, offs_ref, *, tm):
    n_exp = cnt_ref.shape[0]
    n_tiles = te_ref.shape[0]

    def per_expert(e, carry):
        off, tile, run, prev_first = carry
        cnt = cnt_ref[e]
        n_t = (cnt + tm - 1) // tm
        offs_ref[e] = off
        padrow_ref[e] = off + cnt
        padn_ref[e] = n_t * tm - cnt

        def per_tile(j, c):
            te_ref[tile + j] = e
            tbi_ref[tile + j] = tile + j
            tbo_ref[tile + j] = tile + j
            tv_ref[tile + j] = jnp.minimum(cnt - j * tm, tm)
            tnext_ref[tile + j] = jnp.where(j == 0, NO_NEXT, NOT_FIRST)
            trun_ref[tile + j] = run
            return c

        lax.fori_loop(0, n_t, per_tile, 0)
        used = n_t > 0

        @pl.when(used & (prev_first >= 0))
        def _():
            tnext_ref[prev_first] = e

        return (off + n_t * tm, tile + n_t, run + used.astype(jnp.int32), jnp.where(used, tile, prev_first))

    zero = jnp.int32(0)
    _, live, _, _ = lax.fori_loop(0, n_exp, per_expert, (zero, zero, zero, jnp.int32(-1)))
    last = jnp.maximum(live - 1, 0)
    last_expert = te_ref[last]

    def dead_tile(i, c):
        te_ref[i] = last_expert
        tbi_ref[i] = last
        tbo_ref[i] = last
        tv_ref[i] = 0
        tnext_ref[i] = NOT_FIRST
        trun_ref[i] = 0
        return c

    lax.fori_loop(live, n_tiles, dead_tile, 0)

    idx = idx_ref[...]
    pos = rank_ref[...]
    for e in range(n_exp):
        pos = pos + jnp.where(idx == e, offs_ref[e], 0)
    pos_ref[...] = pos


def _plan(counts, idx_t, rank_t, n_tiles):
    n_exp = counts.shape[0]
    smem = pl.BlockSpec(memory_space=pltpu.SMEM)
    vmem = pl.BlockSpec(memory_space=pltpu.VMEM)
    tiles = jax.ShapeDtypeStruct((n_tiles,), jnp.int32)
    experts = jax.ShapeDtypeStruct((n_exp,), jnp.int32)
    return pl.pallas_call(
        functools.partial(_plan_kernel, tm=EXPERT_TILE),
        in_specs=[smem, vmem, vmem],
        out_specs=[vmem] + [smem] * 8,
        out_shape=[jax.ShapeDtypeStruct(idx_t.shape, jnp.int32)] + [tiles] * 6 + [experts] * 2,
        scratch_shapes=[pltpu.SMEM((n_exp,), jnp.int32)],
        name="plan",
    )(counts, idx_t, rank_t)


def _pad_copies(start, n, zero_ref, dst_ref, sem, pad_bits):
    single = n & (SUBLANES - 1)
    for s in range(SUBLANES - 1):
        yield s < single, pltpu.make_async_copy(_row(zero_ref, 0), _row(dst_ref, start + s), sem)
    base = pl.multiple_of(start + single, SUBLANES)
    groups = n // SUBLANES
    for b in range(pad_bits - (SUBLANES.bit_length() - 1)):
        rows = SUBLANES << b
        first = pl.multiple_of(base + ((groups >> (b + 1)) << (b + 1)) * SUBLANES, SUBLANES)
        copy = pltpu.make_async_copy(zero_ref.at[pl.ds(0, rows), :], dst_ref.at[pl.ds(first, rows), :], sem)
        yield ((groups >> b) & 1) == 1, copy


def _slot_specs(n_tokens, tile, step_of):
    per_k = n_tokens // tile
    return [pl.BlockSpec((tile,), lambda i, k=k: (k * per_k + step_of(i),), memory_space=pltpu.SMEM)
            for k in range(TOP_K)]


def _dispatch_kernel(*refs, pad_bits):
    pos_refs = refs[:TOP_K]
    padrow_ref, padn_ref, src_ref, dst_ref, zero_ref, sem, zero_sem = refs[TOP_K:]
    i = pl.program_id(0)
    tile = src_ref.shape[0]
    n_exp = padn_ref.shape[0]

    def for_each_pad_copy(fn):
        def body(e, c):
            for needed, copy in _pad_copies(padrow_ref[e], padn_ref[e], zero_ref, dst_ref, zero_sem, pad_bits):
                pl.when(needed)(functools.partial(fn, copy))
            return c
        lax.fori_loop(0, n_exp, body, 0)

    @pl.when(i == 0)
    def _():
        zero_ref[...] = jnp.zeros_like(zero_ref)
        for_each_pad_copy(lambda copy: copy.start())

    def issue(g, carry):
        for t in (2 * g, 2 * g + 1):
            for k in range(TOP_K):
                pltpu.make_async_copy(_row(src_ref, t), _row(dst_ref, pos_refs[k][t]), sem).start()
        return carry

    lax.fori_loop(0, tile // 2, issue, 0)
    for k in range(TOP_K):
        pltpu.make_async_copy(src_ref, dst_ref.at[pl.ds(0, tile), :], sem).wait()

    @pl.when(i == 0)
    def _():
        for_each_pad_copy(lambda copy: copy.wait())


def _dispatch(hf, pos, padrow, padn, n_sorted):
    t, d = hf.shape
    tile = min(SCATTER_TILE, t)
    pad_bits = (EXPERT_TILE - 1).bit_length()
    smem = pl.BlockSpec(memory_space=pltpu.SMEM)
    return pl.pallas_call(
        functools.partial(_dispatch_kernel, pad_bits=pad_bits),
        grid=(t // tile,),
        in_specs=_slot_specs(t, tile, lambda i: i) + [
            smem, smem,
            pl.BlockSpec((tile, d), lambda i: (i, 0)),
        ],
        out_specs=pl.BlockSpec(memory_space=pl.ANY),
        out_shape=jax.ShapeDtypeStruct((n_sorted, d), _f32),
        scratch_shapes=[pltpu.VMEM((1 << (pad_bits - 1), d), _f32),
                        pltpu.SemaphoreType.DMA(()), pltpu.SemaphoreType.DMA(())],
        compiler_params=_params(("arbitrary",)),
        name="dispatch",
    )(*([pos] * TOP_K), padrow, padn, hf)


def _expert_kernel(te_ref, tbi_ref, tbo_ref, tv_ref, tnext_ref, trun_ref, x_ref, wg_hbm, wu_hbm, wd_hbm, o_ref,
                   wg_ref, wu_ref, wd_ref, wgb_ref, wub_ref, wdb_ref, sems, *, layer):
    i = pl.program_id(0)
    live = tv_ref[i] > 0
    slot = trun_ref[i] % 2

    def fetch(expert, s):
        return [pltpu.make_async_copy(src.at[layer, expert], dst.at[s], sems.at[s])
                for src, dst in ((wg_hbm, wg_ref), (wu_hbm, wu_ref), (wd_hbm, wd_ref))]

    @pl.when(i == 0)
    def _():
        for copy in fetch(te_ref[0], 0):
            copy.start()

    @pl.when(live & (tnext_ref[i] != NOT_FIRST))
    def _():
        @pl.when(tnext_ref[i] != NO_NEXT)
        def _():
            for copy in fetch(tnext_ref[i], 1 - slot):
                copy.start()

        for copy in fetch(te_ref[i], slot):
            copy.wait()
        wgb_ref[...] = wg_ref[slot].astype(_bf16)
        wub_ref[...] = wu_ref[slot].astype(_bf16)
        wdb_ref[...] = wd_ref[slot].astype(_bf16)

    @pl.when(live)
    def _():
        x = x_ref[...].astype(_bf16)
        g = _dot(x, wgb_ref[...])
        u = _dot(x, wub_ref[...])
        h = (g * jax.nn.sigmoid(g) * u).astype(_bf16)
        o_ref[...] = _dot(h, wdb_ref[...])


def _experts(xs, w_gate, w_up, w_down, layer, tile_expert, tile_in, tile_out, tile_valid, tile_next, tile_run):
    _, n_exp, d, f = w_gate.shape
    tm = EXPERT_TILE
    n_tiles = tile_expert.shape[0]
    hbm = pl.BlockSpec(memory_space=pl.ANY)
    return pl.pallas_call(
        functools.partial(_expert_kernel, layer=layer),
        grid_spec=pltpu.PrefetchScalarGridSpec(
            num_scalar_prefetch=6,
            grid=(n_tiles,),
            in_specs=[pl.BlockSpec((tm, d), lambda i, te, tbi, tbo, tv, tn, tr: (tbi[i], 0)), hbm, hbm, hbm],
            out_specs=pl.BlockSpec((tm, d), lambda i, te, tbi, tbo, tv, tn, tr: (tbo[i], 0)),
            scratch_shapes=[pltpu.VMEM((2, d, f), _f32), pltpu.VMEM((2, d, f), _f32), pltpu.VMEM((2, f, d), _f32),
                            pltpu.VMEM((d, f), _bf16), pltpu.VMEM((d, f), _bf16), pltpu.VMEM((f, d), _bf16),
                            pltpu.SemaphoreType.DMA((2,))],
        ),
        out_shape=jax.ShapeDtypeStruct(xs.shape, _f32),
        compiler_params=_params(("arbitrary",)),
        name="experts",
    )(tile_expert, tile_in, tile_out, tile_valid, tile_next, tile_run, xs, w_gate, w_up, w_down)


def _combine_kernel(*refs, n_first):
    pos_refs, posn_refs = refs[:TOP_K], refs[TOP_K:2 * TOP_K]
    ys_ref, wt_ref, sh_ref, x1_ref, gpost_ref, mod_ref = refs[2 * TOP_K:2 * TOP_K + 6]
    rest = refs[2 * TOP_K + 6:]
    final = n_first is not None
    if final:
        yctx_ref, ylat_ref, buf0_ref, buf1_ref, f_ref, sems = rest
    else:
        gnext_ref, modn_ref, x2_ref, hn_ref, buf0_ref, buf1_ref, f_ref, sems = rest
    tm, d = x1_ref.shape
    i = pl.program_id(0)
    last = pl.num_programs(0) - 1

    def start_rows(p_refs, buf_ref, sem, t):
        for k in range(TOP_K):
            pltpu.make_async_copy(_row(ys_ref, p_refs[k][t]), buf_ref.at[k, pl.ds(t, 1), :], sem).start()

    def wait_tile(buf_ref, sem):
        for k in range(TOP_K):
            pltpu.make_async_copy(ys_ref.at[pl.ds(0, tm), :], buf_ref.at[k], sem).wait()

    @pl.when(i == 0)
    def _():
        def first(t, carry):
            start_rows(pos_refs, buf0_ref, sems.at[0], t)
            return carry
        lax.fori_loop(0, tm, first, 0)

    def step(cur_ref, cur_sem, nxt_ref, nxt_sem):
        wait_tile(cur_ref, cur_sem)

        def group(g, carry):
            for u in range(SUBLANES):
                start_rows(posn_refs, nxt_ref, nxt_sem, g * SUBLANES + u)
            rows = pl.ds(pl.multiple_of(g * SUBLANES, SUBLANES), SUBLANES)
            w = wt_ref[rows, :]
            wb = [jnp.broadcast_to(w[:, k:k + 1], (SUBLANES, LANES)) for k in range(TOP_K)]
            for j in range(d // LANES):
                cols = slice(j * LANES, (j + 1) * LANES)
                acc = sh_ref[rows, cols]
                for k in range(TOP_K):
                    acc = acc + wb[k] * cur_ref[k, rows, cols]
                f_ref[rows, cols] = acc
            return carry

        lax.fori_loop(0, tm // SUBLANES, group, 0)

        @pl.when(i == last)
        def _():
            wait_tile(nxt_ref, nxt_sem)

    pl.when(i % 2 == 0)(functools.partial(step, buf0_ref, sems.at[0], buf1_ref, sems.at[1]))
    pl.when(i % 2 == 1)(functools.partial(step, buf1_ref, sems.at[1], buf0_ref, sems.at[0]))
    f = f_ref[...]
    x2 = x1_ref[...] + mod_ref[pl.ds(MOD_GATE_FFN, 1), :] * _rms(f, gpost_ref[...])
    if final:
        in_ctx = pl.program_id(0) < n_first

        @pl.when(in_ctx)
        def _():
            yctx_ref[...] = x2

        @pl.when(jnp.logical_not(in_ctx))
        def _():
            ylat_ref[...] = x2
    else:
        x2_ref[...] = x2
        hn = (_rms(x2, gnext_ref[...]) * (1.0 + modn_ref[pl.ds(MOD_SCALE_MIX, 1), :])
              + modn_ref[pl.ds(MOD_SHIFT_MIX, 1), :])
        hn_ref[...] = hn.astype(hn_ref.dtype)


def _combine(ys, pos, wt, shared, x1, g_post, mod, n_ctx, dec_seq, g_next=None, mod_next=None):
    t, d = x1.shape
    tm = min(GATHER_TILE, t)
    n_steps = t // tm
    final = g_next is None
    row = lambda i: (i, 0)
    fixed = lambda i: (0, 0)
    cond = lambda i: (_cond_row(i * tm, n_ctx, dec_seq), 0, 0)
    in_specs = _slot_specs(t, tm, lambda i: i) + _slot_specs(t, tm, lambda i: jnp.minimum(i + 1, n_steps - 1)) + [
        pl.BlockSpec(memory_space=pl.ANY),
        pl.BlockSpec((tm, TOP_K), row),
        pl.BlockSpec((tm, d), row),
        pl.BlockSpec((tm, d), row),
        pl.BlockSpec((1, d), fixed),
        pl.BlockSpec((None, MOD_ROWS, d), cond),
    ]
    args = [pos] * (2 * TOP_K) + [ys, wt, shared, x1, g_post.reshape(1, d), mod]
    if final:
        n0, n1 = n_ctx // tm, (t - n_ctx) // tm
        out_specs = [pl.BlockSpec((tm, d), lambda i: (jnp.minimum(i, n0 - 1), 0)),
                     pl.BlockSpec((tm, d), lambda i: (jnp.clip(i - n0, 0, n1 - 1), 0))]
        out_shape = [jax.ShapeDtypeStruct((n_ctx, d), _f32), jax.ShapeDtypeStruct((t - n_ctx, d), _f32)]
    else:
        in_specs += [pl.BlockSpec((1, d), fixed), pl.BlockSpec((None, MOD_ROWS, d), cond)]
        args += [g_next.reshape(1, d), mod_next]
        out_specs = [pl.BlockSpec((tm, d), row), pl.BlockSpec((tm, d), row)]
        out_shape = [jax.ShapeDtypeStruct((t, d), _f32), jax.ShapeDtypeStruct((t, d), _bf16)]
    return pl.pallas_call(
        functools.partial(_combine_kernel, n_first=n_ctx // tm if final else None),
        grid=(n_steps,),
        in_specs=in_specs,
        out_specs=out_specs,
        out_shape=out_shape,
        scratch_shapes=[pltpu.VMEM((TOP_K, tm, d), _f32), pltpu.VMEM((TOP_K, tm, d), _f32), pltpu.VMEM((tm, d), _f32),
                        pltpu.SemaphoreType.DMA((2,))],
        compiler_params=_params(("arbitrary",)),
        name="combine",
    )(*args)


def _moe(hf, idx_t, wt_t, rank_t, cnt, layer, w_gate, w_up, w_down, w_sh_gate, w_sh_up, w_sh_down):
    n_exp, d = w_gate.shape[1:3]
    t = hf.shape[0]
    tm = EXPERT_TILE
    n_tiles = t * TOP_K // tm + n_exp
    pos, *tiles, padrow, padn = _plan(cnt[:, 0], idx_t, rank_t, n_tiles)
    pos = pos.reshape(-1)
    xs = _dispatch(hf, pos, padrow, padn, n_tiles * tm)
    ys = _experts(xs, w_gate, w_up, w_down, layer, *tiles)
    n_sh = t // tm
    every = jnp.arange(n_sh, dtype=jnp.int32)
    zeros = jnp.zeros((n_sh,), jnp.int32)
    shared = _experts(hf, w_sh_gate[:, None], w_sh_up[:, None], w_sh_down[:, None], layer,
                      zeros, every, every, jnp.full((n_sh,), tm, jnp.int32),
                      jnp.full((n_sh,), NOT_FIRST, jnp.int32).at[0].set(NO_NEXT), zeros)
    return ys, pos, wt_t.T, shared


def kernel(x_prompt, x_sample, cache_k, cache_v, c, c_ctx, w_ada, b_ada, norm_mix_pre, norm_mix_post, norm_ffn_pre, norm_ffn_post, w_qkv, w_o_attn, rpb, w_conv_in, conv_w, w_conv_out, w_router, b_router, w_exp_gate, w_exp_up, w_exp_down, w_sh_gate, w_sh_up, w_sh_down):
    batch, seq, d = x_prompt.shape
    dec_batch, dec_seq, _ = x_sample.shape
    depth = w_ada.shape[0]
    n_ctx, n_lat = batch * seq, dec_batch * dec_seq
    dh = d // N_HEADS
    past = cache_k.shape[2]

    x = [x_prompt.reshape(n_ctx, d), x_sample.reshape(n_lat, d)]
    cond = jnp.concatenate([c_ctx[None, :], c], axis=0)
    mod = _adaln(cond, w_ada, b_ada)

    new_k, new_v = [], []
    h = _modulate(x, norm_mix_pre[0], mod[0], n_ctx, dec_seq)
    for l in range(depth):
        if l % 2 == 0:
            a = l // 2
            qkv = _matmul(h, w_qkv, a)
            o_ctx, k_new, v_new = _ctx_attention(qkv, batch, seq, d)
            new_k.append(k_new.reshape(batch, seq, N_HEADS, dh))
            new_v.append(v_new.reshape(batch, seq, N_HEADS, dh))
            o_lat = _natten(qkv, n_ctx, cache_k[:, a].reshape(dec_batch, past, d),
                            cache_v[:, a].reshape(dec_batch, past, d), rpb[a], dec_batch, dec_seq, d)
            mixed, w_out, w_layer = [o_ctx, o_lat], w_o_attn, a
        else:
            m = l // 2
            mixed = [_conv_gate(_matmul(h, w_conv_in, m), conv_w[m], n_ctx, seq, dec_seq, d)]
            w_out, w_layer = w_conv_out, m
        x1, hf, logits = _post_mixer(
            mixed, w_out, w_layer, x, norm_mix_post[l], mod[l], norm_ffn_pre[l], w_router[l], n_ctx, dec_seq)
        idx_t, wt_t, rank_t, cnt = _router(logits, b_router[l])
        ys, pos, wt, shared = _moe(hf, idx_t, wt_t, rank_t, cnt, l, w_exp_gate, w_exp_up, w_exp_down,
                                   w_sh_gate, w_sh_up, w_sh_down)
        if l + 1 < depth:
            x2, h = _combine(ys, pos, wt, shared, x1, norm_ffn_post[l], mod[l], n_ctx, dec_seq,
                             norm_mix_pre[l + 1], mod[l + 1])
            x = [x2]
        else:
            y_ctx, y_lat = _combine(ys, pos, wt, shared, x1, norm_ffn_post[l], mod[l], n_ctx, dec_seq)

    return (y_ctx.reshape(batch, seq, d), y_lat.reshape(dec_batch, dec_seq, d),
            jnp.stack(new_k, axis=1), jnp.stack(new_v, axis=1))
```

```python
import functools

import jax
import jax.numpy as jnp
from jax import lax
from jax.experimental import pallas as pl
from jax.experimental.pallas import tpu as pltpu

N_HEADS = 16
GRID_W = 64
WIN_ROWS = 8
WIN_COLS = 16
CONV_WIDTH = 3
N_EXPERTS = 64
TOP_K = 8
N_GROUPS = 8
TOPK_GROUPS = 4
ROUTED_SCALE = 2.5
N_MOD = 6
RMS_EPS = 1e-6
NEG_INF = -1e30

LANES = 128
SUBLANES = 8
VMEM_LIMIT = 56 * 1024 * 1024

MOD_SHIFT_MIX, MOD_SCALE_MIX, MOD_GATE_MIX, MOD_SHIFT_FFN, MOD_SCALE_FFN, MOD_GATE_FFN = range(6)
MOD_ROWS = 8

ROW_TILE = 256
ROUTER_TILE = 1024
EXPERT_TILE = 512
MM_TILE_M = 2048
MM_TILE_N = 512
GATHER_TILE = 256
SCATTER_TILE = 1024

_f32 = jnp.float32
_bf16 = jnp.bfloat16


def _params(sem, vmem=VMEM_LIMIT):
    return pltpu.CompilerParams(dimension_semantics=sem, vmem_limit_bytes=vmem)


def _rms(x, g):
    return x * lax.rsqrt(jnp.mean(x * x, axis=-1, keepdims=True) + RMS_EPS) * g


def _dot(a, b):
    return jnp.dot(a, b, preferred_element_type=_f32)


def _dot_nt(a, b):
    return lax.dot_general(a, b, (((1,), (1,)), ((), ())), preferred_element_type=_f32)


def _cond_row(row0, n_ctx, dec_seq):
    return jnp.where(row0 < n_ctx, 0, 1 + (row0 - n_ctx) // dec_seq)


def _adaln_kernel(cb_ref, w_ref, b_ref, o_ref, acc_ref, *, n_cond):
    k = pl.program_id(2)

    @pl.when(k == 0)
    def _():
        acc_ref[...] = jnp.zeros_like(acc_ref)

    tk, tn = w_ref.shape
    kb = min(LANES, tk)
    for k0 in range(0, tk, kb):
        s = []
        for r in range(n_cond):
            v = cb_ref[r, k0:k0 + kb, :]
            s.append(v * jax.nn.sigmoid(v))
        for c in range(tn // LANES):
            cols = slice(c * LANES, (c + 1) * LANES)
            w = w_ref[k0:k0 + kb, cols]
            for r in range(n_cond):
                acc_ref[r, :, cols] += (w * s[r]).reshape(kb // SUBLANES, SUBLANES, LANES).sum(axis=0)

    @pl.when(k == pl.num_programs(2) - 1)
    def _():
        o_ref[...] = jnp.zeros_like(o_ref)
        for r in range(n_cond):
            o_ref[pl.ds(r, 1), :] = acc_ref[r].sum(axis=0, keepdims=True) + b_ref[...]


def _adaln(cond, w_ada, b_ada):
    n_cond, d = cond.shape
    n_layers, _, n6 = w_ada.shape
    tk, tn = min(1024, d), min(2048, n6)
    cb = jnp.broadcast_to(cond[:, :, None], (n_cond, d, LANES))
    mod = pl.pallas_call(
        functools.partial(_adaln_kernel, n_cond=n_cond),
        grid=(n_layers, n6 // tn, d // tk),
        in_specs=[
            pl.BlockSpec((n_cond, tk, LANES), lambda l, n, k: (0, k, 0)),
            pl.BlockSpec((None, tk, tn), lambda l, n, k: (l, k, n)),
            pl.BlockSpec((None, 1, tn), lambda l, n, k: (l, 0, n)),
        ],
        out_specs=pl.BlockSpec((None, MOD_ROWS, tn), lambda l, n, k: (l, 0, n)),
        out_shape=jax.ShapeDtypeStruct((n_layers, MOD_ROWS, n6), _f32),
        scratch_shapes=[pltpu.VMEM((n_cond, SUBLANES, tn), _f32)],
        compiler_params=_params(("arbitrary", "arbitrary", "arbitrary")),
        name="adaln",
    )(cb, w_ada, b_ada.reshape(n_layers, 1, n6))
    mod = mod[:, :n_cond].reshape(n_layers, n_cond, N_MOD, d)
    return jnp.pad(mod, ((0, 0), (0, 0), (0, MOD_ROWS - N_MOD), (0, 0)))


def _split_specs(parts, tm, width):
    if len(parts) == 1:
        return [pl.BlockSpec((tm, width), lambda i: (i, 0))]
    n0 = parts[0].shape[0] // tm
    n1 = parts[1].shape[0] // tm
    return [pl.BlockSpec((tm, width), lambda i: (jnp.minimum(i, n0 - 1), 0)),
            pl.BlockSpec((tm, width), lambda i: (jnp.clip(i - n0, 0, n1 - 1), 0))]


def _split_read(refs, n_first):
    if len(refs) == 1:
        return refs[0][...]
    return jnp.where(pl.program_id(0) < n_first, refs[0][...], refs[1][...])


def _modulate_kernel(*refs, n_x, n_first):
    x_refs, (g_ref, mod_ref, o_ref) = refs[:n_x], refs[n_x:]
    y = _rms(_split_read(x_refs, n_first), g_ref[...])
    h = y * (1.0 + mod_ref[pl.ds(MOD_SCALE_MIX, 1), :]) + mod_ref[pl.ds(MOD_SHIFT_MIX, 1), :]
    o_ref[...] = h.astype(o_ref.dtype)


def _modulate(xs, g, mod, n_ctx, dec_seq):
    t = sum(x.shape[0] for x in xs)
    d = xs[0].shape[1]
    tm = ROW_TILE
    return pl.pallas_call(
        functools.partial(_modulate_kernel, n_x=len(xs), n_first=xs[0].shape[0] // tm),
        grid=(t // tm,),
        in_specs=_split_specs(xs, tm, d) + [
            pl.BlockSpec((1, d), lambda i: (0, 0)),
            pl.BlockSpec((None, MOD_ROWS, d), lambda i: (_cond_row(i * tm, n_ctx, dec_seq), 0, 0)),
        ],
        out_specs=pl.BlockSpec((tm, d), lambda i: (i, 0)),
        out_shape=jax.ShapeDtypeStruct((t, d), _bf16),
        compiler_params=_params(("parallel",)),
        name="modulate",
    )(*xs, g.reshape(1, d), mod)


def _matmul_kernel(a_ref, w_ref, o_ref):
    o_ref[...] = _dot(a_ref[...], w_ref[...].astype(_bf16)).astype(o_ref.dtype)


def _matmul(a, w, layer):
    n_rows = a.shape[0]
    _, k, n = w.shape
    tm, tn = min(MM_TILE_M, n_rows), min(MM_TILE_N, n)
    return pl.pallas_call(
        _matmul_kernel,
        grid=(n_rows // tm, n // tn),
        in_specs=[
            pl.BlockSpec((tm, k), lambda i, j: (i, 0)),
            pl.BlockSpec((None, k, tn), lambda i, j: (layer, 0, j)),
        ],
        out_specs=pl.BlockSpec((tm, tn), lambda i, j: (i, j)),
        out_shape=jax.ShapeDtypeStruct((n_rows, n), _f32),
        compiler_params=_params(("parallel", "parallel")),
        name="matmul",
    )(a, w)


def _ctx_attn_kernel(q_ref, k_ref, v_ref, o_ref, kout_ref, vout_ref, *, n_heads, scale):
    seq = q_ref.shape[0]
    dh = q_ref.shape[1] // n_heads
    for h in range(n_heads):
        sl = slice(h * dh, (h + 1) * dh)
        k32 = k_ref[:, sl]
        v32 = v_ref[:, sl]
        kout_ref[pl.ds(h, seq, stride=n_heads), :] = k32
        vout_ref[pl.ds(h, seq, stride=n_heads), :] = v32
        q = q_ref[:, sl].astype(_bf16)
        s = _dot_nt(q, k32.astype(_bf16)) * scale
        p = jnp.exp(s - s.max(axis=-1, keepdims=True))
        o = _dot(p.astype(_bf16), v32.astype(_bf16)) / p.sum(axis=-1, keepdims=True)
        o_ref[:, sl] = o.astype(o_ref.dtype)


def _ctx_attention(qkv, batch, seq, d):
    dh = d // N_HEADS
    assert dh == LANES
    cache = jax.ShapeDtypeStruct((batch * seq * N_HEADS, dh), _f32)
    return pl.pallas_call(
        functools.partial(_ctx_attn_kernel, n_heads=N_HEADS, scale=dh ** -0.5),
        grid=(batch,),
        in_specs=[
            pl.BlockSpec((seq, d), lambda b: (b, 0)),
            pl.BlockSpec((seq, d), lambda b: (b, 1)),
            pl.BlockSpec((seq, d), lambda b: (b, 2)),
        ],
        out_specs=[
            pl.BlockSpec((seq, d), lambda b: (b, 0)),
            pl.BlockSpec((seq * N_HEADS, dh), lambda b: (b, 0)),
            pl.BlockSpec((seq * N_HEADS, dh), lambda b: (b, 0)),
        ],
        out_shape=[jax.ShapeDtypeStruct((batch * seq, d), _bf16), cache, cache],
        compiler_params=_params(("parallel",)),
        name="ctx_attention",
    )(qkv, qkv, qkv)


def _natten_kernel(q_ref, k_ref, v_ref, kc_ref, vc_ref, rpb_ref, o_ref, qb_ref, kb_ref, vb_ref,
                   sctx_ref, pctx_ref, oloc_ref, den_ref, blo_ref, bhi_ref, *, rows, width, kh, scale, unroll):
    n_loc = kh * width
    q_col = lax.broadcasted_iota(jnp.int32, (width, n_loc), 0)
    k_col = lax.broadcasted_iota(jnp.int32, (width, n_loc), 1) % width
    col_start = jnp.clip(q_col - WIN_COLS // 2, 0, width - WIN_COLS)
    col_mask = (k_col >= col_start) & (k_col < col_start + WIN_COLS)
    low_half = lax.broadcasted_iota(jnp.int32, (width, LANES), 1) < width

    for dr in range(rpb_ref.shape[0]):
        row = jnp.broadcast_to(rpb_ref[pl.ds(dr, 1), :], (width, LANES))
        blo_ref[dr] = pltpu.roll(row, LANES - (WIN_COLS - 1), axis=1, stride=1, stride_axis=0)
        bhi_ref[dr] = pltpu.roll(row, (LANES - (WIN_COLS - 1) + width) % LANES, axis=1, stride=1, stride_axis=0)

    def bias_of(delta):
        cols = []
        for i in range(0, kh, 2):
            dr = i - delta + (WIN_ROWS - 1)
            cols.append(jnp.where(low_half, blo_ref[dr], bhi_ref[dr + 1]))
        return jnp.concatenate(cols, axis=1)

    qb_ref[...] = q_ref[...].astype(_bf16)
    kb_ref[...] = k_ref[...].astype(_bf16)
    vb_ref[...] = v_ref[...].astype(_bf16)
    sctx_ref[...] = _dot_nt(qb_ref[...], kc_ref[...].astype(_bf16)) * scale

    def one_row(r):
        r0 = jnp.clip(r - kh // 2, 0, rows - kh)
        qrows = pl.ds(pl.multiple_of(r * width, width), width)
        win = pl.ds(pl.multiple_of(r0 * width, width), n_loc)
        s_loc = _dot_nt(qb_ref[qrows, :], kb_ref[win, :]) * scale + bias_of(r - r0)
        s_loc = jnp.where(col_mask, s_loc, NEG_INF)
        s_ctx = sctx_ref[qrows, :]
        m = jnp.maximum(s_loc.max(axis=-1, keepdims=True), s_ctx.max(axis=-1, keepdims=True))
        p_loc = jnp.exp(s_loc - m)
        p_ctx = jnp.exp(s_ctx - m)
        den = p_loc.sum(axis=-1, keepdims=True) + p_ctx.sum(axis=-1, keepdims=True)
        pctx_ref[qrows, :] = p_ctx.astype(_bf16)
        oloc_ref[qrows, :] = _dot(p_loc.astype(_bf16), vb_ref[win, :])
        den_ref[qrows, :] = jnp.broadcast_to(den, (width, den_ref.shape[1]))

    def some_rows(g, carry):
        for u in range(unroll):
            one_row(g * unroll + u)
        return carry

    lax.fori_loop(0, rows // unroll, some_rows, 0)
    o = oloc_ref[...] + _dot(pctx_ref[...], vc_ref[...].astype(_bf16))
    o_ref[...] = (o / den_ref[...]).astype(o_ref.dtype)


def _natten(qkv, n_ctx, k_ctx, v_ctx, rpb, dec_batch, dec_seq, d):
    assert n_ctx % dec_seq == 0
    dh = d // N_HEADS
    rows = dec_seq // GRID_W
    kh = min(WIN_ROWS, rows)
    assert 2 * GRID_W == LANES and kh % 2 == 0 and 2 * WIN_COLS - 1 <= LANES
    past = k_ctx.shape[1]
    n_dr = rpb.shape[1]
    rpb = jnp.pad(rpb.astype(_f32), ((0, 0), (0, 0), (0, LANES - rpb.shape[2])))
    b0 = n_ctx // dec_seq
    unroll = next(u for u in (16, 8, 4, 2, 1) if rows % u == 0)
    return pl.pallas_call(
        functools.partial(_natten_kernel, rows=rows, width=GRID_W, kh=kh, scale=dh ** -0.5, unroll=unroll),
        grid=(dec_batch, N_HEADS),
        in_specs=[
            pl.BlockSpec((dec_seq, dh), lambda b, h: (b0 + b, h)),
            pl.BlockSpec((dec_seq, dh), lambda b, h: (b0 + b, N_HEADS + h)),
            pl.BlockSpec((dec_seq, dh), lambda b, h: (b0 + b, 2 * N_HEADS + h)),
            pl.BlockSpec((None, past, dh), lambda b, h: (b, 0, h)),
            pl.BlockSpec((None, past, dh), lambda b, h: (b, 0, h)),
            pl.BlockSpec((None, n_dr, LANES), lambda b, h: (h, 0, 0)),
        ],
        out_specs=pl.BlockSpec((dec_seq, dh), lambda b, h: (b, h)),
        out_shape=jax.ShapeDtypeStruct((dec_batch * dec_seq, d), _bf16),
        scratch_shapes=[pltpu.VMEM((dec_seq, dh), _bf16), pltpu.VMEM((dec_seq, dh), _bf16),
                        pltpu.VMEM((dec_seq, dh), _bf16),
                        pltpu.VMEM((dec_seq, past), _f32), pltpu.VMEM((dec_seq, past), _bf16),
                        pltpu.VMEM((dec_seq, dh), _f32), pltpu.VMEM((dec_seq, dh), _f32),
                        pltpu.VMEM((n_dr, GRID_W, LANES), _f32), pltpu.VMEM((n_dr, GRID_W, LANES), _f32)],
        compiler_params=_params(("parallel", "parallel")),
        name="natten",
    )(qkv, qkv, qkv, k_ctx, v_ctx, rpb)


def _conv_gate_kernel(b_ref, c_ref, u_ref, cp_ref, up_ref, cn_ref, un_ref, w_ref, o_ref,
                      *, n_ctx, seq, dec_seq):
    tm = o_ref.shape[0]
    row0 = pl.program_id(0) * tm
    in_ctx = row0 < n_ctx
    pos0 = jnp.where(in_ctx, row0 % seq, (row0 - n_ctx) % dec_seq)
    seq_len = jnp.where(in_ctx, seq, dec_seq)
    has_prev = pos0 > 0
    has_next = pos0 + tm < seq_len
    cu = c_ref[...] * u_ref[...]
    prev_row = jnp.where(has_prev, cp_ref[pl.ds(SUBLANES - 1, 1), :] * up_ref[pl.ds(SUBLANES - 1, 1), :], 0.0)
    next_row = jnp.where(has_next, cn_ref[pl.ds(0, 1), :] * un_ref[pl.ds(0, 1), :], 0.0)
    ridx = lax.broadcasted_iota(jnp.int32, cu.shape, 0)
    before = jnp.where(ridx == 0, prev_row, pltpu.roll(cu, 1, axis=0))
    after = jnp.where(ridx == tm - 1, next_row, pltpu.roll(cu, tm - 1, axis=0))
    conv = w_ref[pl.ds(0, 1), :] * before + w_ref[pl.ds(1, 1), :] * cu + w_ref[pl.ds(2, 1), :] * after
    o_ref[...] = (b_ref[...] * conv).astype(o_ref.dtype)


def _conv_gate(bcu, conv_w, n_ctx, seq, dec_seq, d):
    t = bcu.shape[0]
    tm = ROW_TILE
    halo = tm // SUBLANES
    last = t // SUBLANES - 1
    cw = jnp.pad(conv_w, ((0, SUBLANES - CONV_WIDTH), (0, 0)))
    prev_map = lambda col: (lambda i: (jnp.maximum(i * halo - 1, 0), col))
    next_map = lambda col: (lambda i: (jnp.minimum((i + 1) * halo, last), col))
    return pl.pallas_call(
        functools.partial(_conv_gate_kernel, n_ctx=n_ctx, seq=seq, dec_seq=dec_seq),
        grid=(t // tm,),
        in_specs=[
            pl.BlockSpec((tm, d), lambda i: (i, 0)),
            pl.BlockSpec((tm, d), lambda i: (i, 1)),
            pl.BlockSpec((tm, d), lambda i: (i, 2)),
            pl.BlockSpec((SUBLANES, d), prev_map(1)),
            pl.BlockSpec((SUBLANES, d), prev_map(2)),
            pl.BlockSpec((SUBLANES, d), next_map(1)),
            pl.BlockSpec((SUBLANES, d), next_map(2)),
            pl.BlockSpec((SUBLANES, d), lambda i: (0, 0)),
        ],
        out_specs=pl.BlockSpec((tm, d), lambda i: (i, 0)),
        out_shape=jax.ShapeDtypeStruct((t, d), _bf16),
        compiler_params=_params(("parallel",)),
        name="conv_gate",
    )(bcu, bcu, bcu, bcu, bcu, bcu, bcu, cw)


def _route(sel, scores):
    n_grp, eg, tm = sel.shape
    n_exp = n_grp * eg
    j_iota = lax.broadcasted_iota(jnp.int32, sel.shape, 1)
    m1 = sel.max(axis=1, keepdims=True)
    j1 = jnp.min(jnp.where(sel == m1, j_iota, eg), axis=1, keepdims=True)
    m2 = jnp.max(jnp.where(j_iota == j1, -jnp.inf, sel), axis=1, keepdims=True)
    grp = m1 + m2
    g_iota = lax.broadcasted_iota(jnp.int32, grp.shape, 0)
    g_sel = g_iota < 0
    for _ in range(TOPK_GROUPS):
        gm = grp.max(axis=0, keepdims=True)
        gi = jnp.min(jnp.where(grp == gm, g_iota, n_grp), axis=0, keepdims=True)
        hit = g_iota == gi
        g_sel = g_sel | hit
        grp = jnp.where(hit, -jnp.inf, grp)
    cur = jnp.where(jnp.broadcast_to(g_sel, sel.shape), sel, NEG_INF)
    e_iota = lax.broadcasted_iota(jnp.int32, sel.shape, 0) * eg + j_iota
    ids, ws, hits = [], [], []
    for _ in range(TOP_K):
        m = cur.max(axis=1, keepdims=True).max(axis=0, keepdims=True)
        ei = jnp.min(jnp.where(cur == m, e_iota, n_exp), axis=1, keepdims=True).min(axis=0, keepdims=True)
        hit = e_iota == ei
        ids.append(ei)
        ws.append(jnp.sum(jnp.where(hit, scores, 0.0), axis=1, keepdims=True).sum(axis=0, keepdims=True))
        hits.append(hit)
        cur = jnp.where(hit, -jnp.inf, cur)
    total = functools.reduce(lambda a, b: a + b, ws)
    ws = [w / total * ROUTED_SCALE for w in ws]
    return ids, ws, hits


def _split_bf16(x):
    hi = x.astype(_bf16)
    return hi, (x - hi.astype(_f32)).astype(_bf16)


def _post_mixer_kernel(*refs, n_a, n_x, n_first):
    a_refs, x_refs = refs[:n_a], refs[n_a:n_a + n_x]
    wo_ref, gpost_ref, mod_ref, gpre_ref, wr_ref, x1_ref, hf_ref, logit_ref = refs[n_a + n_x:]
    o = _dot(_split_read(a_refs, n_first), wo_ref[...])
    x1 = _split_read(x_refs, n_first) + mod_ref[pl.ds(MOD_GATE_MIX, 1), :] * _rms(o, gpost_ref[...])
    x1_ref[...] = x1
    hf = _rms(x1, gpre_ref[...]) * (1.0 + mod_ref[pl.ds(MOD_SCALE_FFN, 1), :]) + mod_ref[pl.ds(MOD_SHIFT_FFN, 1), :]
    hf_ref[...] = hf
    h_hi, h_lo = _split_bf16(hf)
    w_hi, w_lo = _split_bf16(wr_ref[...])
    logit_ref[...] = _dot(h_hi, w_hi) + (_dot(h_hi, w_lo) + _dot(h_lo, w_hi))


def _post_mixer(a_parts, w_out, layer, x_parts, g_post, mod, g_pre, w_router, n_ctx, dec_seq):
    t = sum(x.shape[0] for x in x_parts)
    d = x_parts[0].shape[1]
    tm = ROW_TILE
    e = w_router.shape[1]
    assert e <= LANES
    row = lambda i: (i, 0)
    fixed = lambda i: (0, 0)
    return pl.pallas_call(
        functools.partial(_post_mixer_kernel, n_a=len(a_parts), n_x=len(x_parts), n_first=n_ctx // tm),
        grid=(t // tm,),
        in_specs=_split_specs(a_parts, tm, d) + _split_specs(x_parts, tm, d) + [
            pl.BlockSpec((None, d, d), lambda i: (layer, 0, 0)),
            pl.BlockSpec((1, d), fixed),
            pl.BlockSpec((None, MOD_ROWS, d), lambda i: (_cond_row(i * tm, n_ctx, dec_seq), 0, 0)),
            pl.BlockSpec((1, d), fixed),
            pl.BlockSpec((d, LANES), fixed),
        ],
        out_specs=[pl.BlockSpec((tm, d), row), pl.BlockSpec((tm, d), row), pl.BlockSpec((tm, LANES), row)],
        out_shape=[jax.ShapeDtypeStruct((t, d), _f32), jax.ShapeDtypeStruct((t, d), _f32),
                   jax.ShapeDtypeStruct((t, LANES), _f32)],
        compiler_params=_params(("parallel",)),
        name="post_mixer",
    )(*a_parts, *x_parts, w_out.astype(_bf16), g_post.reshape(1, d), mod, g_pre.reshape(1, d),
      jnp.pad(w_router, ((0, 0), (0, LANES - e))))


def _router_kernel(logit_ref, br_ref, idx_ref, wt_ref, rank_ref, cnt_ref, carry_ref, before_ref):
    i = pl.program_id(0)
    tm = logit_ref.shape[0]
    n_exp = br_ref.shape[0]

    @pl.when(i == 0)
    def _():
        carry_ref[...] = jnp.zeros_like(carry_ref)
        t_src = lax.broadcasted_iota(jnp.int32, (tm, tm), 0)
        t_dst = lax.broadcasted_iota(jnp.int32, (tm, tm), 1)
        before_ref[...] = jnp.where(t_src < t_dst, 1.0, 0.0).astype(_bf16)

    logits = logit_ref[...].T[:n_exp]
    grouped = (N_GROUPS, n_exp // N_GROUPS, tm)
    scores = jax.nn.sigmoid(logits)
    ids, ws, hits = _route((scores + br_ref[...]).reshape(grouped), scores.reshape(grouped))
    for k in range(TOP_K):
        idx_ref[pl.ds(k, 1), :] = ids[k][0]
        wt_ref[pl.ds(k, 1), :] = ws[k][0]

    any_hit = functools.reduce(lambda a, b: a | b, hits)
    mask = jnp.where(any_hit, 1.0, 0.0).reshape(n_exp, tm).astype(_bf16)
    rank = _dot(mask, before_ref[...]) + jnp.concatenate([carry_ref[...]] * (tm // LANES), axis=1)
    rank = rank.reshape(grouped)
    for k in range(TOP_K):
        rk = jnp.sum(jnp.where(hits[k], rank, 0.0), axis=1, keepdims=True).sum(axis=0, keepdims=True)
        rank_ref[pl.ds(k, 1), :] = rk[0].astype(jnp.int32)
    carry_ref[...] += _dot(mask, jnp.ones((tm, LANES), _bf16))
    cnt_ref[...] = carry_ref[...].astype(jnp.int32)


def _router(logits, b_router):
    t = logits.shape[0]
    e = b_router.shape[0]
    tm = min(ROUTER_TILE, t)
    col = lambda i: (0, i)
    fixed = lambda i: (0, 0)
    slots_i = jax.ShapeDtypeStruct((TOP_K, t), jnp.int32)
    return pl.pallas_call(
        _router_kernel,
        grid=(t // tm,),
        in_specs=[pl.BlockSpec((tm, LANES), lambda i: (i, 0)), pl.BlockSpec((e, 1), fixed)],
        out_specs=[pl.BlockSpec((TOP_K, tm), col), pl.BlockSpec((TOP_K, tm), col), pl.BlockSpec((TOP_K, tm), col),
                   pl.BlockSpec((e, LANES), fixed)],
        out_shape=[slots_i, jax.ShapeDtypeStruct((TOP_K, t), _f32), slots_i,
                   jax.ShapeDtypeStruct((e, LANES), jnp.int32)],
        scratch_shapes=[pltpu.VMEM((e, LANES), _f32), pltpu.VMEM((tm, tm), _bf16)],
        compiler_params=_params(("arbitrary",)),
        name="router",
    )(logits, b_router.reshape(e, 1))


def _row(ref, r):
    return ref.at[pl.ds(r, 1), :]


NOT_FIRST = -2
NO_NEXT = -1


def _plan_kernel(cnt_ref, idx_ref, rank_ref, pos_ref, te_ref, tbi_ref, tbo_ref, tv_ref, tnext_ref, trun_ref,
                 padrow_ref, padn_ref, offs_ref, *, tm):
    n_exp = cnt_ref.shape[0]
    n_tiles = te_ref.shape[0]

    def per_expert(e, carry):
        off, tile, run, prev_first = carry
        cnt = cnt_ref[e]
        n_t = (cnt + tm - 1) // tm
        offs_ref[e] = off
        padrow_ref[e] = off + cnt
        padn_ref[e] = n_t * tm - cnt

        def per_tile(j, c):
            te_ref[tile + j] = e
            tbi_ref[tile + j] = tile + j
            tbo_ref[tile + j] = tile + j
            tv_ref[tile + j] = jnp.minimum(cnt - j * tm, tm)
            tnext_ref[tile + j] = jnp.where(j == 0, NO_NEXT, NOT_FIRST)
            trun_ref[tile + j] = run
            return c

        lax.fori_loop(0, n_t, per_tile, 0)
        used = n_t > 0

        @pl.when(used & (prev_first >= 0))
        def _():
            tnext_ref[prev_first] = e

        return (off + n_t * tm, tile + n_t, run + used.astype(jnp.int32), jnp.where(used, tile, prev_first))

    zero = jnp.int32(0)
    _, live, _, _ = lax.fori_loop(0, n_exp, per_expert, (zero, zero, zero, jnp.int32(-1)))
    last = jnp.maximum(live - 1, 0)
    last_expert = te_ref[last]

    def dead_tile(i, c):
        te_ref[i] = last_expert
        tbi_ref[i] = last
        tbo_ref[i] = last
        tv_ref[i] = 0
        tnext_ref[i] = NOT_FIRST
        trun_ref[i] = 0
        return c

    lax.fori_loop(live, n_tiles, dead_tile, 0)

    idx = idx_ref[...]
    pos = rank_ref[...]
    for e in range(n_exp):
        pos = pos + jnp.where(idx == e, offs_ref[e], 0)
    pos_ref[...] = pos


def _plan(counts, idx_t, rank_t, n_tiles):
    n_exp = counts.shape[0]
    smem = pl.BlockSpec(memory_space=pltpu.SMEM)
    vmem = pl.BlockSpec(memory_space=pltpu.VMEM)
    tiles = jax.ShapeDtypeStruct((n_tiles,), jnp.int32)
    experts = jax.ShapeDtypeStruct((n_exp,), jnp.int32)
    return pl.pallas_call(
        functools.partial(_plan_kernel, tm=EXPERT_TILE),
        in_specs=[smem, vmem, vmem],
        out_specs=[vmem] + [smem] * 8,
        out_shape=[jax.ShapeDtypeStruct(idx_t.shape, jnp.int32)] + [tiles] * 6 + [experts] * 2,
        scratch_shapes=[pltpu.SMEM((n_exp,), jnp.int32)],
        name="plan",
    )(counts, idx_t, rank_t)


def _pad_copies(start, n, zero_ref, dst_ref, sem, pad_bits):
    single = n & (SUBLANES - 1)
    for s in range(SUBLANES - 1):
        yield s < single, pltpu.make_async_copy(_row(zero_ref, 0), _row(dst_ref, start + s), sem)
    base = pl.multiple_of(start + single, SUBLANES)
    groups = n // SUBLANES
    for b in range(pad_bits - (SUBLANES.bit_length() - 1)):
        rows = SUBLANES << b
        first = pl.multiple_of(base + ((groups >> (b + 1)) << (b + 1)) * SUBLANES, SUBLANES)
        copy = pltpu.make_async_copy(zero_ref.at[pl.ds(0, rows), :], dst_ref.at[pl.ds(first, rows), :], sem)
        yield ((groups >> b) & 1) == 1, copy


def _slot_specs(n_tokens, tile, step_of):
    per_k = n_tokens // tile
    return [pl.BlockSpec((tile,), lambda i, k=k: (k * per_k + step_of(i),), memory_space=pltpu.SMEM)
            for k in range(TOP_K)]


def _dispatch_kernel(*refs, pad_bits):
    pos_refs = refs[:TOP_K]
    padrow_ref, padn_ref, src_ref, dst_ref, zero_ref, sem, zero_sem = refs[TOP_K:]
    i = pl.program_id(0)
    tile = src_ref.shape[0]
    n_exp = padn_ref.shape[0]

    def for_each_pad_copy(fn):
        def body(e, c):
            for needed, copy in _pad_copies(padrow_ref[e], padn_ref[e], zero_ref, dst_ref, zero_sem, pad_bits):
                pl.when(needed)(functools.partial(fn, copy))
            return c
        lax.fori_loop(0, n_exp, body, 0)

    @pl.when(i == 0)
    def _():
        zero_ref[...] = jnp.zeros_like(zero_ref)
        for_each_pad_copy(lambda copy: copy.start())

    def issue(g, carry):
        for t in (2 * g, 2 * g + 1):
            for k in range(TOP_K):
                pltpu.make_async_copy(_row(src_ref, t), _row(dst_ref, pos_refs[k][t]), sem).start()
        return carry

    lax.fori_loop(0, tile // 2, issue, 0)
    for k in range(TOP_K):
        pltpu.make_async_copy(src_ref, dst_ref.at[pl.ds(0, tile), :], sem).wait()

    @pl.when(i == 0)
    def _():
        for_each_pad_copy(lambda copy: copy.wait())


def _dispatch(hf, pos, padrow, padn, n_sorted):
    t, d = hf.shape
    tile = min(SCATTER_TILE, t)
    pad_bits = (EXPERT_TILE - 1).bit_length()
    smem = pl.BlockSpec(memory_space=pltpu.SMEM)
    return pl.pallas_call(
        functools.partial(_dispatch_kernel, pad_bits=pad_bits),
        grid=(t // tile,),
        in_specs=_slot_specs(t, tile, lambda i: i) + [
            smem, smem,
            pl.BlockSpec((tile, d), lambda i: (i, 0)),
        ],
        out_specs=pl.BlockSpec(memory_space=pl.ANY),
        out_shape=jax.ShapeDtypeStruct((n_sorted, d), _f32),
        scratch_shapes=[pltpu.VMEM((1 << (pad_bits - 1), d), _f32),
                        pltpu.SemaphoreType.DMA(()), pltpu.SemaphoreType.DMA(())],
        compiler_params=_params(("arbitrary",)),
        name="dispatch",
    )(*([pos] * TOP_K), padrow, padn, hf)


def _expert_kernel(te_ref, tbi_ref, tbo_ref, tv_ref, tnext_ref, trun_ref, x_ref, wg_hbm, wu_hbm, wd_hbm, o_ref,
                   wg_ref, wu_ref, wd_ref, wgb_ref, wub_ref, wdb_ref, sems, *, layer):
    i = pl.program_id(0)
    live = tv_ref[i] > 0
    slot = trun_ref[i] % 2

    def fetch(expert, s):
        return [pltpu.make_async_copy(src.at[layer, expert], dst.at[s], sems.at[s])
                for src, dst in ((wg_hbm, wg_ref), (wu_hbm, wu_ref), (wd_hbm, wd_ref))]

    @pl.when(i == 0)
    def _():
        for copy in fetch(te_ref[0], 0):
            copy.start()

    @pl.when(live & (tnext_ref[i] != NOT_FIRST))
    def _():
        @pl.when(tnext_ref[i] != NO_NEXT)
        def _():
            for copy in fetch(tnext_ref[i], 1 - slot):
                copy.start()

        for copy in fetch(te_ref[i], slot):
            copy.wait()
        wgb_ref[...] = wg_ref[slot].astype(_bf16)
        wub_ref[...] = wu_ref[slot].astype(_bf16)
        wdb_ref[...] = wd_ref[slot].astype(_bf16)

    @pl.when(live)
    def _():
        x = x_ref[...].astype(_bf16)
        g = _dot(x, wgb_ref[...])
        u = _dot(x, wub_ref[...])
        h = (g * jax.nn.sigmoid(g) * u).astype(_bf16)
        o_ref[...] = _dot(h, wdb_ref[...])


def _experts(xs, w_gate, w_up, w_down, layer, tile_expert, tile_in, tile_out, tile_valid, tile_next, tile_run):
    _, n_exp, d, f = w_gate.shape
    tm = EXPERT_TILE
    n_tiles = tile_expert.shape[0]
    hbm = pl.BlockSpec(memory_space=pl.ANY)
    return pl.pallas_call(
        functools.partial(_expert_kernel, layer=layer),
        grid_spec=pltpu.PrefetchScalarGridSpec(
            num_scalar_prefetch=6,
            grid=(n_tiles,),
            in_specs=[pl.BlockSpec((tm, d), lambda i, te, tbi, tbo, tv, tn, tr: (tbi[i], 0)), hbm, hbm, hbm],
            out_specs=pl.BlockSpec((tm, d), lambda i, te, tbi, tbo, tv, tn, tr: (tbo[i], 0)),
            scratch_shapes=[pltpu.VMEM((2, d, f), _f32), pltpu.VMEM((2, d, f), _f32), pltpu.VMEM((2, f, d), _f32),
                            pltpu.VMEM((d, f), _bf16), pltpu.VMEM((d, f), _bf16), pltpu.VMEM((f, d), _bf16),
                            pltpu.SemaphoreType.DMA((2,))],
        ),
        out_shape=jax.ShapeDtypeStruct(xs.shape, _f32),
        compiler_params=_params(("arbitrary",)),
        name="experts",
    )(tile_expert, tile_in, tile_out, tile_valid, tile_next, tile_run, xs, w_gate, w_up, w_down)


def _combine_kernel(*refs, n_first):
    pos_refs, posn_refs = refs[:TOP_K], refs[TOP_K:2 * TOP_K]
    ys_ref, wt_ref, sh_ref, x1_ref, gpost_ref, mod_ref = refs[2 * TOP_K:2 * TOP_K + 6]
    rest = refs[2 * TOP_K + 6:]
    final = n_first is not None
    if final:
        yctx_ref, ylat_ref, buf0_ref, buf1_ref, f_ref, sems = rest
    else:
        gnext_ref, modn_ref, x2_ref, hn_ref, buf0_ref, buf1_ref, f_ref, sems = rest
    tm, d = x1_ref.shape
    i = pl.program_id(0)
    last = pl.num_programs(0) - 1

    def start_rows(p_refs, buf_ref, sem, t):
        for k in range(TOP_K):
            pltpu.make_async_copy(_row(ys_ref, p_refs[k][t]), buf_ref.at[k, pl.ds(t, 1), :], sem).start()

    def wait_tile(buf_ref, sem):
        for k in range(TOP_K):
            pltpu.make_async_copy(ys_ref.at[pl.ds(0, tm), :], buf_ref.at[k], sem).wait()

    @pl.when(i == 0)
    def _():
        def first(t, carry):
            start_rows(pos_refs, buf0_ref, sems.at[0], t)
            return carry
        lax.fori_loop(0, tm, first, 0)

    def step(cur_ref, cur_sem, nxt_ref, nxt_sem):
        wait_tile(cur_ref, cur_sem)

        def group(g, carry):
            for u in range(SUBLANES):
                start_rows(posn_refs, nxt_ref, nxt_sem, g * SUBLANES + u)
            rows = pl.ds(pl.multiple_of(g * SUBLANES, SUBLANES), SUBLANES)
            w = wt_ref[rows, :]
            wb = [jnp.broadcast_to(w[:, k:k + 1], (SUBLANES, LANES)) for k in range(TOP_K)]
            for j in range(d // LANES):
                cols = slice(j * LANES, (j + 1) * LANES)
                acc = sh_ref[rows, cols]
                for k in range(TOP_K):
                    acc = acc + wb[k] * cur_ref[k, rows, cols]
                f_ref[rows, cols] = acc
            return carry

        lax.fori_loop(0, tm // SUBLANES, group, 0)

        @pl.when(i == last)
        def _():
            wait_tile(nxt_ref, nxt_sem)

    pl.when(i % 2 == 0)(functools.partial(step, buf0_ref, sems.at[0], buf1_ref, sems.at[1]))
    pl.when(i % 2 == 1)(functools.partial(step, buf1_ref, sems.at[1], buf0_ref, sems.at[0]))
    f = f_ref[...]
    x2 = x1_ref[...] + mod_ref[pl.ds(MOD_GATE_FFN, 1), :] * _rms(f, gpost_ref[...])
    if final:
        in_ctx = pl.program_id(0) < n_first

        @pl.when(in_ctx)
        def _():
            yctx_ref[...] = x2

        @pl.when(jnp.logical_not(in_ctx))
        def _():
            ylat_ref[...] = x2
    else:
        x2_ref[...] = x2
        hn = (_rms(x2, gnext_ref[...]) * (1.0 + modn_ref[pl.ds(MOD_SCALE_MIX, 1), :])
              + modn_ref[pl.ds(MOD_SHIFT_MIX, 1), :])
        hn_ref[...] = hn.astype(hn_ref.dtype)


def _combine(ys, pos, wt, shared, x1, g_post, mod, n_ctx, dec_seq, g_next=None, mod_next=None):
    t, d = x1.shape
    tm = min(GATHER_TILE, t)
    n_steps = t // tm
    final = g_next is None
    row = lambda i: (i, 0)
    fixed = lambda i: (0, 0)
    cond = lambda i: (_cond_row(i * tm, n_ctx, dec_seq), 0, 0)
    in_specs = _slot_specs(t, tm, lambda i: i) + _slot_specs(t, tm, lambda i: jnp.minimum(i + 1, n_steps - 1)) + [
        pl.BlockSpec(memory_space=pl.ANY),
        pl.BlockSpec((tm, TOP_K), row),
        pl.BlockSpec((tm, d), row),
        pl.BlockSpec((tm, d), row),
        pl.BlockSpec((1, d), fixed),
        pl.BlockSpec((None, MOD_ROWS, d), cond),
    ]
    args = [pos] * (2 * TOP_K) + [ys, wt, shared, x1, g_post.reshape(1, d), mod]
    if final:
        n0, n1 = n_ctx // tm, (t - n_ctx) // tm
        out_specs = [pl.BlockSpec((tm, d), lambda i: (jnp.minimum(i, n0 - 1), 0)),
                     pl.BlockSpec((tm, d), lambda i: (jnp.clip(i - n0, 0, n1 - 1), 0))]
        out_shape = [jax.ShapeDtypeStruct((n_ctx, d), _f32), jax.ShapeDtypeStruct((t - n_ctx, d), _f32)]
    else:
        in_specs += [pl.BlockSpec((1, d), fixed), pl.BlockSpec((None, MOD_ROWS, d), cond)]
        args += [g_next.reshape(1, d), mod_next]
        out_specs = [pl.BlockSpec((tm, d), row), pl.BlockSpec((tm, d), row)]
        out_shape = [jax.ShapeDtypeStruct((t, d), _f32), jax.ShapeDtypeStruct((t, d), _bf16)]
    return pl.pallas_call(
        functools.partial(_combine_kernel, n_first=n_ctx // tm if final else None),
        grid=(n_steps,),
        in_specs=in_specs,
        out_specs=out_specs,
        out_shape=out_shape,
        scratch_shapes=[pltpu.VMEM((TOP_K, tm, d), _f32), pltpu.VMEM((TOP_K, tm, d), _f32), pltpu.VMEM((tm, d), _f32),
                        pltpu.SemaphoreType.DMA((2,))],
        compiler_params=_params(("arbitrary",)),
        name="combine",
    )(*args)


def _moe(hf, idx_t, wt_t, rank_t, cnt, layer, w_gate, w_up, w_down, w_sh_gate, w_sh_up, w_sh_down):
    n_exp, d = w_gate.shape[1:3]
    t = hf.shape[0]
    tm = EXPERT_TILE
    n_tiles = t * TOP_K // tm + n_exp
    pos, *tiles, padrow, padn = _plan(cnt[:, 0], idx_t, rank_t, n_tiles)
    pos = pos.reshape(-1)
    xs = _dispatch(hf, pos, padrow, padn, n_tiles * tm)
    ys = _experts(xs, w_gate, w_up, w_down, layer, *tiles)
    n_sh = t // tm
    every = jnp.arange(n_sh, dtype=jnp.int32)
    zeros = jnp.zeros((n_sh,), jnp.int32)
    shared = _experts(hf, w_sh_gate[:, None], w_sh_up[:, None], w_sh_down[:, None], layer,
                      zeros, every, every, jnp.full((n_sh,), tm, jnp.int32),
                      jnp.full((n_sh,), NOT_FIRST, jnp.int32).at[0].set(NO_NEXT), zeros)
    return ys, pos, wt_t.T, shared


def kernel(x_prompt, x_sample, cache_k, cache_v, c, c_ctx, w_ada, b_ada, norm_mix_pre, norm_mix_post, norm_ffn_pre, norm_ffn_post, w_qkv, w_o_attn, rpb, w_conv_in, conv_w, w_conv_out, w_router, b_router, w_exp_gate, w_exp_up, w_exp_down, w_sh_gate, w_sh_up, w_sh_down):
    batch, seq, d = x_prompt.shape
    dec_batch, dec_seq, _ = x_sample.shape
    depth = w_ada.shape[0]
    n_ctx, n_lat = batch * seq, dec_batch * dec_seq
    dh = d // N_HEADS
    past = cache_k.shape[2]

    x = [x_prompt.reshape(n_ctx, d), x_sample.reshape(n_lat, d)]
    cond = jnp.concatenate([c_ctx[None, :], c], axis=0)
    mod = _adaln(cond, w_ada, b_ada)

    new_k, new_v = [], []
    h = _modulate(x, norm_mix_pre[0], mod[0], n_ctx, dec_seq)
    for l in range(depth):
        if l % 2 == 0:
            a = l // 2
            qkv = _matmul(h, w_qkv, a)
            o_ctx, k_new, v_new = _ctx_attention(qkv, batch, seq, d)
            new_k.append(k_new.reshape(batch, seq, N_HEADS, dh))
            new_v.append(v_new.reshape(batch, seq, N_HEADS, dh))
            o_lat = _natten(qkv, n_ctx, cache_k[:, a].reshape(dec_batch, past, d),
                            cache_v[:, a].reshape(dec_batch, past, d), rpb[a], dec_batch, dec_seq, d)
            mixed, w_out, w_layer = [o_ctx, o_lat], w_o_attn, a
        else:
            m = l // 2
            mixed = [_conv_gate(_matmul(h, w_conv_in, m), conv_w[m], n_ctx, seq, dec_seq, d)]
            w_out, w_layer = w_conv_out, m
        x1, hf, logits = _post_mixer(
            mixed, w_out, w_layer, x, norm_mix_post[l], mod[l], norm_ffn_pre[l], w_router[l], n_ctx, dec_seq)
        idx_t, wt_t, rank_t, cnt = _router(logits, b_router[l])
        ys, pos, wt, shared = _moe(hf, idx_t, wt_t, rank_t, cnt, l, w_exp_gate, w_exp_up, w_exp_down,
                                   w_sh_gate, w_sh_up, w_sh_down)
        if l + 1 < depth:
            x2, h = _combine(ys, pos, wt, shared, x1, norm_ffn_post[l], mod[l], n_ctx, dec_seq,
                             norm_mix_pre[l + 1], mod[l + 1])
            x = [x2]
        else:
            y_ctx, y_lat = _combine(ys, pos, wt, shared, x1, norm_ffn_post[l], mod[l], n_ctx, dec_seq)

    return (y_ctx.reshape(batch, seq, d), y_lat.reshape(dec_batch, dec_seq, d),
            jnp.stack(new_k, axis=1), jnp.stack(new_v, axis=1))
```

```python
import functools

import jax
import jax.numpy as jnp
from jax import lax
from jax.experimental import pallas as pl
from jax.experimental.pallas import tpu as pltpu

N_HEADS = 16
GRID_W = 64
WIN_ROWS = 8
WIN_COLS = 16
CONV_WIDTH = 3
N_EXPERTS = 64
TOP_K = 8
N_GROUPS = 8
TOPK_GROUPS = 4
ROUTED_SCALE = 2.5
N_MOD = 6
RMS_EPS = 1e-6
NEG_INF = -1e30

LANES = 128
SUBLANES = 8
VMEM_LIMIT = 56 * 1024 * 1024

MOD_SHIFT_MIX, MOD_SCALE_MIX, MOD_GATE_MIX, MOD_SHIFT_FFN, MOD_SCALE_FFN, MOD_GATE_FFN = range(6)
MOD_ROWS = 8

ROW_TILE = 256
ROUTER_TILE = 1024
EXPERT_TILE = 512
MM_TILE_M = 2048
MM_TILE_N = 512
GATHER_TILE = 256
SCATTER_TILE = 1024

_f32 = jnp.float32
_bf16 = jnp.bfloat16


def _params(sem, vmem=VMEM_LIMIT):
    return pltpu.CompilerParams(dimension_semantics=sem, vmem_limit_bytes=vmem)


def _rms(x, g):
    return x * lax.rsqrt(jnp.mean(x * x, axis=-1, keepdims=True) + RMS_EPS) * g


def _dot(a, b):
    return jnp.dot(a, b, preferred_element_type=_f32)


def _dot_nt(a, b):
    return lax.dot_general(a, b, (((1,), (1,)), ((), ())), preferred_element_type=_f32)


def _cond_row(row0, n_ctx, dec_seq):
    return jnp.where(row0 < n_ctx, 0, 1 + (row0 - n_ctx) // dec_seq)


def _adaln_kernel(cb_ref, w_ref, b_ref, o_ref, acc_ref, *, n_cond):
    k = pl.program_id(2)

    @pl.when(k == 0)
    def _():
        acc_ref[...] = jnp.zeros_like(acc_ref)

    tk, tn = w_ref.shape
    kb = min(LANES, tk)
    for k0 in range(0, tk, kb):
        s = []
        for r in range(n_cond):
            v = cb_ref[r, k0:k0 + kb, :]
            s.append(v * jax.nn.sigmoid(v))
        for c in range(tn // LANES):
            cols = slice(c * LANES, (c + 1) * LANES)
            w = w_ref[k0:k0 + kb, cols]
            for r in range(n_cond):
                acc_ref[r, :, cols] += (w * s[r]).reshape(kb // SUBLANES, SUBLANES, LANES).sum(axis=0)

    @pl.when(k == pl.num_programs(2) - 1)
    def _():
        o_ref[...] = jnp.zeros_like(o_ref)
        for r in range(n_cond):
            o_ref[pl.ds(r, 1), :] = acc_ref[r].sum(axis=0, keepdims=True) + b_ref[...]


def _adaln(cond, w_ada, b_ada):
    n_cond, d = cond.shape
    n_layers, _, n6 = w_ada.shape
    tk, tn = min(1024, d), min(2048, n6)
    cb = jnp.broadcast_to(cond[:, :, None], (n_cond, d, LANES))
    mod = pl.pallas_call(
        functools.partial(_adaln_kernel, n_cond=n_cond),
        grid=(n_layers, n6 // tn, d // tk),
        in_specs=[
            pl.BlockSpec((n_cond, tk, LANES), lambda l, n, k: (0, k, 0)),
            pl.BlockSpec((None, tk, tn), lambda l, n, k: (l, k, n)),
            pl.BlockSpec((None, 1, tn), lambda l, n, k: (l, 0, n)),
        ],
        out_specs=pl.BlockSpec((None, MOD_ROWS, tn), lambda l, n, k: (l, 0, n)),
        out_shape=jax.ShapeDtypeStruct((n_layers, MOD_ROWS, n6), _f32),
        scratch_shapes=[pltpu.VMEM((n_cond, SUBLANES, tn), _f32)],
        compiler_params=_params(("arbitrary", "arbitrary", "arbitrary")),
        name="adaln",
    )(cb, w_ada, b_ada.reshape(n_layers, 1, n6))
    mod = mod[:, :n_cond].reshape(n_layers, n_cond, N_MOD, d)
    return jnp.pad(mod, ((0, 0), (0, 0), (0, MOD_ROWS - N_MOD), (0, 0)))


def _split_specs(parts, tm, width):
    if len(parts) == 1:
        return [pl.BlockSpec((tm, width), lambda i: (i, 0))]
    n0 = parts[0].shape[0] // tm
    n1 = parts[1].shape[0] // tm
    return [pl.BlockSpec((tm, width), lambda i: (jnp.minimum(i, n0 - 1), 0)),
            pl.BlockSpec((tm, width), lambda i: (jnp.clip(i - n0, 0, n1 - 1), 0))]


def _split_read(refs, n_first):
    if len(refs) == 1:
        return refs[0][...]
    return jnp.where(pl.program_id(0) < n_first, refs[0][...], refs[1][...])


def _modulate_kernel(*refs, n_x, n_first):
    x_refs, (g_ref, mod_ref, o_ref) = refs[:n_x], refs[n_x:]
    y = _rms(_split_read(x_refs, n_first), g_ref[...])
    h = y * (1.0 + mod_ref[pl.ds(MOD_SCALE_MIX, 1), :]) + mod_ref[pl.ds(MOD_SHIFT_MIX, 1), :]
    o_ref[...] = h.astype(o_ref.dtype)


def _modulate(xs, g, mod, n_ctx, dec_seq):
    t = sum(x.shape[0] for x in xs)
    d = xs[0].shape[1]
    tm = ROW_TILE
    return pl.pallas_call(
        functools.partial(_modulate_kernel, n_x=len(xs), n_first=xs[0].shape[0] // tm),
        grid=(t // tm,),
        in_specs=_split_specs(xs, tm, d) + [
            pl.BlockSpec((1, d), lambda i: (0, 0)),
            pl.BlockSpec((None, MOD_ROWS, d), lambda i: (_cond_row(i * tm, n_ctx, dec_seq), 0, 0)),
        ],
        out_specs=pl.BlockSpec((tm, d), lambda i: (i, 0)),
        out_shape=jax.ShapeDtypeStruct((t, d), _bf16),
        compiler_params=_params(("parallel",)),
        name="modulate",
    )(*xs, g.reshape(1, d), mod)


def _matmul_kernel(a_ref, w_ref, o_ref):
    o_ref[...] = _dot(a_ref[...], w_ref[...].astype(_bf16)).astype(o_ref.dtype)


def _matmul(a, w, layer):
    n_rows = a.shape[0]
    _, k, n = w.shape
    tm, tn = min(MM_TILE_M, n_rows), min(MM_TILE_N, n)
    return pl.pallas_call(
        _matmul_kernel,
        grid=(n_rows // tm, n // tn),
        in_specs=[
            pl.BlockSpec((tm, k), lambda i, j: (i, 0)),
            pl.BlockSpec((None, k, tn), lambda i, j: (layer, 0, j)),
        ],
        out_specs=pl.BlockSpec((tm, tn), lambda i, j: (i, j)),
        out_shape=jax.ShapeDtypeStruct((n_rows, n), _f32),
        compiler_params=_params(("parallel", "parallel")),
        name="matmul",
    )(a, w)


def _ctx_attn_kernel(q_ref, k_ref, v_ref, o_ref, kout_ref, vout_ref, *, n_heads, scale):
    seq = q_ref.shape[0]
    dh = q_ref.shape[1] // n_heads
    for h in range(n_heads):
        sl = slice(h * dh, (h + 1) * dh)
        k32 = k_ref[:, sl]
        v32 = v_ref[:, sl]
        kout_ref[pl.ds(h, seq, stride=n_heads), :] = k32
        vout_ref[pl.ds(h, seq, stride=n_heads), :] = v32
        q = q_ref[:, sl].astype(_bf16)
        s = _dot_nt(q, k32.astype(_bf16)) * scale
        p = jnp.exp(s - s.max(axis=-1, keepdims=True))
        o = _dot(p.astype(_bf16), v32.astype(_bf16)) / p.sum(axis=-1, keepdims=True)
        o_ref[:, sl] = o.astype(o_ref.dtype)


def _ctx_attention(qkv, batch, seq, d):
    dh = d // N_HEADS
    assert dh == LANES
    cache = jax.ShapeDtypeStruct((batch * seq * N_HEADS, dh), _f32)
    return pl.pallas_call(
        functools.partial(_ctx_attn_kernel, n_heads=N_HEADS, scale=dh ** -0.5),
        grid=(batch,),
        in_specs=[
            pl.BlockSpec((seq, d), lambda b: (b, 0)),
            pl.BlockSpec((seq, d), lambda b: (b, 1)),
            pl.BlockSpec((seq, d), lambda b: (b, 2)),
        ],
        out_specs=[
            pl.BlockSpec((seq, d), lambda b: (b, 0)),
            pl.BlockSpec((seq * N_HEADS, dh), lambda b: (b, 0)),
            pl.BlockSpec((seq * N_HEADS, dh), lambda b: (b, 0)),
        ],
        out_shape=[jax.ShapeDtypeStruct((batch * seq, d), _bf16), cache, cache],
        compiler_params=_params(("parallel",)),
        name="ctx_attention",
    )(qkv, qkv, qkv)


def _natten_kernel(q_ref, k_ref, v_ref, kc_ref, vc_ref, rpb_ref, o_ref, qb_ref, kb_ref, vb_ref,
                   sctx_ref, pctx_ref, oloc_ref, den_ref, blo_ref, bhi_ref, *, rows, width, kh, scale, unroll):
    n_loc = kh * width
    q_col = lax.broadcasted_iota(jnp.int32, (width, n_loc), 0)
    k_col = lax.broadcasted_iota(jnp.int32, (width, n_loc), 1) % width
    col_start = jnp.clip(q_col - WIN_COLS // 2, 0, width - WIN_COLS)
    col_mask = (k_col >= col_start) & (k_col < col_start + WIN_COLS)
    low_half = lax.broadcasted_iota(jnp.int32, (width, LANES), 1) < width

    for dr in range(rpb_ref.shape[0]):
        row = jnp.broadcast_to(rpb_ref[pl.ds(dr, 1), :], (width, LANES))
        blo_ref[dr] = pltpu.roll(row, LANES - (WIN_COLS - 1), axis=1, stride=1, stride_axis=0)
        bhi_ref[dr] = pltpu.roll(row, (LANES - (WIN_COLS - 1) + width) % LANES, axis=1, stride=1, stride_axis=0)

    def bias_of(delta):
        cols = []
        for i in range(0, kh, 2):
            dr = i - delta + (WIN_ROWS - 1)
            cols.append(jnp.where(low_half, blo_ref[dr], bhi_ref[dr + 1]))
        return jnp.concatenate(cols, axis=1)

    qb_ref[...] = q_ref[...].astype(_bf16)
    kb_ref[...] = k_ref[...].astype(_bf16)
    vb_ref[...] = v_ref[...].astype(_bf16)
    sctx_ref[...] = _dot_nt(qb_ref[...], kc_ref[...].astype(_bf16)) * scale

    def one_row(r):
        r0 = jnp.clip(r - kh // 2, 0, rows - kh)
        qrows = pl.ds(pl.multiple_of(r * width, width), width)
        win = pl.ds(pl.multiple_of(r0 * width, width), n_loc)
        s_loc = _dot_nt(qb_ref[qrows, :], kb_ref[win, :]) * scale + bias_of(r - r0)
        s_loc = jnp.where(col_mask, s_loc, NEG_INF)
        s_ctx = sctx_ref[qrows, :]
        m = jnp.maximum(s_loc.max(axis=-1, keepdims=True), s_ctx.max(axis=-1, keepdims=True))
        p_loc = jnp.exp(s_loc - m)
        p_ctx = jnp.exp(s_ctx - m)
        den = p_loc.sum(axis=-1, keepdims=True) + p_ctx.sum(axis=-1, keepdims=True)
        pctx_ref[qrows, :] = p_ctx.astype(_bf16)
        oloc_ref[qrows, :] = _dot(p_loc.astype(_bf16), vb_ref[win, :])
        den_ref[qrows, :] = jnp.broadcast_to(den, (width, den_ref.shape[1]))

    def some_rows(g, carry):
        for u in range(unroll):
            one_row(g * unroll + u)
        return carry

    lax.fori_loop(0, rows // unroll, some_rows, 0)
    o = oloc_ref[...] + _dot(pctx_ref[...], vc_ref[...].astype(_bf16))
    o_ref[...] = (o / den_ref[...]).astype(o_ref.dtype)


def _natten(qkv, n_ctx, k_ctx, v_ctx, rpb, dec_batch, dec_seq, d):
    assert n_ctx % dec_seq == 0
    dh = d // N_HEADS
    rows = dec_seq // GRID_W
    kh = min(WIN_ROWS, rows)
    assert 2 * GRID_W == LANES and kh % 2 == 0 and 2 * WIN_COLS - 1 <= LANES
    past = k_ctx.shape[1]
    n_dr = rpb.shape[1]
    rpb = jnp.pad(rpb.astype(_f32), ((0, 0), (0, 0), (0, LANES - rpb.shape[2])))
    b0 = n_ctx // dec_seq
    unroll = next(u for u in (16, 8, 4, 2, 1) if rows % u == 0)
    return pl.pallas_call(
        functools.partial(_natten_kernel, rows=rows, width=GRID_W, kh=kh, scale=dh ** -0.5, unroll=unroll),
        grid=(dec_batch, N_HEADS),
        in_specs=[
            pl.BlockSpec((dec_seq, dh), lambda b, h: (b0 + b, h)),
            pl.BlockSpec((dec_seq, dh), lambda b, h: (b0 + b, N_HEADS + h)),
            pl.BlockSpec((dec_seq, dh), lambda b, h: (b0 + b, 2 * N_HEADS + h)),
            pl.BlockSpec((None, past, dh), lambda b, h: (b, 0, h)),
            pl.BlockSpec((None, past, dh), lambda b, h: (b, 0, h)),
            pl.BlockSpec((None, n_dr, LANES), lambda b, h: (h, 0, 0)),
        ],
        out_specs=pl.BlockSpec((dec_seq, dh), lambda b, h: (b, h)),
        out_shape=jax.ShapeDtypeStruct((dec_batch * dec_seq, d), _bf16),
        scratch_shapes=[pltpu.VMEM((dec_seq, dh), _bf16), pltpu.VMEM((dec_seq, dh), _bf16),
                        pltpu.VMEM((dec_seq, dh), _bf16),
                        pltpu.VMEM((dec_seq, past), _f32), pltpu.VMEM((dec_seq, past), _bf16),
                        pltpu.VMEM((dec_seq, dh), _f32), pltpu.VMEM((dec_seq, dh), _f32),
                        pltpu.VMEM((n_dr, GRID_W, LANES), _f32), pltpu.VMEM((n_dr, GRID_W, LANES), _f32)],
        compiler_params=_params(("parallel", "parallel")),
        name="natten",
    )(qkv, qkv, qkv, k_ctx, v_ctx, rpb)


def _conv_gate_kernel(b_ref, c_ref, u_ref, cp_ref, up_ref, cn_ref, un_ref, w_ref, o_ref,
                      *, n_ctx, seq, dec_seq):
    tm = o_ref.shape[0]
    row0 = pl.program_id(0) * tm
    in_ctx = row0 < n_ctx
    pos0 = jnp.where(in_ctx, row0 % seq, (row0 - n_ctx) % dec_seq)
    seq_len = jnp.where(in_ctx, seq, dec_seq)
    has_prev = pos0 > 0
    has_next = pos0 + tm < seq_len
    cu = c_ref[...] * u_ref[...]
    prev_row = jnp.where(has_prev, cp_ref[pl.ds(SUBLANES - 1, 1), :] * up_ref[pl.ds(SUBLANES - 1, 1), :], 0.0)
    next_row = jnp.where(has_next, cn_ref[pl.ds(0, 1), :] * un_ref[pl.ds(0, 1), :], 0.0)
    ridx = lax.broadcasted_iota(jnp.int32, cu.shape, 0)
    before = jnp.where(ridx == 0, prev_row, pltpu.roll(cu, 1, axis=0))
    after = jnp.where(ridx == tm - 1, next_row, pltpu.roll(cu, tm - 1, axis=0))
    conv = w_ref[pl.ds(0, 1), :] * before + w_ref[pl.ds(1, 1), :] * cu + w_ref[pl.ds(2, 1), :] * after
    o_ref[...] = (b_ref[...] * conv).astype(o_ref.dtype)


def _conv_gate(bcu, conv_w, n_ctx, seq, dec_seq, d):
    t = bcu.shape[0]
    tm = ROW_TILE
    halo = tm // SUBLANES
    last = t // SUBLANES - 1
    cw = jnp.pad(conv_w, ((0, SUBLANES - CONV_WIDTH), (0, 0)))
    prev_map = lambda col: (lambda i: (jnp.maximum(i * halo - 1, 0), col))
    next_map = lambda col: (lambda i: (jnp.minimum((i + 1) * halo, last), col))
    return pl.pallas_call(
        functools.partial(_conv_gate_kernel, n_ctx=n_ctx, seq=seq, dec_seq=dec_seq),
        grid=(t // tm,),
        in_specs=[
            pl.BlockSpec((tm, d), lambda i: (i, 0)),
            pl.BlockSpec((tm, d), lambda i: (i, 1)),
            pl.BlockSpec((tm, d), lambda i: (i, 2)),
            pl.BlockSpec((SUBLANES, d), prev_map(1)),
            pl.BlockSpec((SUBLANES, d), prev_map(2)),
            pl.BlockSpec((SUBLANES, d), next_map(1)),
            pl.BlockSpec((SUBLANES, d), next_map(2)),
            pl.BlockSpec((SUBLANES, d), lambda i: (0, 0)),
        ],
        out_specs=pl.BlockSpec((tm, d), lambda i: (i, 0)),
        out_shape=jax.ShapeDtypeStruct((t, d), _bf16),
        compiler_params=_params(("parallel",)),
        name="conv_gate",
    )(bcu, bcu, bcu, bcu, bcu, bcu, bcu, cw)


def _route(sel, scores):
    n_grp, eg, tm = sel.shape
    n_exp = n_grp * eg
    j_iota = lax.broadcasted_iota(jnp.int32, sel.shape, 1)
    m1 = sel.max(axis=1, keepdims=True)
    j1 = jnp.min(jnp.where(sel == m1, j_iota, eg), axis=1, keepdims=True)
    m2 = jnp.max(jnp.where(j_iota == j1, -jnp.inf, sel), axis=1, keepdims=True)
    grp = m1 + m2
    g_iota = lax.broadcasted_iota(jnp.int32, grp.shape, 0)
    g_sel = g_iota < 0
    for _ in range(TOPK_GROUPS):
        gm = grp.max(axis=0, keepdims=True)
        gi = jnp.min(jnp.where(grp == gm, g_iota, n_grp), axis=0, keepdims=True)
        hit = g_iota == gi
        g_sel = g_sel | hit
        grp = jnp.where(hit, -jnp.inf, grp)
    cur = jnp.where(jnp.broadcast_to(g_sel, sel.shape), sel, NEG_INF)
    e_iota = lax.broadcasted_iota(jnp.int32, sel.shape, 0) * eg + j_iota
    ids, ws, hits = [], [], []
    for _ in range(TOP_K):
        m = cur.max(axis=1, keepdims=True).max(axis=0, keepdims=True)
        ei = jnp.min(jnp.where(cur == m, e_iota, n_exp), axis=1, keepdims=True).min(axis=0, keepdims=True)
        hit = e_iota == ei
        ids.append(ei)
        ws.append(jnp.sum(jnp.where(hit, scores, 0.0), axis=1, keepdims=True).sum(axis=0, keepdims=True))
        hits.append(hit)
        cur = jnp.where(hit, -jnp.inf, cur)
    total = functools.reduce(lambda a, b: a + b, ws)
    ws = [w / total * ROUTED_SCALE for w in ws]
    return ids, ws, hits


def _split_bf16(x):
    hi = x.astype(_bf16)
    return hi, (x - hi.astype(_f32)).astype(_bf16)


def _post_mixer_kernel(*refs, n_a, n_x, n_first):
    a_refs, x_refs = refs[:n_a], refs[n_a:n_a + n_x]
    wo_ref, gpost_ref, mod_ref, gpre_ref, wr_ref, x1_ref, hf_ref, logit_ref = refs[n_a + n_x:]
    o = _dot(_split_read(a_refs, n_first), wo_ref[...])
    x1 = _split_read(x_refs, n_first) + mod_ref[pl.ds(MOD_GATE_MIX, 1), :] * _rms(o, gpost_ref[...])
    x1_ref[...] = x1
    hf = _rms(x1, gpre_ref[...]) * (1.0 + mod_ref[pl.ds(MOD_SCALE_FFN, 1), :]) + mod_ref[pl.ds(MOD_SHIFT_FFN, 1), :]
    hf_ref[...] = hf
    h_hi, h_lo = _split_bf16(hf)
    w_hi, w_lo = _split_bf16(wr_ref[...])
    logit_ref[...] = _dot(h_hi, w_hi) + (_dot(h_hi, w_lo) + _dot(h_lo, w_hi))


def _post_mixer(a_parts, w_out, layer, x_parts, g_post, mod, g_pre, w_router, n_ctx, dec_seq):
    t = sum(x.shape[0] for x in x_parts)
    d = x_parts[0].shape[1]
    tm = ROW_TILE
    e = w_router.shape[1]
    assert e <= LANES
    row = lambda i: (i, 0)
    fixed = lambda i: (0, 0)
    return pl.pallas_call(
        functools.partial(_post_mixer_kernel, n_a=len(a_parts), n_x=len(x_parts), n_first=n_ctx // tm),
        grid=(t // tm,),
        in_specs=_split_specs(a_parts, tm, d) + _split_specs(x_parts, tm, d) + [
            pl.BlockSpec((None, d, d), lambda i: (layer, 0, 0)),
            pl.BlockSpec((1, d), fixed),
            pl.BlockSpec((None, MOD_ROWS, d), lambda i: (_cond_row(i * tm, n_ctx, dec_seq), 0, 0)),
            pl.BlockSpec((1, d), fixed),
            pl.BlockSpec((d, LANES), fixed),
        ],
        out_specs=[pl.BlockSpec((tm, d), row), pl.BlockSpec((tm, d), row), pl.BlockSpec((tm, LANES), row)],
        out_shape=[jax.ShapeDtypeStruct((t, d), _f32), jax.ShapeDtypeStruct((t, d), _f32),
                   jax.ShapeDtypeStruct((t, LANES), _f32)],
        compiler_params=_params(("parallel",)),
        name="post_mixer",
    )(*a_parts, *x_parts, w_out.astype(_bf16), g_post.reshape(1, d), mod, g_pre.reshape(1, d),
      jnp.pad(w_router, ((0, 0), (0, LANES - e))))


def _router_kernel(logit_ref, br_ref, idx_ref, wt_ref, rank_ref, cnt_ref, carry_ref, before_ref):
    i = pl.program_id(0)
    tm = logit_ref.shape[0]
    n_exp = br_ref.shape[0]

    @pl.when(i == 0)
    def _():
        carry_ref[...] = jnp.zeros_like(carry_ref)
        t_src = lax.broadcasted_iota(jnp.int32, (tm, tm), 0)
        t_dst = lax.broadcasted_iota(jnp.int32, (tm, tm), 1)
        before_ref[...] = jnp.where(t_src < t_dst, 1.0, 0.0).astype(_bf16)

    logits = logit_ref[...].T[:n_exp]
    grouped = (N_GROUPS, n_exp // N_GROUPS, tm)
    scores = jax.nn.sigmoid(logits)
    ids, ws, hits = _route((scores + br_ref[...]).reshape(grouped), scores.reshape(grouped))
    for k in range(TOP_K):
        idx_ref[pl.ds(k, 1), :] = ids[k][0]
        wt_ref[pl.ds(k, 1), :] = ws[k][0]

    any_hit = functools.reduce(lambda a, b: a | b, hits)
    mask = jnp.where(any_hit, 1.0, 0.0).reshape(n_exp, tm).astype(_bf16)
    rank = _dot(mask, before_ref[...]) + jnp.concatenate([carry_ref[...]] * (tm // LANES), axis=1)
    rank = rank.reshape(grouped)
    for k in range(TOP_K):
        rk = jnp.sum(jnp.where(hits[k], rank, 0.0), axis=1, keepdims=True).sum(axis=0, keepdims=True)
        rank_ref[pl.ds(k, 1), :] = rk[0].astype(jnp.int32)
    carry_ref[...] += _dot(mask, jnp.ones((tm, LANES), _bf16))
    cnt_ref[...] = carry_ref[...].astype(jnp.int32)


def _router(logits, b_router):
    t = logits.shape[0]
    e = b_router.shape[0]
    tm = min(ROUTER_TILE, t)
    col = lambda i: (0, i)
    fixed = lambda i: (0, 0)
    slots_i = jax.ShapeDtypeStruct((TOP_K, t), jnp.int32)
    return pl.pallas_call(
        _router_kernel,
        grid=(t // tm,),
        in_specs=[pl.BlockSpec((tm, LANES), lambda i: (i, 0)), pl.BlockSpec((e, 1), fixed)],
        out_specs=[pl.BlockSpec((TOP_K, tm), col), pl.BlockSpec((TOP_K, tm), col), pl.BlockSpec((TOP_K, tm), col),
                   pl.BlockSpec((e, LANES), fixed)],
        out_shape=[slots_i, jax.ShapeDtypeStruct((TOP_K, t), _f32), slots_i,
                   jax.ShapeDtypeStruct((e, LANES), jnp.int32)],
        scratch_shapes=[pltpu.VMEM((e, LANES), _f32), pltpu.VMEM((tm, tm), _bf16)],
        compiler_params=_params(("arbitrary",)),
        name="router",
    )(logits, b_router.reshape(e, 1))


def _row(ref, r):
    return ref.at[pl.ds(r, 1), :]


NOT_FIRST = -2
NO_NEXT = -1


def _plan_kernel(cnt_ref, idx_ref, rank_ref, pos_ref, te_ref, tbi_ref, tbo_ref, tv_ref, tnext_ref, trun_ref,
                 padrow_ref, padn_ref, offs_ref, *, tm):
    n_exp = cnt_ref.shape[0]
    n_tiles = te_ref.shape[0]

    def per_expert(e, carry):
        off, tile, run, prev_first = carry
        cnt = cnt_ref[e]
        n_t = (cnt + tm - 1) // tm
        offs_ref[e] = off
        padrow_ref[e] = off + cnt
        padn_ref[e] = n_t * tm - cnt

        def per_tile(j, c):
            te_ref[tile + j] = e
            tbi_ref[tile + j] = tile + j
            tbo_ref[tile + j] = tile + j
            tv_ref[tile + j] = jnp.minimum(cnt - j * tm, tm)
            tnext_ref[tile + j] = jnp.where(j == 0, NO_NEXT, NOT_FIRST)
            trun_ref[tile + j] = run
            return c

        lax.fori_loop(0, n_t, per_tile, 0)
        used = n_t > 0

        @pl.when(used & (prev_first >= 0))
        def _():
            tnext_ref[prev_first] = e

        return (off + n_t * tm, tile + n_t, run + used.astype(jnp.int32), jnp.where(used, tile, prev_first))

    zero = jnp.int32(0)
    _, live, _, _ = lax.fori_loop(0, n_exp, per_expert, (zero, zero, zero, jnp.int32(-1)))
    last = jnp.maximum(live - 1, 0)
    last_expert = te_ref[last]

    def dead_tile(i, c):
        te_ref[i] = last_expert
        tbi_ref[i] = last
        tbo_ref[i] = last
        tv_ref[i] = 0
        tnext_ref[i] = NOT_FIRST
        trun_ref[i] = 0
        return c

    lax.fori_loop(live, n_tiles, dead_tile, 0)

    idx = idx_ref[...]
    pos = rank_ref[...]
    for e in range(n_exp):
        pos = pos + jnp.where(idx == e, offs_ref[e], 0)
    pos_ref[...] = pos


def _plan(counts, idx_t, rank_t, n_tiles):
    n_exp = counts.shape[0]
    smem = pl.BlockSpec(memory_space=pltpu.SMEM)
    vmem = pl.BlockSpec(memory_space=pltpu.VMEM)
    tiles = jax.ShapeDtypeStruct((n_tiles,), jnp.int32)
    experts = jax.ShapeDtypeStruct((n_exp,), jnp.int32)
    return pl.pallas_call(
        functools.partial(_plan_kernel, tm=EXPERT_TILE),
        in_specs=[smem, vmem, vmem],
        out_specs=[vmem] + [smem] * 8,
        out_shape=[jax.ShapeDtypeStruct(idx_t.shape, jnp.int32)] + [tiles] * 6 + [experts] * 2,
        scratch_shapes=[pltpu.SMEM((n_exp,), jnp.int32)],
        name="plan",
    )(counts, idx_t, rank_t)


def _pad_copies(start, n, zero_ref, dst_ref, sem, pad_bits):
    single = n & (SUBLANES - 1)
    for s in range(SUBLANES - 1):
        yield s < single, pltpu.make_async_copy(_row(zero_ref, 0), _row(dst_ref, start + s), sem)
    base = pl.multiple_of(start + single, SUBLANES)
    groups = n // SUBLANES
    for b in range(pad_bits - (SUBLANES.bit_length() - 1)):
        rows = SUBLANES << b
        first = pl.multiple_of(base + ((groups >> (b + 1)) << (b + 1)) * SUBLANES, SUBLANES)
        copy = pltpu.make_async_copy(zero_ref.at[pl.ds(0, rows), :], dst_ref.at[pl.ds(first, rows), :], sem)
        yield ((groups >> b) & 1) == 1, copy


def _slot_specs(n_tokens, tile, step_of):
    per_k = n_tokens // tile
    return [pl.BlockSpec((tile,), lambda i, k=k: (k * per_k + step_of(i),), memory_space=pltpu.SMEM)
            for k in range(TOP_K)]


def _dispatch_kernel(*refs, pad_bits):
    pos_refs = refs[:TOP_K]
    padrow_ref, padn_ref, src_ref, dst_ref, zero_ref, sem, zero_sem = refs[TOP_K:]
    i = pl.program_id(0)
    tile = src_ref.shape[0]
    n_exp = padn_ref.shape[0]

    def for_each_pad_copy(fn):
        def body(e, c):
            for needed, copy in _pad_copies(padrow_ref[e], padn_ref[e], zero_ref, dst_ref, zero_sem, pad_bits):
                pl.when(needed)(functools.partial(fn, copy))
            return c
        lax.fori_loop(0, n_exp, body, 0)

    @pl.when(i == 0)
    def _():
        zero_ref[...] = jnp.zeros_like(zero_ref)
        for_each_pad_copy(lambda copy: copy.start())

    def issue(g, carry):
        for t in (2 * g, 2 * g + 1):
            for k in range(TOP_K):
                pltpu.make_async_copy(_row(src_ref, t), _row(dst_ref, pos_refs[k][t]), sem).start(priority=k % 2)
        return carry

    lax.fori_loop(0, tile // 2, issue, 0)
    for k in range(TOP_K):
        pltpu.make_async_copy(src_ref, dst_ref.at[pl.ds(0, tile), :], sem).wait()

    @pl.when(i == 0)
    def _():
        for_each_pad_copy(lambda copy: copy.wait())


def _dispatch(hf, pos, padrow, padn, n_sorted):
    t, d = hf.shape
    tile = min(SCATTER_TILE, t)
    pad_bits = (EXPERT_TILE - 1).bit_length()
    smem = pl.BlockSpec(memory_space=pltpu.SMEM)
    return pl.pallas_call(
        functools.partial(_dispatch_kernel, pad_bits=pad_bits),
        grid=(t // tile,),
        in_specs=_slot_specs(t, tile, lambda i: i) + [
            smem, smem,
            pl.BlockSpec((tile, d), lambda i: (i, 0)),
        ],
        out_specs=pl.BlockSpec(memory_space=pl.ANY),
        out_shape=jax.ShapeDtypeStruct((n_sorted, d), _f32),
        scratch_shapes=[pltpu.VMEM((1 << (pad_bits - 1), d), _f32),
                        pltpu.SemaphoreType.DMA(()), pltpu.SemaphoreType.DMA(())],
        compiler_params=_params(("arbitrary",)),
        name="dispatch",
    )(*([pos] * TOP_K), padrow, padn, hf)


def _expert_kernel(te_ref, tbi_ref, tbo_ref, tv_ref, tnext_ref, trun_ref, x_ref, wg_hbm, wu_hbm, wd_hbm, o_ref,
                   wg_ref, wu_ref, wd_ref, wgb_ref, wub_ref, wdb_ref, sems, *, layer):
    i = pl.program_id(0)
    live = tv_ref[i] > 0
    slot = trun_ref[i] % 2

    def fetch(expert, s):
        return [pltpu.make_async_copy(src.at[layer, expert], dst.at[s], sems.at[s])
                for src, dst in ((wg_hbm, wg_ref), (wu_hbm, wu_ref), (wd_hbm, wd_ref))]

    @pl.when(i == 0)
    def _():
        for copy in fetch(te_ref[0], 0):
            copy.start()

    @pl.when(live & (tnext_ref[i] != NOT_FIRST))
    def _():
        @pl.when(tnext_ref[i] != NO_NEXT)
        def _():
            for copy in fetch(tnext_ref[i], 1 - slot):
                copy.start()

        for copy in fetch(te_ref[i], slot):
            copy.wait()
        wgb_ref[...] = wg_ref[slot].astype(_bf16)
        wub_ref[...] = wu_ref[slot].astype(_bf16)
        wdb_ref[...] = wd_ref[slot].astype(_bf16)

    @pl.when(live)
    def _():
        x = x_ref[...].astype(_bf16)
        g = _dot(x, wgb_ref[...])
        u = _dot(x, wub_ref[...])
        h = (g * jax.nn.sigmoid(g) * u).astype(_bf16)
        o_ref[...] = _dot(h, wdb_ref[...])


def _experts(xs, w_gate, w_up, w_down, layer, tile_expert, tile_in, tile_out, tile_valid, tile_next, tile_run):
    _, n_exp, d, f = w_gate.shape
    tm = EXPERT_TILE
    n_tiles = tile_expert.shape[0]
    hbm = pl.BlockSpec(memory_space=pl.ANY)
    return pl.pallas_call(
        functools.partial(_expert_kernel, layer=layer),
        grid_spec=pltpu.PrefetchScalarGridSpec(
            num_scalar_prefetch=6,
            grid=(n_tiles,),
            in_specs=[pl.BlockSpec((tm, d), lambda i, te, tbi, tbo, tv, tn, tr: (tbi[i], 0)), hbm, hbm, hbm],
            out_specs=pl.BlockSpec((tm, d), lambda i, te, tbi, tbo, tv, tn, tr: (tbo[i], 0)),
            scratch_shapes=[pltpu.VMEM((2, d, f), _f32), pltpu.VMEM((2, d, f), _f32), pltpu.VMEM((2, f, d), _f32),
                            pltpu.VMEM((d, f), _bf16), pltpu.VMEM((d, f), _bf16), pltpu.VMEM((f, d), _bf16),
                            pltpu.SemaphoreType.DMA((2,))],
        ),
        out_shape=jax.ShapeDtypeStruct(xs.shape, _f32),
        compiler_params=_params(("arbitrary",)),
        name="experts",
    )(tile_expert, tile_in, tile_out, tile_valid, tile_next, tile_run, xs, w_gate, w_up, w_down)


def _combine_kernel(*refs, n_first):
    pos_refs, posn_refs = refs[:TOP_K], refs[TOP_K:2 * TOP_K]
    ys_ref, wt_ref, sh_ref, x1_ref, gpost_ref, mod_ref = refs[2 * TOP_K:2 * TOP_K + 6]
    rest = refs[2 * TOP_K + 6:]
    final = n_first is not None
    if final:
        yctx_ref, ylat_ref, buf0_ref, buf1_ref, f_ref, sems = rest
    else:
        gnext_ref, modn_ref, x2_ref, hn_ref, buf0_ref, buf1_ref, f_ref, sems = rest
    tm, d = x1_ref.shape
    i = pl.program_id(0)
    last = pl.num_programs(0) - 1

    def start_rows(p_refs, buf_ref, sem, t):
        for k in range(TOP_K):
            pltpu.make_async_copy(_row(ys_ref, p_refs[k][t]), buf_ref.at[k, pl.ds(t, 1), :],
                                  sem).start(priority=k % 2)

    def wait_tile(buf_ref, sem):
        for k in range(TOP_K):
            pltpu.make_async_copy(ys_ref.at[pl.ds(0, tm), :], buf_ref.at[k], sem).wait()

    @pl.when(i == 0)
    def _():
        def first(t, carry):
            start_rows(pos_refs, buf0_ref, sems.at[0], t)
            return carry
        lax.fori_loop(0, tm, first, 0)

    def step(cur_ref, cur_sem, nxt_ref, nxt_sem):
        wait_tile(cur_ref, cur_sem)

        def group(g, carry):
            for u in range(SUBLANES):
                start_rows(posn_refs, nxt_ref, nxt_sem, g * SUBLANES + u)
            rows = pl.ds(pl.multiple_of(g * SUBLANES, SUBLANES), SUBLANES)
            w = wt_ref[rows, :]
            wb = [jnp.broadcast_to(w[:, k:k + 1], (SUBLANES, LANES)) for k in range(TOP_K)]
            for j in range(d // LANES):
                cols = slice(j * LANES, (j + 1) * LANES)
                acc = sh_ref[rows, cols]
                for k in range(TOP_K):
                    acc = acc + wb[k] * cur_ref[k, rows, cols]
                f_ref[rows, cols] = acc
            return carry

        lax.fori_loop(0, tm // SUBLANES, group, 0)

        @pl.when(i == last)
        def _():
            wait_tile(nxt_ref, nxt_sem)

    pl.when(i % 2 == 0)(functools.partial(step, buf0_ref, sems.at[0], buf1_ref, sems.at[1]))
    pl.when(i % 2 == 1)(functools.partial(step, buf1_ref, sems.at[1], buf0_ref, sems.at[0]))
    f = f_ref[...]
    x2 = x1_ref[...] + mod_ref[pl.ds(MOD_GATE_FFN, 1), :] * _rms(f, gpost_ref[...])
    if final:
        in_ctx = pl.program_id(0) < n_first

        @pl.when(in_ctx)
        def _():
            yctx_ref[...] = x2

        @pl.when(jnp.logical_not(in_ctx))
        def _():
            ylat_ref[...] = x2
    else:
        x2_ref[...] = x2
        hn = (_rms(x2, gnext_ref[...]) * (1.0 + modn_ref[pl.ds(MOD_SCALE_MIX, 1), :])
              + modn_ref[pl.ds(MOD_SHIFT_MIX, 1), :])
        hn_ref[...] = hn.astype(hn_ref.dtype)


def _combine(ys, pos, wt, shared, x1, g_post, mod, n_ctx, dec_seq, g_next=None, mod_next=None):
    t, d = x1.shape
    tm = min(GATHER_TILE, t)
    n_steps = t // tm
    final = g_next is None
    row = lambda i: (i, 0)
    fixed = lambda i: (0, 0)
    cond = lambda i: (_cond_row(i * tm, n_ctx, dec_seq), 0, 0)
    in_specs = _slot_specs(t, tm, lambda i: i) + _slot_specs(t, tm, lambda i: jnp.minimum(i + 1, n_steps - 1)) + [
        pl.BlockSpec(memory_space=pl.ANY),
        pl.BlockSpec((tm, TOP_K), row),
        pl.BlockSpec((tm, d), row),
        pl.BlockSpec((tm, d), row),
        pl.BlockSpec((1, d), fixed),
        pl.BlockSpec((None, MOD_ROWS, d), cond),
    ]
    args = [pos] * (2 * TOP_K) + [ys, wt, shared, x1, g_post.reshape(1, d), mod]
    if final:
        n0, n1 = n_ctx // tm, (t - n_ctx) // tm
        out_specs = [pl.BlockSpec((tm, d), lambda i: (jnp.minimum(i, n0 - 1), 0)),
                     pl.BlockSpec((tm, d), lambda i: (jnp.clip(i - n0, 0, n1 - 1), 0))]
        out_shape = [jax.ShapeDtypeStruct((n_ctx, d), _f32), jax.ShapeDtypeStruct((t - n_ctx, d), _f32)]
    else:
        in_specs += [pl.BlockSpec((1, d), fixed), pl.BlockSpec((None, MOD_ROWS, d), cond)]
        args += [g_next.reshape(1, d), mod_next]
        out_specs = [pl.BlockSpec((tm, d), row), pl.BlockSpec((tm, d), row)]
        out_shape = [jax.ShapeDtypeStruct((t, d), _f32), jax.ShapeDtypeStruct((t, d), _bf16)]
    return pl.pallas_call(
        functools.partial(_combine_kernel, n_first=n_ctx // tm if final else None),
        grid=(n_steps,),
        in_specs=in_specs,
        out_specs=out_specs,
        out_shape=out_shape,
        scratch_shapes=[pltpu.VMEM((TOP_K, tm, d), _f32), pltpu.VMEM((TOP_K, tm, d), _f32), pltpu.VMEM((tm, d), _f32),
                        pltpu.SemaphoreType.DMA((2,))],
        compiler_params=_params(("arbitrary",)),
        name="combine",
    )(*args)


def _moe(hf, idx_t, wt_t, rank_t, cnt, layer, w_gate, w_up, w_down, w_sh_gate, w_sh_up, w_sh_down):
    n_exp, d = w_gate.shape[1:3]
    t = hf.shape[0]
    tm = EXPERT_TILE
    n_tiles = t * TOP_K // tm + n_exp
    pos, *tiles, padrow, padn = _plan(cnt[:, 0], idx_t, rank_t, n_tiles)
    pos = pos.reshape(-1)
    xs = _dispatch(hf, pos, padrow, padn, n_tiles * tm)
    ys = _experts(xs, w_gate, w_up, w_down, layer, *tiles)
    n_sh = t // tm
    every = jnp.arange(n_sh, dtype=jnp.int32)
    zeros = jnp.zeros((n_sh,), jnp.int32)
    shared = _experts(hf, w_sh_gate[:, None], w_sh_up[:, None], w_sh_down[:, None], layer,
                      zeros, every, every, jnp.full((n_sh,), tm, jnp.int32),
                      jnp.full((n_sh,), NOT_FIRST, jnp.int32).at[0].set(NO_NEXT), zeros)
    return ys, pos, wt_t.T, shared


def kernel(x_prompt, x_sample, cache_k, cache_v, c, c_ctx, w_ada, b_ada, norm_mix_pre, norm_mix_post, norm_ffn_pre, norm_ffn_post, w_qkv, w_o_attn, rpb, w_conv_in, conv_w, w_conv_out, w_router, b_router, w_exp_gate, w_exp_up, w_exp_down, w_sh_gate, w_sh_up, w_sh_down):
    batch, seq, d = x_prompt.shape
    dec_batch, dec_seq, _ = x_sample.shape
    depth = w_ada.shape[0]
    n_ctx, n_lat = batch * seq, dec_batch * dec_seq
    dh = d // N_HEADS
    past = cache_k.shape[2]

    x = [x_prompt.reshape(n_ctx, d), x_sample.reshape(n_lat, d)]
    cond = jnp.concatenate([c_ctx[None, :], c], axis=0)
    mod = _adaln(cond, w_ada, b_ada)

    new_k, new_v = [], []
    h = _modulate(x, norm_mix_pre[0], mod[0], n_ctx, dec_seq)
    for l in range(depth):
        if l % 2 == 0:
            a = l // 2
            qkv = _matmul(h, w_qkv, a)
            o_ctx, k_new, v_new = _ctx_attention(qkv, batch, seq, d)
            new_k.append(k_new.reshape(batch, seq, N_HEADS, dh))
            new_v.append(v_new.reshape(batch, seq, N_HEADS, dh))
            o_lat = _natten(qkv, n_ctx, cache_k[:, a].reshape(dec_batch, past, d),
                            cache_v[:, a].reshape(dec_batch, past, d), rpb[a], dec_batch, dec_seq, d)
            mixed, w_out, w_layer = [o_ctx, o_lat], w_o_attn, a
        else:
            m = l // 2
            mixed = [_conv_gate(_matmul(h, w_conv_in, m), conv_w[m], n_ctx, seq, dec_seq, d)]
            w_out, w_layer = w_conv_out, m
        x1, hf, logits = _post_mixer(
            mixed, w_out, w_layer, x, norm_mix_post[l], mod[l], norm_ffn_pre[l], w_router[l], n_ctx, dec_seq)
        idx_t, wt_t, rank_t, cnt = _router(logits, b_router[l])
        ys, pos, wt, shared = _moe(hf, idx_t, wt_t, rank_t, cnt, l, w_exp_gate, w_exp_up, w_exp_down,
                                   w_sh_gate, w_sh_up, w_sh_down)
        if l + 1 < depth:
            x2, h = _combine(ys, pos, wt, shared, x1, norm_ffn_post[l], mod[l], n_ctx, dec_seq,
                             norm_mix_pre[l + 1], mod[l + 1])
            x = [x2]
        else:
            y_ctx, y_lat = _combine(ys, pos, wt, shared, x1, norm_ffn_post[l], mod[l], n_ctx, dec_seq)

    return (y_ctx.reshape(batch, seq, d), y_lat.reshape(dec_batch, dec_seq, d),
            jnp.stack(new_k, axis=1), jnp.stack(new_v, axis=1))
```
